```python
import math
import jax, jax.numpy as jnp
from jax import lax
import numpy as np

D_MODEL = 2048
BATCH = 4
SEQ = 2048
DEPTH = 4
DEC_BATCH = 128
DEC_SEQ = 8
PAST_LEN = 16384
PAGE_SIZE = 128

N_META = 16
CHUNK = 64
NORM_EPS = 1e-6

RW_HEADS = 8
RW_HEAD = 64
RW_W = RW_HEADS * RW_HEAD
RW_DECAY_RANK = 64
RW_A_RANK = 64
RW_GATE_RANK = 128
RW_COLS = 3 * RW_W + RW_DECAY_RANK + RW_A_RANK + RW_GATE_RANK
RW_LN_EPS = 64e-5
RW_SPLITS = (RW_W, 2 * RW_W, 3 * RW_W, 3 * RW_W + RW_DECAY_RANK, 3 * RW_W + RW_DECAY_RANK + RW_A_RANK)

RET_HEADS = 4
RET_DK = 64
RET_DV = 128
RET_W = RET_HEADS * RET_DV
RET_COLS = 2 * RET_HEADS * RET_DK + 2 * RET_W
RET_SPLITS = (RET_HEADS * RET_DK, 2 * RET_HEADS * RET_DK, 2 * RET_HEADS * RET_DK + RET_W)
ROPE_BASE = 10000.0

HG_HEADS = 4
HG_DK = 128
HG_DV = 128
HG_W = HG_HEADS * HG_DV
HG_COLS = 2 * HG_HEADS * HG_DK + 2 * HG_W
HG_SPLITS = (HG_HEADS * HG_DK, 2 * HG_HEADS * HG_DK, 2 * HG_HEADS * HG_DK + HG_W)

GD_HEADS = 4
GD_DK = 128
GD_DV = 128
GD_W = GD_HEADS * GD_DV
GD_CONV = 4
GD_CONV_COLS = 2 * GD_HEADS * GD_DK + GD_W
GD_COLS = GD_CONV_COLS + 2 * GD_HEADS + GD_W
GD_SPLITS = (GD_CONV_COLS, GD_CONV_COLS + GD_HEADS, GD_CONV_COLS + 2 * GD_HEADS)
GD_QKV_SPLITS = (GD_HEADS * GD_DK, 2 * GD_HEADS * GD_DK)

N_BRANCH = 4
BR_W = 512
GATE_COLS = N_BRANCH * D_MODEL
IN_COLS = RW_COLS + RET_COLS + HG_COLS + GD_COLS + GATE_COLS
IN_SPLITS = (RW_COLS, RW_COLS + RET_COLS, RW_COLS + RET_COLS + HG_COLS, RW_COLS + RET_COLS + HG_COLS + GD_COLS)
FF = ((8 * D_MODEL // 3 + 255) // 256) * 256

kernel_name = 'hybrid_gated_branch_decoder_step'


def _rmsnorm(x, g):
    xf = x.astype(jnp.float32)
    y = xf * lax.rsqrt(jnp.mean(xf * xf, axis=-1, keepdims=True) + NORM_EPS)
    return (y * g.astype(jnp.float32)).astype(x.dtype)


def _rms_heads(x):
    return x * lax.rsqrt(jnp.mean(x * x, axis=-1, keepdims=True) + NORM_EPS)


def _groupnorm(x, g, b):
    mu = jnp.mean(x, axis=-1, keepdims=True)
    var = jnp.mean(jnp.square(x - mu), axis=-1, keepdims=True)
    return (x - mu) * lax.rsqrt(var + RW_LN_EPS) * g + b


def _l2norm(x):
    return x * lax.rsqrt(jnp.sum(x * x, axis=-1, keepdims=True) + 1e-6)


def _rotary(x, pos):
    half = x.shape[-1] // 2
    inv = ROPE_BASE ** (-jnp.arange(half, dtype=jnp.float32) / half)
    ang = pos.astype(jnp.float32)[:, None] * inv[None, :]
    cos = jnp.cos(ang)[None, :, None, :]
    sin = jnp.sin(ang)[None, :, None, :]
    x1, x2 = x[..., :half], x[..., half:]
    return jnp.concatenate([x1 * cos - x2 * sin, x1 * sin + x2 * cos], axis=-1)


def _masked_exp(diff, mask):
    return jnp.where(mask, jnp.exp(jnp.where(mask, diff, 0.0)), 0.0)


def _rwkv7_scan(S, r, w, k, v, a, b):
    def step(S, inp):
        r_t, w_t, k_t, v_t, a_t, b_t = inp
        sa = jnp.einsum('bhvk,bhk->bhv', S, a_t)
        S = S * w_t[:, :, None, :] + sa[..., None] * b_t[:, :, None, :] + v_t[..., None] * k_t[:, :, None, :]
        return S, jnp.einsum('bhvk,bhk->bhv', S, r_t)
    S, o = lax.scan(step, S, tuple(jnp.moveaxis(z, 1, 0) for z in (r, w, k, v, a, b)))
    return S, jnp.moveaxis(o, 0, 1)


def _decay_chunk(S, q, k, v, g):
    C = q.shape[2]
    b = jnp.cumsum(g, axis=2)
    causal = jnp.tril(jnp.ones((C, C), dtype=bool))
    dmat = _masked_exp(b[..., :, None] - b[..., None, :], causal)
    scores = jnp.einsum('bhtd,bhsd->bhts', q, k) * dmat
    o = jnp.einsum('bhts,bhsv->bhtv', scores, v) + jnp.exp(b)[..., None] * jnp.einsum('bhtd,bhdv->bhtv', q, S)
    b_last = b[..., -1:]
    S_new = jnp.exp(b_last)[..., None] * S + jnp.einsum('bhsd,bhsv->bhdv', k * jnp.exp(b_last - b)[..., None], v)
    return S_new, o


def _gla_chunk(S, q, k, v, g):
    C = q.shape[2]
    b = jnp.cumsum(g, axis=2)
    causal = jnp.tril(jnp.ones((C, C), dtype=bool))[None, None, :, :, None]
    dec = _masked_exp(b[:, :, :, None, :] - b[:, :, None, :, :], causal)
    scores = jnp.einsum('bhtsd,bhsd->bhts', q[:, :, :, None, :] * dec, k)
    o = jnp.einsum('bhts,bhsv->bhtv', scores, v) + jnp.einsum('bhtd,bhdv->bhtv', q * jnp.exp(b), S)
    b_last = b[:, :, -1:]
    S_new = jnp.exp(b_last[:, :, 0])[..., None] * S + jnp.einsum('bhsd,bhsv->bhdv', k * jnp.exp(b_last - b), v)
    return S_new, o


def _delta_chunk(S, q, k, v, beta, g):
    C = q.shape[2]
    dk = q.shape[-1]
    b = jnp.cumsum(g, axis=2)
    causal = jnp.tril(jnp.ones((C, C), dtype=bool))
    strict = jnp.tril(jnp.ones((C, C), dtype=bool), -1)
    dmat = _masked_exp(b[..., :, None] - b[..., None, :], causal)
    A = beta[..., :, None] * jnp.einsum('bhtd,bhsd->bhts', k, k) * jnp.where(strict, dmat, 0.0)
    lhs = jnp.eye(C, dtype=A.dtype) + A
    rhs = jnp.concatenate([(beta * jnp.exp(b))[..., None] * k, beta[..., None] * v], axis=-1)
    sol = lax.linalg.triangular_solve(lhs, rhs, left_side=True, lower=True, unit_diagonal=True)
    delta = sol[..., dk:] - jnp.einsum('bhtd,bhdv->bhtv', sol[..., :dk], S)
    o = jnp.exp(b)[..., None] * jnp.einsum('bhtd,bhdv->bhtv', q, S) + jnp.einsum('bhts,bhsv->bhtv', jnp.einsum('bhtd,bhsd->bhts', q, k) * dmat, delta)
    b_last = b[..., -1:]
    S_new = jnp.exp(b_last)[..., None] * S + jnp.einsum('bhsd,bhsv->bhdv', k * jnp.exp(b_last - b)[..., None], delta)
    return S_new, o


def _chunked_scan(chunk_fn, state, xs, n_lead):
    T = xs[0].shape[2]
    outs = []
    if n_lead > 0:
        state, o = chunk_fn(state, *(z[:, :, :n_lead] for z in xs))
        outs.append(o)
    n_chunks = (T - n_lead) // CHUNK
    if n_chunks > 0:
        blocks = tuple(jnp.moveaxis(z[:, :, n_lead:].reshape(z.shape[:2] + (n_chunks, CHUNK) + z.shape[3:]), 2, 0) for z in xs)
        state, o = lax.scan(lambda s, blk: chunk_fn(s, *blk), state, blocks)
        o = jnp.moveaxis(o, 0, 2)
        outs.append(o.reshape(o.shape[:2] + (n_chunks * CHUNK,) + o.shape[4:]))
    out = jnp.concatenate(outs, axis=2) if len(outs) > 1 else outs[0]
    return state, out


def _layer(x, pos, states, P):
    st_wkv, st_shift, st_ret, st_hg, st_gd, st_conv = states
    f32 = jnp.float32
    bn, t_len, _ = x.shape
    n_lead = t_len % CHUNK
    h = _rmsnorm(x, P['norm_mix'])
    proj = (h @ P['w_in']).astype(f32)
    pa, pb, pc, pd, pg = jnp.split(proj, IN_SPLITS, axis=-1)

    def heads(z, n_h):
        return z.reshape(bn, t_len, n_h, -1)

    def bh(z):
        return jnp.swapaxes(z, 1, 2)

    prev = jnp.concatenate([st_shift.astype(f32)[:, None], pa[:, :-1]], axis=1)
    xm = pa + (prev - pa) * P['rw_mu']
    r, k, v, wl, al, gl = jnp.split(xm, RW_SPLITS, axis=-1)
    w_log = -jax.nn.softplus(-(P['rw_w0'] + jnp.tanh(wl) @ P['rw_w2'])) - 0.5
    decay = jnp.exp(-jnp.exp(w_log))
    a = jax.nn.sigmoid(P['rw_a0'] + al @ P['rw_a2'])
    g_a = jax.nn.sigmoid(gl) @ P['rw_g2']
    kk = _l2norm(heads(k * P['rw_kk'], RW_HEADS))
    k = k * (1.0 + (a - 1.0) * P['rw_ka'])
    r, k, v, decay, a = (heads(z, RW_HEADS) for z in (r, k, v, decay, a))
    wkv, oa = _rwkv7_scan(st_wkv.astype(f32), r, decay, k, v, -kk, kk * a)
    oa = _groupnorm(oa, P['rw_ln_g'].reshape(RW_HEADS, RW_HEAD), P['rw_ln_b'].reshape(RW_HEADS, RW_HEAD))
    oa = oa + jnp.sum(r * k * P['rw_rk'], axis=-1, keepdims=True) * v
    oa = oa.reshape(bn, t_len, RW_W) * g_a

    q, k, v, g_b = jnp.split(pb, RET_SPLITS, axis=-1)
    q = _rotary(heads(q, RET_HEADS), pos)
    k = _rotary(heads(k, RET_HEADS), pos) * (RET_DK ** -0.5)
    log_gamma = jnp.log1p(-jnp.exp2(-5.0 - jnp.arange(RET_HEADS, dtype=f32)))
    g = jnp.broadcast_to(log_gamma[None, :, None], (bn, RET_HEADS, t_len))
    ret, ob = _chunked_scan(_decay_chunk, st_ret.astype(f32), (bh(q), bh(k), bh(heads(v, RET_HEADS)), g), n_lead)
    ob = _rms_heads(bh(ob)).reshape(bn, t_len, RET_W) * jax.nn.silu(g_b)

    q, f, i, g_c = jnp.split(pc, HG_SPLITS, axis=-1)
    lb = P['hg_lb']
    log_f = jnp.log(lb + (1.0 - lb) * jax.nn.sigmoid(f))
    k_in = (1.0 - lb) * jax.nn.sigmoid(-f)
    hg, oc = _chunked_scan(_gla_chunk, st_hg.astype(f32),
                           (bh(heads(jax.nn.silu(q), HG_HEADS)), bh(heads(k_in, HG_HEADS)),
                            bh(heads(i, HG_HEADS)), bh(heads(log_f, HG_HEADS))), n_lead)
    oc = (_rms_heads(bh(oc)) * P['hg_norm_g'].reshape(HG_HEADS, HG_DV)).reshape(bn, t_len, HG_W) * jax.nn.silu(g_c)

    qkv_in, beta_in, alpha_in, g_d = jnp.split(pd, GD_SPLITS, axis=-1)
    full = jnp.concatenate([st_conv.astype(f32), qkv_in], axis=1)
    conv = full[:, :t_len] * P['gd_conv'][0]
    for j in range(1, GD_CONV):
        conv = conv + full[:, j:j + t_len] * P['gd_conv'][j]
    q, k, v = jnp.split(jax.nn.silu(conv), GD_QKV_SPLITS, axis=-1)
    q = _l2norm(heads(q, GD_HEADS)) * (GD_DK ** -0.5)
    k = _l2norm(heads(k, GD_HEADS))
    beta = jax.nn.sigmoid(beta_in)
    g = -jnp.exp(P['gd_a_log']) * jax.nn.softplus(alpha_in + P['gd_dt_bias'])
    gd, od = _chunked_scan(_delta_chunk, st_gd.astype(f32),
                           (bh(q), bh(k), bh(heads(v, GD_HEADS)), bh(beta), bh(g)), n_lead)
    od = (_rms_heads(bh(od)) * P['gd_norm_g'].reshape(GD_HEADS, GD_DV)).reshape(bn, t_len, GD_W) * jax.nn.silu(g_d)
    new_conv = full[:, t_len:]

    gates = jax.nn.sigmoid(pg).reshape(bn, t_len, N_BRANCH, D_MODEL)
    merged = jnp.zeros((bn, t_len, D_MODEL), f32)
    for n, o_n in enumerate((oa, ob, oc, od)):
        merged = merged + gates[:, :, n] * (o_n @ P['w_branch'][n])
    x = x + (merged @ P['w_out']).astype(x.dtype)

    h2 = _rmsnorm(x, P['norm_ffn'])
    up, gate = jnp.split(h2 @ P['w_up'], [FF], axis=-1)
    x = x + ((jax.nn.silu(gate) * up) @ P['w_down']).astype(x.dtype)
    return x, (wkv, pa[:, -1], ret, hg, gd, new_conv)


def setup_inputs(seed: int = 0) -> dict:
    key = jax.random.key(seed)
    ks = jax.random.split(key, 40)
    f32 = jnp.float32

    def nrm(i, shape, scale):
        return scale * jax.random.normal(ks[i], shape, f32)

    def gain(i, shape):
        return 1.0 + 0.05 * jax.random.normal(ks[i], shape, f32)

    dt = jnp.exp(jax.random.uniform(ks[30], (DEPTH, GD_HEADS), f32, math.log(1e-3), math.log(1e-1)))
    return {
        'x_prompt': nrm(0, (BATCH, SEQ, D_MODEL), 1.0),
        'x_sample': nrm(1, (DEC_BATCH, DEC_SEQ, D_MODEL), 1.0),
        'state_rwkv_wkv': nrm(2, (DEPTH, DEC_BATCH, RW_HEADS, RW_HEAD, RW_HEAD), 0.5),
        'state_rwkv_shift': nrm(3, (DEPTH, DEC_BATCH, RW_COLS), 1.0),
        'state_ret': nrm(4, (DEPTH, DEC_BATCH, RET_HEADS, RET_DK, RET_DV), 0.5),
        'state_hgrn': nrm(5, (DEPTH, DEC_BATCH, HG_HEADS, HG_DK, HG_DV), 0.3),
        'state_gdn': nrm(6, (DEPTH, DEC_BATCH, GD_HEADS, GD_DK, GD_DV), 0.3),
        'state_gdn_conv': nrm(7, (DEPTH, DEC_BATCH, GD_CONV - 1, GD_CONV_COLS), 1.0),
        'meta_tokens': nrm(8, (N_META, D_MODEL), 1.0),
        'norm_mix': gain(9, (DEPTH, D_MODEL)),
        'w_in': nrm(10, (DEPTH, D_MODEL, IN_COLS), D_MODEL ** -0.5),
        'rw_mu': jax.random.uniform(ks[11], (DEPTH, RW_COLS), f32),
        'rw_w0': -1.0 + nrm(12, (DEPTH, RW_W), 0.5),
        'rw_w2': nrm(13, (DEPTH, RW_DECAY_RANK, RW_W), 0.5 * RW_DECAY_RANK ** -0.5),
        'rw_a0': nrm(14, (DEPTH, RW_W), 0.1),
        'rw_a2': nrm(15, (DEPTH, RW_A_RANK, RW_W), 0.5 * RW_A_RANK ** -0.5),
        'rw_g2': nrm(16, (DEPTH, RW_GATE_RANK, RW_W), RW_GATE_RANK ** -0.5),
        'rw_kk': 0.85 + nrm(17, (DEPTH, RW_W), 0.05),
        'rw_ka': gain(18, (DEPTH, RW_W)),
        'rw_rk': nrm(19, (DEPTH, RW_HEADS, RW_HEAD), 0.1),
        'rw_ln_g': gain(20, (DEPTH, RW_W)),
        'rw_ln_b': nrm(21, (DEPTH, RW_W), 0.02),
        'hg_lb': nrm(22, (DEPTH, HG_HEADS * HG_DK), 1.0),
        'hg_norm_g': gain(23, (DEPTH, HG_W)),
        'gd_conv': nrm(24, (DEPTH, GD_CONV, GD_CONV_COLS), GD_CONV ** -0.5),
        'gd_a_log': jnp.log(jax.random.uniform(ks[25], (DEPTH, GD_HEADS), f32, 1.0, 16.0)),
        'gd_dt_bias': dt + jnp.log(-jnp.expm1(-dt)),
        'gd_norm_g': gain(26, (DEPTH, GD_W)),
        'w_branch': nrm(27, (DEPTH, N_BRANCH, BR_W, D_MODEL), BR_W ** -0.5),
        'w_out': nrm(28, (DEPTH, D_MODEL, D_MODEL), D_MODEL ** -0.5),
        'norm_ffn': gain(29, (DEPTH, D_MODEL)),
        'w_up': nrm(31, (DEPTH, D_MODEL, 2 * FF), D_MODEL ** -0.5),
        'w_down': nrm(32, (DEPTH, FF, D_MODEL), FF ** -0.5),
        'norm_final': gain(33, (D_MODEL,)),
    }


def reference(x_prompt, x_sample, state_rwkv_wkv, state_rwkv_shift, state_ret, state_hgrn, state_gdn, state_gdn_conv,
              meta_tokens, norm_mix, w_in, rw_mu, rw_w0, rw_w2, rw_a0, rw_a2, rw_g2, rw_kk, rw_ka, rw_rk,
              rw_ln_g, rw_ln_b, hg_lb, hg_norm_g, gd_conv, gd_a_log, gd_dt_bias, gd_norm_g,
              w_branch, w_out, norm_ffn, w_up, w_down, norm_final):
    f32 = jnp.float32
    lb_soft = jax.nn.softmax(hg_lb.astype(f32), axis=0)
    lb_all = jnp.cumsum(lb_soft, axis=0) - lb_soft[0]

    bp = x_prompt.shape[0]
    meta = jnp.broadcast_to(meta_tokens.astype(x_prompt.dtype)[None], (bp, N_META, D_MODEL))
    xp = jnp.concatenate([meta, x_prompt], axis=1)
    xs = x_sample
    pos_p = jnp.arange(xp.shape[1], dtype=jnp.int32)
    pos_s = PAST_LEN + jnp.arange(xs.shape[1], dtype=jnp.int32)

    sample_states = (state_rwkv_wkv, state_rwkv_shift, state_ret, state_hgrn, state_gdn, state_gdn_conv)
    new_p = [[] for _ in sample_states]
    new_s = [[] for _ in sample_states]
    for l in range(DEPTH):
        P = {
            'norm_mix': norm_mix[l], 'w_in': w_in[l],
            'rw_mu': rw_mu[l], 'rw_w0': rw_w0[l], 'rw_w2': rw_w2[l], 'rw_a0': rw_a0[l], 'rw_a2': rw_a2[l],
            'rw_g2': rw_g2[l], 'rw_kk': rw_kk[l], 'rw_ka': rw_ka[l], 'rw_rk': rw_rk[l],
            'rw_ln_g': rw_ln_g[l], 'rw_ln_b': rw_ln_b[l],
            'hg_lb': lb_all[l], 'hg_norm_g': hg_norm_g[l],
            'gd_conv': gd_conv[l], 'gd_a_log': gd_a_log[l], 'gd_dt_bias': gd_dt_bias[l], 'gd_norm_g': gd_norm_g[l],
            'w_branch': w_branch[l], 'w_out': w_out[l],
            'norm_ffn': norm_ffn[l], 'w_up': w_up[l], 'w_down': w_down[l],
        }
        zero_states = tuple(jnp.zeros((bp,) + s.shape[2:], s.dtype) for s in sample_states)
        xp, st_p = _layer(xp, pos_p, zero_states, P)
        xs, st_s = _layer(xs, pos_s, tuple(s[l] for s in sample_states), P)
        for i in range(len(sample_states)):
            new_p[i].append(st_p[i].astype(sample_states[i].dtype))
            new_s[i].append(st_s[i].astype(sample_states[i].dtype))

    y_prompt = _rmsnorm(xp[:, N_META:], norm_final)
    y_sample = _rmsnorm(xs, norm_final)
    p_wkv, p_shift, p_ret, p_hg, p_gd, p_conv = (jnp.stack(z) for z in new_p)
    s_wkv, s_shift, s_ret, s_hg, s_gd, s_conv = (jnp.stack(z) for z in new_s)
    return (y_prompt, y_sample, p_wkv, p_shift, p_ret, p_hg, p_gd, p_conv, s_wkv, s_shift, s_ret, s_hg, s_gd, s_conv)
```

```python
import functools
import math

import jax
import jax.numpy as jnp
from jax import lax
from jax.experimental import pallas as pl
from jax.experimental.pallas import tpu as pltpu

F32 = jnp.float32
BF16 = jnp.bfloat16

N_META = 16
PAST_LEN = 16384
NORM_EPS = 1e-6
RW_LN_EPS = 64e-5
ROPE_BASE = 10000.0

D_MODEL = 2048
BR_W = 512
RW_HEADS, RW_HEAD = 8, 64
RW_COLS = 1792
RET_HEADS, RET_DK, RET_DV = 4, 64, 128
RET_COLS = 1536
HG_HEADS, HG_D = 4, 128
HG_COLS = 2048
GD_HEADS, GD_D = 4, 128
GD_CONV = 4
GD_QKV = 1536
GD_COLS_PAD = 2176

LANES = 128
SUBLANES = 8
VMEM_LIMIT = 56 * 1024 * 1024


def _cparams(n_axes):
    return pltpu.CompilerParams(dimension_semantics=("arbitrary",) * n_axes, vmem_limit_bytes=VMEM_LIMIT)


def _dg(a, b, ca, cb):
    return lax.dot_general(a, b, (((ca,), (cb,)), ((), ())), preferred_element_type=F32)


def _mm(a, b):
    return _dg(a.astype(BF16), b.astype(BF16), 1, 0)


def _mm_nt(a, b):
    return _dg(a.astype(BF16), b.astype(BF16), 1, 1)


def _mm_tn(a, b):
    return _dg(a.astype(BF16), b.astype(BF16), 0, 0)


def _split(x, n):
    parts, r = [], x
    for _ in range(n):
        p = r.astype(BF16)
        parts.append(p)
        r = r - p.astype(F32)
    return parts


def _mm_hp(a, b):
    ah, al = _split(a, 2)
    bh, bl = _split(b, 2)
    return _dg(ah, bh, 1, 0) + (_dg(ah, bl, 1, 0) + _dg(al, bh, 1, 0))


def _mm_mask(m01, x):
    x0, x1, x2 = _split(x, 3)
    return _dg(m01, x0, 1, 0) + (_dg(m01, x1, 1, 0) + _dg(m01, x2, 1, 0))


def _iota(shape, dim):
    return lax.broadcasted_iota(jnp.int32, shape, dim)


def _softplus(x):
    return jnp.maximum(x, 0.0) + jnp.log(1.0 + jnp.exp(-jnp.abs(x)))


def _sigmoid(x):
    return jax.nn.sigmoid(x)


def _silu(x):
    return x * jax.nn.sigmoid(x)


def _unit_lower_inverse(low, c, mm):
    eye = (_iota((c, c), 0) == _iota((c, c), 1)).astype(F32)
    inv = eye + low
    power = low
    covered = 1
    while covered < c - 1:
        power = mm(power, power)
        inv = inv + mm(inv, power)
        covered = 2 * covered + 1
    return inv


def _rmsnorm_kernel(x_ref, g_ref, o_ref):
    x = x_ref[...]
    y = x * lax.rsqrt(jnp.mean(x * x, axis=-1, keepdims=True) + NORM_EPS)
    o_ref[...] = (y * g_ref[...]).astype(o_ref.dtype)


def _rmsnorm(x, g, tm, out_dtype):
    n, d = x.shape
    return pl.pallas_call(
        _rmsnorm_kernel,
        out_shape=jax.ShapeDtypeStruct((n, d), out_dtype),
        grid=(n // tm,),
        in_specs=[pl.BlockSpec((tm, d), lambda i: (i, 0)), pl.BlockSpec((1, d), lambda i: (0, 0))],
        out_specs=pl.BlockSpec((tm, d), lambda i: (i, 0)),
        compiler_params=_cparams(1),
        name="rmsnorm",
    )(x, g.reshape(1, d))


def _matmul_kernel(x_ref, w_ref, o_ref):
    o_ref[...] = _dg(x_ref[...], w_ref[...], 1, 0)


def _matmul(x, w, tm, tn, name):
    n, k = x.shape
    m = w.shape[1]
    return pl.pallas_call(
        _matmul_kernel,
        out_shape=jax.ShapeDtypeStruct((n, m), F32),
        grid=(n // tm, m // tn),
        in_specs=[pl.BlockSpec((tm, k), lambda i, j: (i, 0)), pl.BlockSpec((k, tn), lambda i, j: (0, j))],
        out_specs=pl.BlockSpec((tm, tn), lambda i, j: (i, j)),
        compiler_params=_cparams(2),
        name=name,
    )(x, w)


def _merge_kernel(x_ref, oa_ref, ob_ref, oc_ref, od_ref, g0_ref, g1_ref, g2_ref, g3_ref, wb_ref, wo_ref,
                  y_ref, acc_ref):
    j = pl.program_id(1)

    @pl.when(j == 0)
    def _init():
        acc_ref[...] = jnp.zeros_like(acc_ref)

    merged = None
    for n, (o_ref, g_ref) in enumerate(((oa_ref, g0_ref), (ob_ref, g1_ref), (oc_ref, g2_ref), (od_ref, g3_ref))):
        term = _sigmoid(g_ref[...]) * _dg(o_ref[...].astype(BF16), wb_ref[n], 1, 0)
        merged = term if merged is None else merged + term
    acc_ref[...] += _dg(merged.astype(BF16), wo_ref[...], 1, 0)

    @pl.when(j == pl.num_programs(1) - 1)
    def _fin():
        y_ref[...] = x_ref[...] + acc_ref[...]


def _merge(x, outs, pg, wb, wo, tm, tj):
    n, d = x.shape
    nj = d // tj
    o_spec = pl.BlockSpec((tm, BR_W), lambda i, j: (i, 0))
    g_specs = [pl.BlockSpec((tm, tj), functools.partial(lambda i, j, nb: (i, nb * nj + j), nb=nb)) for nb in range(4)]
    return pl.pallas_call(
        _merge_kernel,
        out_shape=jax.ShapeDtypeStruct((n, d), F32),
        grid=(n // tm, nj),
        in_specs=[pl.BlockSpec((tm, d), lambda i, j: (i, 0)), o_spec, o_spec, o_spec, o_spec, *g_specs,
                  pl.BlockSpec((4, BR_W, tj), lambda i, j: (0, 0, j)),
                  pl.BlockSpec((tj, d), lambda i, j: (j, 0))],
        out_specs=pl.BlockSpec((tm, d), lambda i, j: (i, 0)),
        scratch_shapes=[pltpu.VMEM((tm, d), F32)],
        compiler_params=_cparams(2),
        name="merge",
    )(x, *outs, pg, pg, pg, pg, wb, wo)


def _ffn_kernel(x_ref, g_ref, wu_ref, wg_ref, wd_ref, y_ref, h_ref, acc_ref):
    j = pl.program_id(1)

    @pl.when(j == 0)
    def _init():
        x = x_ref[...]
        y = x * lax.rsqrt(jnp.mean(x * x, axis=-1, keepdims=True) + NORM_EPS)
        h_ref[...] = (y * g_ref[...]).astype(BF16)
        acc_ref[...] = jnp.zeros_like(acc_ref)

    h = h_ref[...]
    up = _dg(h, wu_ref[...], 1, 0)
    gate = _dg(h, wg_ref[...], 1, 0)
    acc_ref[...] += _dg((_silu(gate) * up).astype(BF16), wd_ref[...], 1, 0)

    @pl.when(j == pl.num_programs(1) - 1)
    def _fin():
        y_ref[...] = x_ref[...] + acc_ref[...]


def _ffn(x, g, w_up, w_down, tm, tf):
    n, d = x.shape
    ff = w_down.shape[0]
    nf = ff // tf
    return pl.pallas_call(
        _ffn_kernel,
        out_shape=jax.ShapeDtypeStruct((n, d), F32),
        grid=(n // tm, nf),
        in_specs=[pl.BlockSpec((tm, d), lambda i, j: (i, 0)), pl.BlockSpec((1, d), lambda i, j: (0, 0)),
                  pl.BlockSpec((d, tf), lambda i, j: (0, j)),
                  pl.BlockSpec((d, tf), lambda i, j: (0, nf + j)),
                  pl.BlockSpec((tf, d), lambda i, j: (j, 0))],
        out_specs=pl.BlockSpec((tm, d), lambda i, j: (i, 0)),
        scratch_shapes=[pltpu.VMEM((tm, d), BF16), pltpu.VMEM((tm, d), F32)],
        compiler_params=_cparams(2),
        name="ffn",
    )(x, g.reshape(1, d), w_up, w_up, w_down)


def _rwkv_kernel(pa_ref, sh0_ref, s0_ref, mu_ref, w0_ref, w2_ref, a0_ref, a2_ref, g2_ref, kkw_ref, ka_ref,
                 rk_ref, lng_ref, lnb_ref, bd_ref, o_ref, s_ref, sh_ref, xs_scr, o_scr, *, c):
    ci = pl.program_id(1)

    @pl.when(ci == 0)
    def _init():
        xs_scr[SUBLANES - 1:SUBLANES, :] = sh0_ref[...]
        s_ref[...] = s0_ref[...]

    pa = pa_ref[...]
    xs_scr[SUBLANES:SUBLANES + c, :] = pa
    prev = xs_scr[SUBLANES - 1:SUBLANES - 1 + c, :]
    last = pa[c - 1:c, :]
    xs_scr[SUBLANES - 1:SUBLANES, :] = last
    sh_ref[...] = last

    xm = pa + (prev - pa) * mu_ref[...]
    r = xm[:, 0:512]
    k0 = xm[:, 512:1024]
    v = xm[:, 1024:1536]
    wa = xm[:, 1536:1664]
    gl = xm[:, 1664:1792]
    w_log = -_softplus(-(w0_ref[...] + _mm(jnp.tanh(wa), w2_ref[...]))) - 0.5
    ld = -jnp.exp(w_log)
    asig = _sigmoid(a0_ref[...] + _mm(wa, a2_ref[...]))
    g_a = _mm(_sigmoid(gl), g2_ref[...])
    bd = bd_ref[...]
    kkr = k0 * kkw_ref[...]
    kkn = kkr * lax.rsqrt(_mm(kkr * kkr, bd) + 1e-6)
    k1 = k0 * (1.0 + (asig - 1.0) * ka_ref[...])
    a_vec = -kkn
    b_vec = kkn * asig

    tri = (_iota((c, c), 0) >= _iota((c, c), 1)).astype(BF16)
    cum = _mm_mask(tri, ld)
    cum_last = cum[c - 1:c, :]
    a_t = a_vec * jnp.exp(cum - ld)
    r_t = r * jnp.exp(cum)
    inv_p = jnp.exp(-cum)
    b_t = b_vec * inv_p
    k_t = k1 * inv_p
    to_end = jnp.exp(cum_last - cum)
    b_e = b_vec * to_end
    k_e = k1 * to_end
    p_end = jnp.exp(cum_last)

    row2 = _iota((c, 2 * c), 0)
    col2 = _iota((c, 2 * c), 1)
    col2 = jnp.where(col2 >= c, col2 - c, col2)
    strict2 = col2 < row2
    incl2 = col2 <= row2
    zeros_v = jnp.zeros((c, RW_HEAD), F32)

    for h in range(RW_HEADS):
        sl = slice(h * RW_HEAD, (h + 1) * RW_HEAD)
        s_h = s_ref[h]
        lhs2 = jnp.concatenate([a_t[:, sl], r_t[:, sl]], axis=0)
        rhs2 = jnp.concatenate([b_t[:, sl], k_t[:, sl]], axis=0)
        gram = _mm_nt(lhs2, rhs2)
        from_state = _mm_nt(lhs2, s_h)
        v_h = v[:, sl]
        low2 = jnp.where(strict2, gram[0:c], 0.0)
        rhs_u = from_state[0:c] + _mm(low2, jnp.concatenate([zeros_v, v_h], axis=0))
        inv = _unit_lower_inverse(low2[:, 0:c], c, _mm_hp)
        u = _mm_hp(inv, rhs_u)
        uv = jnp.concatenate([u, v_h], axis=0)
        o_h = from_state[c:2 * c] + _mm(jnp.where(incl2, gram[c:2 * c], 0.0), uv)
        s_ref[h] = s_h * p_end[:, sl] + _mm_tn(uv, jnp.concatenate([b_e[:, sl], k_e[:, sl]], axis=0))
        o_scr[:, sl] = o_h

    o = o_scr[...]
    inv_n = 1.0 / RW_HEAD
    mean = _mm(o, bd) * inv_n
    dev = o - mean
    var = _mm(dev * dev, bd) * inv_n
    normed = dev * lax.rsqrt(var + RW_LN_EPS) * lng_ref[...] + lnb_ref[...]
    bonus = _mm(r * k1 * rk_ref[...], bd) * v
    o_ref[...] = (normed + bonus) * g_a


def _rwkv(pa, shift0, wkv0, layer, prm, *, batch, seq, c, row_block0):
    nc = seq // c
    kern = functools.partial(_rwkv_kernel, c=c)
    vec = lambda n: pl.BlockSpec((1, n), lambda b, ci: (0, 0))
    mat = lambda r_, n: pl.BlockSpec((r_, n), lambda b, ci: (0, 0))
    return pl.pallas_call(
        kern,
        out_shape=(jax.ShapeDtypeStruct((batch * seq, BR_W), F32),
                   jax.ShapeDtypeStruct((batch, RW_HEADS, RW_HEAD, RW_HEAD), F32),
                   jax.ShapeDtypeStruct((batch, 1, RW_COLS), F32)),
        grid=(batch, nc),
        in_specs=[pl.BlockSpec((c, RW_COLS), lambda b, ci: (row_block0 + b * nc + ci, 0)),
                  pl.BlockSpec((None, None, 1, RW_COLS), lambda b, ci: (layer, b, 0, 0)),
                  pl.BlockSpec((None, None, RW_HEADS, RW_HEAD, RW_HEAD), lambda b, ci: (layer, b, 0, 0, 0)),
                  vec(RW_COLS), vec(512), mat(128, 512), vec(512), mat(128, 512), mat(128, 512),
                  vec(512), vec(512), vec(512), vec(512), vec(512), mat(512, 512)],
        out_specs=(pl.BlockSpec((c, BR_W), lambda b, ci: (b * nc + ci, 0)),
                   pl.BlockSpec((None, RW_HEADS, RW_HEAD, RW_HEAD), lambda b, ci: (b, 0, 0, 0)),
                   pl.BlockSpec((None, 1, RW_COLS), lambda b, ci: (b, 0, 0))),
        scratch_shapes=[pltpu.VMEM((SUBLANES + c, RW_COLS), F32), pltpu.VMEM((c, BR_W), F32)],
        compiler_params=_cparams(2),
        name="rwkv7",
    )(pa, shift0, wkv0, *prm)


def _ret_kernel(pb_ref, cos_ref, sin_ref, s0_ref, o_ref, s_ref, *, c):
    ci = pl.program_id(1)

    @pl.when(ci == 0)
    def _init():
        s_ref[...] = s0_ref[...]

    cos = cos_ref[...]
    sin = sin_ref[...]
    width = RET_HEADS * RET_DK
    half = RET_DK // 2
    first_half = (_iota((c, width), 1) & half) == 0

    def rot(x):
        partner = jnp.where(first_half, pltpu.roll(x, width - half, 1), pltpu.roll(x, half, 1))
        return x * cos + partner * sin

    q = rot(pb_ref[:, 0:width])
    k = rot(pb_ref[:, width:2 * width]) * (RET_DK ** -0.5)
    v = pb_ref[:, 2 * width:2 * width + BR_W]
    g_b = pb_ref[:, 2 * width + BR_W:2 * width + 2 * BR_W]

    row = _iota((c, c), 0)
    col = _iota((c, c), 1)
    causal = row >= col
    dist = jnp.where(causal, (row - col).astype(F32), 0.0)
    t1 = (_iota((c, 1), 0) + 1).astype(F32)
    for h in range(RET_HEADS):
        log_gamma = math.log1p(-(2.0 ** (-5.0 - h)))
        dmat = jnp.where(causal, jnp.exp(dist * log_gamma), 0.0)
        q_h = q[:, h * RET_DK:(h + 1) * RET_DK]
        k_h = k[:, h * RET_DK:(h + 1) * RET_DK]
        v_h = v[:, h * RET_DV:(h + 1) * RET_DV]
        s_h = s_ref[h]
        scores = _mm_nt(q_h, k_h) * dmat
        o_h = _mm(scores, v_h) + jnp.exp(t1 * log_gamma) * _mm(q_h, s_h)
        k_dec = k_h * jnp.exp((c - t1) * log_gamma)
        s_ref[h] = math.exp(c * log_gamma) * s_h + _mm_tn(k_dec, v_h)
        o_n = o_h * lax.rsqrt(jnp.mean(o_h * o_h, axis=-1, keepdims=True) + NORM_EPS)
        o_ref[:, h * RET_DV:(h + 1) * RET_DV] = o_n * _silu(g_b[:, h * RET_DV:(h + 1) * RET_DV])


def _ret(pb, cos, sin, ret0, layer, *, batch, seq, c, row_block0):
    nc = seq // c
    width = RET_HEADS * RET_DK
    return pl.pallas_call(
        functools.partial(_ret_kernel, c=c),
        out_shape=(jax.ShapeDtypeStruct((batch * seq, BR_W), F32),
                   jax.ShapeDtypeStruct((batch, RET_HEADS, RET_DK, RET_DV), F32)),
        grid=(batch, nc),
        in_specs=[pl.BlockSpec((c, RET_COLS), lambda b, ci: (row_block0 + b * nc + ci, 0)),
                  pl.BlockSpec((c, width), lambda b, ci: (ci, 0)),
                  pl.BlockSpec((c, width), lambda b, ci: (ci, 0)),
                  pl.BlockSpec((None, None, RET_HEADS, RET_DK, RET_DV), lambda b, ci: (layer, b, 0, 0, 0))],
        out_specs=(pl.BlockSpec((c, BR_W), lambda b, ci: (b * nc + ci, 0)),
                   pl.BlockSpec((None, RET_HEADS, RET_DK, RET_DV), lambda b, ci: (b, 0, 0, 0))),
        compiler_params=_cparams(2),
        name="retention",
    )(pb, cos, sin, ret0)


def _hgrn_kernel(pc_ref, lbp_ref, ng_ref, s0_ref, o_ref, s_ref, st_scr, *, c, m, layer, n_chunks):
    ci = pl.program_id(1)

    @pl.when(ci == 0)
    def _init():
        for h in range(HG_HEADS):
            st_scr[h] = s0_ref[h].T

    lbp = lbp_ref[...]
    e = jnp.exp(lbp - jnp.max(lbp, axis=0, keepdims=True))
    soft = e / jnp.sum(e, axis=0, keepdims=True)
    lb = jnp.zeros((1, BR_W), F32)
    for j in range(1, layer + 1):
        lb = lb + soft[j:j + 1, :]

    q = _silu(pc_ref[:, 0:BR_W])
    f_in = pc_ref[:, BR_W:2 * BR_W]
    v = pc_ref[:, 2 * BR_W:3 * BR_W]
    g_c = pc_ref[:, 3 * BR_W:4 * BR_W]
    log_f = jnp.log(lb + (1.0 - lb) * _sigmoid(f_in))
    k = (1.0 - lb) * _sigmoid(-f_in)

    tri = (_iota((c, c), 0) >= _iota((c, c), 1)).astype(BF16)
    cum = _mm_mask(tri, log_f)
    cum_last = cum[c - 1:c, :]
    q_e = q * jnp.exp(cum)
    k_e = k * jnp.exp(cum_last - cum)
    p_end = jnp.exp(cum_last)

    col_m = _iota((m, m), 1)
    row_l = _iota((m, LANES), 0)
    for h in range(HG_HEADS):
        sl = slice(h * HG_D, (h + 1) * HG_D)
        st = st_scr[h]
        o_rows = []
        for i in range(c // m):
            r0 = i * m
            q_i = q[r0:r0 + m, sl]
            c_i = cum[r0:r0 + m, sl]
            k_i = k[r0:r0 + m, sl]
            diag = jnp.zeros((m, m), F32)
            for s in range(m):
                keep = row_l >= s
                decay = jnp.exp(jnp.where(keep, c_i - c_i[s:s + 1, :], 0.0))
                col = jnp.sum(jnp.where(keep, q_i * k_i[s:s + 1, :] * decay, 0.0), axis=-1, keepdims=True)
                diag = jnp.where(col_m == s, col, diag)
            o_i = _mm(diag, v[r0:r0 + m, sl])
            if i > 0:
                c_ref = cum[r0 - 1:r0, sl]
                q_s = q_i * jnp.exp(c_i - c_ref)
                k_s = k[0:r0, sl] * jnp.exp(c_ref - cum[0:r0, sl])
                o_i = o_i + _mm(_mm_nt(q_s, k_s), v[0:r0, sl])
            o_rows.append(o_i)
        o_h = _mm_nt(q_e[:, sl], st) + jnp.concatenate(o_rows, axis=0)
        st_scr[h] = st * p_end[:, sl] + _mm_tn(v[:, sl], k_e[:, sl])
        o_n = o_h * lax.rsqrt(jnp.mean(o_h * o_h, axis=-1, keepdims=True) + NORM_EPS) * ng_ref[:, sl]
        o_ref[:, sl] = o_n * _silu(g_c[:, sl])

    @pl.when(ci == n_chunks - 1)
    def _fin():
        for h in range(HG_HEADS):
            s_ref[h] = st_scr[h].T


def _hgrn(pc, lbp, ng, hg0, state_layer, layer, *, batch, seq, c, row_block0):
    nc = seq // c
    m = 16 if c % 16 == 0 else SUBLANES
    depth = lbp.shape[0]
    return pl.pallas_call(
        functools.partial(_hgrn_kernel, c=c, m=m, layer=layer, n_chunks=nc),
        out_shape=(jax.ShapeDtypeStruct((batch * seq, BR_W), F32),
                   jax.ShapeDtypeStruct((batch, HG_HEADS, HG_D, HG_D), F32)),
        grid=(batch, nc),
        in_specs=[pl.BlockSpec((c, HG_COLS), lambda b, ci: (row_block0 + b * nc + ci, 0)),
                  pl.BlockSpec((depth, BR_W), lambda b, ci: (0, 0)),
                  pl.BlockSpec((1, BR_W), lambda b, ci: (0, 0)),
                  pl.BlockSpec((None, None, HG_HEADS, HG_D, HG_D), lambda b, ci: (state_layer, b, 0, 0, 0))],
        out_specs=(pl.BlockSpec((c, BR_W), lambda b, ci: (b * nc + ci, 0)),
                   pl.BlockSpec((None, HG_HEADS, HG_D, HG_D), lambda b, ci: (b, 0, 0, 0))),
        scratch_shapes=[pltpu.VMEM((HG_HEADS, HG_D, HG_D), F32)],
        compiler_params=_cparams(2),
        name="hgrn2",
    )(pc, lbp, ng, hg0)


def _gdn_kernel(pd_ref, cs0_ref, s0_ref, cw_ref, alog_ref, dtb_ref, ng_ref, o_ref, s_ref, cs_ref, ext_scr, *, c):
    ci = pl.program_id(1)
    keep = GD_CONV - 1

    @pl.when(ci == 0)
    def _init():
        ext_scr[SUBLANES - keep:SUBLANES, :] = cs0_ref[...]
        s_ref[...] = s0_ref[...]

    x = pd_ref[:, 0:GD_QKV]
    ext_scr[SUBLANES:SUBLANES + c, :] = x
    conv = ext_scr[SUBLANES - keep:SUBLANES - keep + c, :] * cw_ref[0:1, :]
    for j in range(1, keep):
        conv = conv + ext_scr[SUBLANES - keep + j:SUBLANES - keep + j + c, :] * cw_ref[j:j + 1, :]
    conv = conv + x * cw_ref[keep:keep + 1, :]
    tail = ext_scr[SUBLANES + c - keep:SUBLANES + c, :]
    ext_scr[SUBLANES - keep:SUBLANES, :] = tail
    cs_ref[...] = tail

    act = _silu(conv)
    q = act[:, 0:BR_W]
    k = act[:, BR_W:2 * BR_W]
    v = act[:, 2 * BR_W:3 * BR_W]
    g_d = pd_ref[:, GD_QKV:GD_QKV + BR_W]
    ba = pd_ref[:, GD_QKV + BR_W:GD_QKV + BR_W + LANES]
    beta_all = _sigmoid(ba)
    g_all = -jnp.exp(alog_ref[...]) * _softplus(ba + dtb_ref[...])

    row = _iota((c, c), 0)
    col = _iota((c, c), 1)
    causal = row >= col
    strict = row > col
    eye = row == col
    tri = causal.astype(BF16)
    ones = jnp.ones((c, c), BF16)
    cum_all = _mm_mask(tri, g_all)

    for h in range(GD_HEADS):
        sl = slice(h * GD_D, (h + 1) * GD_D)
        q_h = q[:, sl]
        q_h = q_h * lax.rsqrt(jnp.sum(q_h * q_h, axis=-1, keepdims=True) + 1e-6) * (GD_D ** -0.5)
        k_h = k[:, sl]
        k_h = k_h * lax.rsqrt(jnp.sum(k_h * k_h, axis=-1, keepdims=True) + 1e-6)
        v_h = v[:, sl]
        beta = beta_all[:, h:h + 1]
        cum = cum_all[:, GD_HEADS + h:GD_HEADS + h + 1]
        cum_t = jnp.broadcast_to(cum, (c, c))
        cum_s = _mm_mask(ones, jnp.where(eye, cum_t, 0.0))
        dmat = jnp.where(causal, jnp.exp(jnp.where(causal, cum_t - cum_s, 0.0)), 0.0)
        a_mat = beta * _mm_nt(k_h, k_h) * jnp.where(strict, dmat, 0.0)
        inv = _unit_lower_inverse(-a_mat, c, _mm_hp)
        e_cum = jnp.exp(cum)
        sol = _mm_hp(inv, jnp.concatenate([(beta * e_cum) * k_h, beta * v_h], axis=1))
        s_h = s_ref[h]
        delta = sol[:, GD_D:2 * GD_D] - _mm(sol[:, 0:GD_D], s_h)
        o_h = e_cum * _mm(q_h, s_h) + _mm(_mm_nt(q_h, k_h) * dmat, delta)
        cum_last = cum[c - 1:c, :]
        s_ref[h] = jnp.exp(cum_last) * s_h + _mm_tn(k_h * jnp.exp(cum_last - cum), delta)
        o_n = o_h * lax.rsqrt(jnp.mean(o_h * o_h, axis=-1, keepdims=True) + NORM_EPS) * ng_ref[:, sl]
        o_ref[:, sl] = o_n * _silu(g_d[:, sl])


def _gdn(pd, conv0, gd0, layer, cw, alog, dtb, ng, *, batch, seq, c, row_block0):
    nc = seq // c
    keep = GD_CONV - 1
    return pl.pallas_call(
        functools.partial(_gdn_kernel, c=c),
        out_shape=(jax.ShapeDtypeStruct((batch * seq, BR_W), F32),
                   jax.ShapeDtypeStruct((batch, GD_HEADS, GD_D, GD_D), F32),
                   jax.ShapeDtypeStruct((batch, keep, GD_QKV), F32)),
        grid=(batch, nc),
        in_specs=[pl.BlockSpec((c, GD_COLS_PAD), lambda b, ci: (row_block0 + b * nc + ci, 0)),
                  pl.BlockSpec((None, None, keep, GD_QKV), lambda b, ci: (layer, b, 0, 0)),
                  pl.BlockSpec((None, None, GD_HEADS, GD_D, GD_D), lambda b, ci: (layer, b, 0, 0, 0)),
                  pl.BlockSpec((GD_CONV, GD_QKV), lambda b, ci: (0, 0)),
                  pl.BlockSpec((1, LANES), lambda b, ci: (0, 0)),
                  pl.BlockSpec((1, LANES), lambda b, ci: (0, 0)),
                  pl.BlockSpec((1, BR_W), lambda b, ci: (0, 0))],
        out_specs=(pl.BlockSpec((c, BR_W), lambda b, ci: (b * nc + ci, 0)),
                   pl.BlockSpec((None, GD_HEADS, GD_D, GD_D), lambda b, ci: (b, 0, 0, 0)),
                   pl.BlockSpec((None, keep, GD_QKV), lambda b, ci: (b, 0, 0))),
        scratch_shapes=[pltpu.VMEM((SUBLANES + c, GD_QKV), F32)],
        compiler_params=_cparams(2),
        name="gdn",
    )(pd, conv0, gd0, cw, alog, dtb, ng)


def _pick_chunk(seq):
    for c in (64, 56, 48, 40, 32, 24, 16, 8):
        if seq % c == 0:
            return c
    raise ValueError(f"sequence length {seq} is not a multiple of {SUBLANES}")


def _pick_tile(n, candidates):
    for t in candidates:
        if n % t == 0:
            return t
    raise ValueError(f"no tile in {candidates} divides {n}")


def _rope_tables(pos):
    half = RET_DK // 2
    inv = ROPE_BASE ** (-jnp.arange(half, dtype=F32) / half)
    ang = pos.astype(F32)[:, None] * inv[None, :]
    cos, sin = jnp.cos(ang), jnp.sin(ang)
    cos_t = jnp.tile(jnp.concatenate([cos, cos], axis=1), (1, RET_HEADS))
    sin_t = jnp.tile(jnp.concatenate([-sin, sin], axis=1), (1, RET_HEADS))
    return cos_t, sin_t


def kernel(x_prompt, x_sample, state_rwkv_wkv, state_rwkv_shift, state_ret, state_hgrn, state_gdn, state_gdn_conv, meta_tokens, norm_mix, w_in, rw_mu, rw_w0, rw_w2, rw_a0, rw_a2, rw_g2, rw_kk, rw_ka, rw_rk, rw_ln_g, rw_ln_b, hg_lb, hg_norm_g, gd_conv, gd_a_log, gd_dt_bias, gd_norm_g, w_branch, w_out, norm_ffn, w_up, w_down, norm_final):
    depth = norm_mix.shape[0]
    bp, tq, d = x_prompt.shape
    tp = tq + N_META
    bs, ts, _ = x_sample.shape
    n_p, n_s = bp * tp, bs * ts
    n_tok = n_p + n_s
    cp, cs = _pick_chunk(tp), _pick_chunk(ts)
    assert cs == ts and n_p % ts == 0
    tm = _pick_tile(n_tok, (464, 512, 256, 232, 128, 64, 32, 16, 8))

    meta = jnp.broadcast_to(meta_tokens.astype(F32)[None], (bp, N_META, d))
    x = jnp.concatenate([jnp.concatenate([meta, x_prompt], axis=1).reshape(n_p, d), x_sample.reshape(n_s, d)], axis=0)

    cos_p, sin_p = _rope_tables(jnp.arange(tp, dtype=jnp.int32))
    cos_s, sin_s = _rope_tables(PAST_LEN + jnp.arange(ts, dtype=jnp.int32))

    head_of = jnp.arange(BR_W) // RW_HEAD
    bd = (head_of[:, None] == head_of[None, :]).astype(BF16)

    zeros = lambda *s: jnp.zeros((1, bp) + s, F32)
    z_wkv, z_shift = zeros(RW_HEADS, RW_HEAD, RW_HEAD), zeros(1, RW_COLS)
    z_ret, z_hg, z_gd = zeros(RET_HEADS, RET_DK, RET_DV), zeros(HG_HEADS, HG_D, HG_D), zeros(GD_HEADS, GD_D, GD_D)
    z_conv = zeros(GD_CONV - 1, GD_QKV)
    shift_s = state_rwkv_shift.reshape(depth, bs, 1, RW_COLS)

    o_a, o_b, o_c = RW_COLS, RW_COLS + RET_COLS, RW_COLS + RET_COLS + HG_COLS
    o_g = o_c + GD_QKV + 2 * GD_HEADS + BR_W
    new_p = [[] for _ in range(6)]
    new_s = [[] for _ in range(6)]
    for l in range(depth):
        wl = w_in[l]
        w_a = wl[:, 0:o_a].astype(BF16)
        w_b = wl[:, o_a:o_b].astype(BF16)
        w_c = wl[:, o_b:o_c].astype(BF16)
        w_d = jnp.concatenate([wl[:, o_c:o_c + GD_QKV], wl[:, o_c + GD_QKV + 2 * GD_HEADS:o_g],
                               wl[:, o_c + GD_QKV:o_c + GD_QKV + 2 * GD_HEADS],
                               jnp.zeros((d, LANES - 2 * GD_HEADS), F32)], axis=1).astype(BF16)
        w_g = wl[:, o_g:].astype(BF16)

        h = _rmsnorm(x, norm_mix[l], tm, BF16)
        pa = _matmul(h, w_a, tm, 896, "in_proj_a")
        pb = _matmul(h, w_b, tm, 768, "in_proj_b")
        pc = _matmul(h, w_c, tm, 1024, "in_proj_c")
        pd = _matmul(h, w_d, tm, GD_COLS_PAD, "in_proj_d")
        pg = _matmul(h, w_g, tm, 1024, "in_proj_g")

        row1 = lambda a: a.reshape(1, -1)
        pad_rows = lambda a, top: jnp.concatenate(
            [jnp.zeros((top, BR_W), F32), a, jnp.zeros((LANES - top - a.shape[0], BR_W), F32)], axis=0).astype(BF16)
        rw_prm = (row1(rw_mu[l]), row1(rw_w0[l]), pad_rows(rw_w2[l], 0), row1(rw_a0[l]), pad_rows(rw_a2[l], 64),
                  rw_g2[l].astype(BF16), row1(rw_kk[l]), row1(rw_ka[l]), row1(rw_rk[l]), row1(rw_ln_g[l]),
                  row1(rw_ln_b[l]), bd)
        lane_pad = lambda a: jnp.concatenate(
            [jnp.zeros((GD_HEADS,), F32), a, jnp.zeros((LANES - 2 * GD_HEADS,), F32)]).reshape(1, LANES)
        gd_prm = (gd_conv[l], lane_pad(gd_a_log[l]), lane_pad(gd_dt_bias[l]), row1(gd_norm_g[l]))
        hg_g = row1(hg_norm_g[l])

        outs = []
        for grp, (batch, seq, c, rb0, sts, lay, cos, sin) in enumerate((
                (bp, tp, cp, 0, (z_wkv, z_shift, z_ret, z_hg, z_gd, z_conv), 0, cos_p, sin_p),
                (bs, ts, cs, n_p // ts, (state_rwkv_wkv, shift_s, state_ret, state_hgrn, state_gdn, state_gdn_conv), l,
                 cos_s, sin_s))):
            kw = dict(batch=batch, seq=seq, c=c, row_block0=rb0)
            oa, wkv, shift = _rwkv(pa, sts[1], sts[0], lay, rw_prm, **kw)
            ob, ret = _ret(pb, cos, sin, sts[2], lay, **kw)
            oc, hg = _hgrn(pc, hg_lb, hg_g, sts[3], lay, l, **kw)
            od, gd, conv = _gdn(pd, sts[5], sts[4], lay, *gd_prm, **kw)
            outs.append((oa, ob, oc, od))
            dst = new_p if grp == 0 else new_s
            for i, st in enumerate((wkv, shift.reshape(batch, RW_COLS), ret, hg, gd, conv)):
                dst[i].append(st)
        branch = [jnp.concatenate([outs[0][n], outs[1][n]], axis=0) for n in range(4)]
        wb = w_branch[l].astype(BF16)
        x = _merge(x, branch, pg, wb, w_out[l].astype(BF16), tm, 512)
        x = _ffn(x, norm_ffn[l], w_up[l].astype(BF16), w_down[l].astype(BF16), tm, 512)

    y = _rmsnorm(x, norm_final, tm, F32)
    y_prompt = y[:n_p].reshape(bp, tp, d)[:, N_META:]
    y_sample = y[n_p:].reshape(bs, ts, d)
    p_states = tuple(jnp.stack(z) for z in new_p)
    s_states = tuple(jnp.stack(z) for z in new_s)
    return (y_prompt, y_sample) + p_states + s_states
```

```python
import functools
import math

import jax
import jax.numpy as jnp
import numpy as np
from jax import lax
from jax.experimental import pallas as pl
from jax.experimental.pallas import tpu as pltpu

F32 = jnp.float32
BF16 = jnp.bfloat16

N_META = 16
PAST_LEN = 16384
NORM_EPS = 1e-6
RW_LN_EPS = 64e-5
ROPE_BASE = 10000.0

D_MODEL = 2048
BR_W = 512
RW_HEADS, RW_HEAD = 8, 64
RW_COLS = 1792
RW_GROUP = 4
RW_GW = RW_GROUP * RW_HEAD
RET_HEADS, RET_DK, RET_DV = 4, 64, 128
RET_COLS = 1536
HG_HEADS, HG_D = 4, 128
HG_COLS = 2048
GD_HEADS, GD_D = 4, 128
GD_CONV = 4
GD_QKV = 1536
GD_COLS_PAD = 2176

LANES = 128
SUBLANES = 8
VMEM_LIMIT = 56 * 1024 * 1024


def _cparams(n_axes):
    return pltpu.CompilerParams(dimension_semantics=("arbitrary",) * n_axes, vmem_limit_bytes=VMEM_LIMIT)


def _dg(a, b, ca, cb):
    return lax.dot_general(a, b, (((ca,), (cb,)), ((), ())), preferred_element_type=F32)


def _mm(a, b):
    return _dg(a.astype(BF16), b.astype(BF16), 1, 0)


def _mm_nt(a, b):
    return _dg(a.astype(BF16), b.astype(BF16), 1, 1)


def _mm_tn(a, b):
    return _dg(a.astype(BF16), b.astype(BF16), 0, 0)


def _split(x, n):
    parts, r = [], x
    for _ in range(n):
        p = r.astype(BF16)
        parts.append(p)
        r = r - p.astype(F32)
    return parts


def _mm_mask(m01, x):
    x0, x1, x2 = _split(x, 3)
    return _dg(m01, x0, 1, 0) + (_dg(m01, x1, 1, 0) + _dg(m01, x2, 1, 0))


def _iota(shape, dim):
    return lax.broadcasted_iota(jnp.int32, shape, dim)


def _softplus(x):
    return jnp.maximum(x, 0.0) + jnp.log(1.0 + jnp.exp(-jnp.abs(x)))


def _sigmoid(x):
    return jax.nn.sigmoid(x)


def _silu(x):
    return x * jax.nn.sigmoid(x)


def _unit_lower_inverse(lows, c):
    shape = lows[0].shape
    eye = (_iota(shape, 0) == _iota(shape, 1)).astype(F32)
    invs = [eye + low for low in lows]
    powers = list(lows)
    covered = 1
    while covered < c - 1:
        powers = [_mm(p, p) for p in powers]
        invs = [inv + _mm(inv, p) for inv, p in zip(invs, powers)]
        covered = 2 * covered + 1
    return invs


def _rmsnorm_kernel(x_ref, g_ref, o_ref):
    x = x_ref[...]
    y = x * lax.rsqrt(jnp.mean(x * x, axis=-1, keepdims=True) + NORM_EPS)
    o_ref[...] = (y * g_ref[...]).astype(o_ref.dtype)


def _rmsnorm(x, g, tm, out_dtype):
    n, d = x.shape
    return pl.pallas_call(
        _rmsnorm_kernel,
        out_shape=jax.ShapeDtypeStruct((n, d), out_dtype),
        grid=(n // tm,),
        in_specs=[pl.BlockSpec((tm, d), lambda i: (i, 0)), pl.BlockSpec((1, d), lambda i: (0, 0))],
        out_specs=pl.BlockSpec((tm, d), lambda i: (i, 0)),
        compiler_params=_cparams(1),
        name="rmsnorm",
    )(x, g.reshape(1, d))


def _matmul_kernel(x_ref, w_ref, o_ref):
    o_ref[...] = _dg(x_ref[...], w_ref[...], 1, 0)


def _matmul(x, w, tm, tn, name):
    n, k = x.shape
    m = w.shape[1]
    return pl.pallas_call(
        _matmul_kernel,
        out_shape=jax.ShapeDtypeStruct((n, m), F32),
        grid=(n // tm, m // tn),
        in_specs=[pl.BlockSpec((tm, k), lambda i, j: (i, 0)), pl.BlockSpec((k, tn), lambda i, j: (0, j))],
        out_specs=pl.BlockSpec((tm, tn), lambda i, j: (i, j)),
        compiler_params=_cparams(2),
        name=name,
    )(x, w)


def _merge_kernel(x_ref, oa_ref, ob_ref, oc_ref, od_ref, g0_ref, g1_ref, g2_ref, g3_ref, wb_ref, wo_ref,
                  y_ref, acc_ref):
    j = pl.program_id(1)

    @pl.when(j == 0)
    def _init():
        acc_ref[...] = jnp.zeros_like(acc_ref)

    merged = None
    for n, (o_ref, g_ref) in enumerate(((oa_ref, g0_ref), (ob_ref, g1_ref), (oc_ref, g2_ref), (od_ref, g3_ref))):
        term = _sigmoid(g_ref[...]) * _dg(o_ref[...].astype(BF16), wb_ref[n], 1, 0)
        merged = term if merged is None else merged + term
    acc_ref[...] += _dg(merged.astype(BF16), wo_ref[...], 1, 0)

    @pl.when(j == pl.num_programs(1) - 1)
    def _fin():
        y_ref[...] = x_ref[...] + acc_ref[...]


def _merge(x, outs, pg, wb, wo, tm, tj):
    n, d = x.shape
    nj = d // tj
    o_spec = pl.BlockSpec((tm, BR_W), lambda i, j: (i, 0))
    g_specs = [pl.BlockSpec((tm, tj), functools.partial(lambda i, j, nb: (i, nb * nj + j), nb=nb)) for nb in range(4)]
    return pl.pallas_call(
        _merge_kernel,
        out_shape=jax.ShapeDtypeStruct((n, d), F32),
        grid=(n // tm, nj),
        in_specs=[pl.BlockSpec((tm, d), lambda i, j: (i, 0)), o_spec, o_spec, o_spec, o_spec, *g_specs,
                  pl.BlockSpec((4, BR_W, tj), lambda i, j: (0, 0, j)),
                  pl.BlockSpec((tj, d), lambda i, j: (j, 0))],
        out_specs=pl.BlockSpec((tm, d), lambda i, j: (i, 0)),
        scratch_shapes=[pltpu.VMEM((tm, d), F32)],
        compiler_params=_cparams(2),
        name="merge",
    )(x, *outs, pg, pg, pg, pg, wb, wo)


def _ffn_kernel(x_ref, g_ref, wu_ref, wg_ref, wd_ref, y_ref, h_ref, acc_ref):
    j = pl.program_id(1)

    @pl.when(j == 0)
    def _init():
        x = x_ref[...]
        y = x * lax.rsqrt(jnp.mean(x * x, axis=-1, keepdims=True) + NORM_EPS)
        h_ref[...] = (y * g_ref[...]).astype(BF16)
        acc_ref[...] = jnp.zeros_like(acc_ref)

    h = h_ref[...]
    up = _dg(h, wu_ref[...], 1, 0)
    gate = _dg(h, wg_ref[...], 1, 0)
    acc_ref[...] += _dg((_silu(gate) * up).astype(BF16), wd_ref[...], 1, 0)

    @pl.when(j == pl.num_programs(1) - 1)
    def _fin():
        y_ref[...] = x_ref[...] + acc_ref[...]


def _ffn(x, g, w_up, w_down, tm, tf):
    n, d = x.shape
    ff = w_down.shape[0]
    nf = ff // tf
    return pl.pallas_call(
        _ffn_kernel,
        out_shape=jax.ShapeDtypeStruct((n, d), F32),
        grid=(n // tm, nf),
        in_specs=[pl.BlockSpec((tm, d), lambda i, j: (i, 0)), pl.BlockSpec((1, d), lambda i, j: (0, 0)),
                  pl.BlockSpec((d, tf), lambda i, j: (0, j)),
                  pl.BlockSpec((d, tf), lambda i, j: (0, nf + j)),
                  pl.BlockSpec((tf, d), lambda i, j: (j, 0))],
        out_specs=pl.BlockSpec((tm, d), lambda i, j: (i, 0)),
        scratch_shapes=[pltpu.VMEM((tm, d), BF16), pltpu.VMEM((tm, d), F32)],
        compiler_params=_cparams(2),
        name="ffn",
    )(x, g.reshape(1, d), w_up, w_up, w_down)


def _rwkv_kernel(pa_ref, sh0_ref, s0_ref, mu_ref, w0_ref, w2_ref, a0_ref, a2_ref, g2_ref, kkw_ref, ka_ref,
                 rk_ref, lng_ref, lnb_ref, bd_ref, hm_ref, ms_ref, mi_ref, tri_ref, o_ref, s_ref, sh_ref,
                 xs_scr, sbd_scr, *, c, nb, n_chunks, rw_pad):
    ci = pl.program_id(1)
    seqs = range(nb)
    head_block = lambda h: (h // RW_GROUP, slice((h % RW_GROUP) * RW_HEAD, (h % RW_GROUP + 1) * RW_HEAD))

    @pl.when(ci == 0)
    def _init():
        sbd_scr[...] = jnp.zeros_like(sbd_scr)
        for j in seqs:
            xs_scr[j, SUBLANES - 1:SUBLANES, :] = sh0_ref[j]
            for h in range(RW_HEADS):
                g, sl = head_block(h)
                sbd_scr[j, g, sl, sl] = s0_ref[j, h]

    pa = pa_ref[...]
    prevs = []
    for j in seqs:
        pa_j = pa[j * c:(j + 1) * c, :]
        xs_scr[j, SUBLANES:SUBLANES + c, :] = pa_j
        prevs.append(xs_scr[j, SUBLANES - 1:SUBLANES - 1 + c, :])
        last = pa_j[c - 1:c, :]
        xs_scr[j, SUBLANES - 1:SUBLANES, :] = last
        sh_ref[j] = last
    prev = jnp.concatenate(prevs, axis=0)

    xm = pa + (prev - pa) * mu_ref[...]
    r = xm[:, 0:512]
    k0 = xm[:, 512:1024]
    v = xm[:, 1024:1536]
    wa = xm[:, 1536:1664]
    gl = xm[:, 1664:1792]
    w_log = -_softplus(-(w0_ref[...] + _mm(jnp.tanh(wa), w2_ref[...]))) - 0.5
    ld = -jnp.exp(w_log)
    asig = _sigmoid(a0_ref[...] + _mm(wa, a2_ref[...]))
    g_a = _mm(_sigmoid(gl), g2_ref[...])
    bd = bd_ref[...]
    kkr = k0 * kkw_ref[...]
    kkn = kkr * lax.rsqrt(_mm(kkr * kkr, bd) + 1e-6)
    k1 = k0 * (1.0 + (asig - 1.0) * ka_ref[...])
    a_vec = -kkn
    b_vec = kkn * asig

    cum = _mm_mask(tri_ref[...], ld)
    cum_ends = [cum[(j + 1) * c - 1:(j + 1) * c, :] for j in seqs]
    cum_last = jnp.concatenate([jnp.broadcast_to(e, (c, BR_W)) for e in cum_ends], axis=0)
    a_t = a_vec * jnp.exp(cum - ld)
    r_t = r * jnp.exp(cum)
    inv_p = jnp.exp(-cum)
    b_t = b_vec * inv_p
    k_t = k1 * inv_p
    to_end = jnp.exp(cum_last - cum)
    b_e = b_vec * to_end
    k_e = k1 * to_end

    seq_rows = RW_GROUP * c
    rows = nb * seq_rows
    pad = [jnp.zeros((rw_pad - rows, RW_GW), F32)] if rw_pad > rows else []
    m_strict = ms_ref[...]
    m_incl = mi_ref[...]

    def stack(x):
        return jnp.concatenate([x[j * c:(j + 1) * c] * hm_ref[hh:hh + 1, :] for j in seqs for hh in range(RW_GROUP)],
                               axis=0)

    groups = range(RW_HEADS // RW_GROUP)
    lanes = [slice(g * RW_GW, (g + 1) * RW_GW) for g in groups]
    of_seq = lambda x, j: x[j * seq_rows:(j + 1) * seq_rows]
    a4 = [stack(a_t[:, gl_]) for gl_ in lanes]
    r4 = [stack(r_t[:, gl_]) for gl_ in lanes]
    v4 = [stack(v[:, gl_]) for gl_ in lanes]
    rhs2 = [jnp.concatenate([stack(b_t[:, gl_]), *pad, stack(k_t[:, gl_]), *pad], axis=0) for gl_ in lanes]
    gram = [_mm_nt(jnp.concatenate([a4[g], r4[g]], axis=0), rhs2[g]) for g in groups]
    from_state = [[_mm_nt(jnp.concatenate([of_seq(a4[g], j), of_seq(r4[g], j)], axis=0), sbd_scr[j, g]) for j in seqs]
                  for g in groups]
    fs_a = [jnp.concatenate([from_state[g][j][0:seq_rows] for j in seqs], axis=0) for g in groups]
    fs_r = [jnp.concatenate([from_state[g][j][seq_rows:2 * seq_rows] for j in seqs], axis=0) for g in groups]
    low2 = [gram[g][0:rows] * m_strict for g in groups]
    rhs_u = [fs_a[g] + _mm(low2[g][:, rw_pad:2 * rw_pad], jnp.concatenate([v4[g], *pad], axis=0)) for g in groups]
    inv = _unit_lower_inverse([low2[g][:, 0:rows] for g in groups], c)
    u = [_mm(inv[g], rhs_u[g]) for g in groups]
    o_groups = []
    for g in groups:
        uv = jnp.concatenate([u[g], *pad, v4[g], *pad], axis=0)
        o4 = fs_r[g] + _mm(gram[g][rows:2 * rows] * m_incl, uv)
        o_seq = []
        for j in seqs:
            o_j = o4[j * seq_rows:j * seq_rows + c]
            for hh in range(1, RW_GROUP):
                o_j = o_j + o4[j * seq_rows + hh * c:j * seq_rows + (hh + 1) * c]
            o_seq.append(o_j)
        o_groups.append(jnp.concatenate(o_seq, axis=0))
        b4 = stack(b_e[:, lanes[g]])
        k4 = stack(k_e[:, lanes[g]])
        for j in seqs:
            ends = jnp.concatenate([of_seq(b4, j), of_seq(k4, j)], axis=0)
            uv_j = jnp.concatenate([of_seq(u[g], j), of_seq(v4[g], j)], axis=0)
            sbd_scr[j, g] = sbd_scr[j, g] * jnp.exp(cum_ends[j][:, lanes[g]]) + _mm_tn(uv_j, ends)

    @pl.when(ci == n_chunks - 1)
    def _fin():
        for j in seqs:
            for h in range(RW_HEADS):
                g, sl = head_block(h)
                s_ref[j, h] = sbd_scr[j, g, sl, sl]

    o = jnp.concatenate(o_groups, axis=1)
    inv_n = 1.0 / RW_HEAD
    mean = _mm(o, bd) * inv_n
    dev = o - mean
    var = _mm(dev * dev, bd) * inv_n
    normed = dev * lax.rsqrt(var + RW_LN_EPS) * lng_ref[...] + lnb_ref[...]
    bonus = _mm(r * k1 * rk_ref[...], bd) * v
    o_ref[...] = (normed + bonus) * g_a


def _block_masks(n_blocks, c):
    r = np.arange(n_blocks * c)
    same = (r[:, None] // c) == (r[None, :] // c)
    causal = (same & (r[None, :] <= r[:, None])).astype(np.float32)
    strict = (same & (r[None, :] < r[:, None])).astype(np.float32)
    return causal, strict


def _rwkv_masks(c, nb):
    rows = nb * RW_GROUP * c
    rw_pad = -(-rows // LANES) * LANES
    lane_head = np.arange(RW_GW) // RW_HEAD
    hm = (lane_head[None, :] == np.arange(RW_GROUP)[:, None]).astype(np.float32)
    incl, strict = _block_masks(nb * RW_GROUP, c)
    widen = lambda m: np.tile(np.pad(m, ((0, 0), (0, rw_pad - rows))), (1, 2))
    return rw_pad, jnp.asarray(hm), jnp.asarray(widen(strict)), jnp.asarray(widen(incl))


def _rwkv(pa, shift0, wkv0, layer, prm, *, batch, seq, c, nb, row_block0):
    nc = seq // c
    assert batch % nb == 0 and (nb == 1 or nc == 1)
    rw_pad, hm, m_strict, m_incl = _rwkv_masks(c, nb)
    rows = nb * RW_GROUP * c
    tri = jnp.asarray(_block_masks(nb, c)[0]).astype(BF16)
    kern = functools.partial(_rwkv_kernel, c=c, nb=nb, n_chunks=nc, rw_pad=rw_pad)
    vec = lambda n: pl.BlockSpec((1, n), lambda b, ci: (0, 0))
    mat = lambda r_, n: pl.BlockSpec((r_, n), lambda b, ci: (0, 0))
    prm = tuple(prm) + (hm, m_strict, m_incl, tri)
    return pl.pallas_call(
        kern,
        out_shape=(jax.ShapeDtypeStruct((batch * seq, BR_W), F32),
                   jax.ShapeDtypeStruct((batch, RW_HEADS, RW_HEAD, RW_HEAD), F32),
                   jax.ShapeDtypeStruct((batch, 1, RW_COLS), F32)),
        grid=(batch // nb, nc),
        in_specs=[pl.BlockSpec((nb * c, RW_COLS), lambda b, ci: (row_block0 // nb + b * nc + ci, 0)),
                  pl.BlockSpec((None, nb, 1, RW_COLS), lambda b, ci: (layer, b, 0, 0)),
                  pl.BlockSpec((None, nb, RW_HEADS, RW_HEAD, RW_HEAD), lambda b, ci: (layer, b, 0, 0, 0)),
                  vec(RW_COLS), vec(512), mat(128, 512), vec(512), mat(128, 512), mat(128, 512),
                  vec(512), vec(512), vec(512), vec(512), vec(512), mat(512, 512),
                  mat(RW_GROUP, RW_GW), mat(rows, 2 * rw_pad), mat(rows, 2 * rw_pad), mat(nb * c, nb * c)],
        out_specs=(pl.BlockSpec((nb * c, BR_W), lambda b, ci: (b * nc + ci, 0)),
                   pl.BlockSpec((nb, RW_HEADS, RW_HEAD, RW_HEAD), lambda b, ci: (b, 0, 0, 0)),
                   pl.BlockSpec((nb, 1, RW_COLS), lambda b, ci: (b, 0, 0))),
        scratch_shapes=[pltpu.VMEM((nb, SUBLANES + c, RW_COLS), F32),
                        pltpu.VMEM((nb, RW_HEADS // RW_GROUP, RW_GW, RW_GW), F32)],
        compiler_params=_cparams(2),
        name="rwkv7",
    )(pa, shift0, wkv0, *prm)


def _ret_kernel(pb_ref, cos_ref, sin_ref, s0_ref, o_ref, s_ref, *, c):
    ci = pl.program_id(1)

    @pl.when(ci == 0)
    def _init():
        s_ref[...] = s0_ref[...]

    cos = cos_ref[...]
    sin = sin_ref[...]
    width = RET_HEADS * RET_DK
    half = RET_DK // 2
    first_half = (_iota((c, width), 1) & half) == 0

    def rot(x):
        partner = jnp.where(first_half, pltpu.roll(x, width - half, 1), pltpu.roll(x, half, 1))
        return x * cos + partner * sin

    q = rot(pb_ref[:, 0:width])
    k = rot(pb_ref[:, width:2 * width]) * (RET_DK ** -0.5)
    v = pb_ref[:, 2 * width:2 * width + BR_W]
    g_b = pb_ref[:, 2 * width + BR_W:2 * width + 2 * BR_W]

    row = _iota((c, c), 0)
    col = _iota((c, c), 1)
    causal = row >= col
    dist = jnp.where(causal, (row - col).astype(F32), 0.0)
    t1 = (_iota((c, 1), 0) + 1).astype(F32)
    for h in range(RET_HEADS):
        log_gamma = math.log1p(-(2.0 ** (-5.0 - h)))
        dmat = jnp.where(causal, jnp.exp(dist * log_gamma), 0.0)
        q_h = q[:, h * RET_DK:(h + 1) * RET_DK]
        k_h = k[:, h * RET_DK:(h + 1) * RET_DK]
        v_h = v[:, h * RET_DV:(h + 1) * RET_DV]
        s_h = s_ref[h]
        scores = _mm_nt(q_h, k_h) * dmat
        o_h = _mm(scores, v_h) + jnp.exp(t1 * log_gamma) * _mm(q_h, s_h)
        k_dec = k_h * jnp.exp((c - t1) * log_gamma)
        s_ref[h] = math.exp(c * log_gamma) * s_h + _mm_tn(k_dec, v_h)
        o_n = o_h * lax.rsqrt(jnp.mean(o_h * o_h, axis=-1, keepdims=True) + NORM_EPS)
        o_ref[:, h * RET_DV:(h + 1) * RET_DV] = o_n * _silu(g_b[:, h * RET_DV:(h + 1) * RET_DV])


def _ret(pb, cos, sin, ret0, layer, *, batch, seq, c, row_block0):
    nc = seq // c
    width = RET_HEADS * RET_DK
    return pl.pallas_call(
        functools.partial(_ret_kernel, c=c),
        out_shape=(jax.ShapeDtypeStruct((batch * seq, BR_W), F32),
                   jax.ShapeDtypeStruct((batch, RET_HEADS, RET_DK, RET_DV), F32)),
        grid=(batch, nc),
        in_specs=[pl.BlockSpec((c, RET_COLS), lambda b, ci: (row_block0 + b * nc + ci, 0)),
                  pl.BlockSpec((c, width), lambda b, ci: (ci, 0)),
                  pl.BlockSpec((c, width), lambda b, ci: (ci, 0)),
                  pl.BlockSpec((None, None, RET_HEADS, RET_DK, RET_DV), lambda b, ci: (layer, b, 0, 0, 0))],
        out_specs=(pl.BlockSpec((c, BR_W), lambda b, ci: (b * nc + ci, 0)),
                   pl.BlockSpec((None, RET_HEADS, RET_DK, RET_DV), lambda b, ci: (b, 0, 0, 0))),
        compiler_params=_cparams(2),
        name="retention",
    )(pb, cos, sin, ret0)


def _hgrn_kernel(pc_ref, lbp_ref, ng_ref, s0_ref, o_ref, s_ref, st_scr, *, c, m, layer, n_chunks):
    ci = pl.program_id(1)

    @pl.when(ci == 0)
    def _init():
        for h in range(HG_HEADS):
            st_scr[h] = s0_ref[h].T

    lbp = lbp_ref[...]
    e = jnp.exp(lbp - jnp.max(lbp, axis=0, keepdims=True))
    soft = e / jnp.sum(e, axis=0, keepdims=True)
    lb = jnp.zeros((1, BR_W), F32)
    for j in range(1, layer + 1):
        lb = lb + soft[j:j + 1, :]

    q = _silu(pc_ref[:, 0:BR_W])
    f_in = pc_ref[:, BR_W:2 * BR_W]
    v = pc_ref[:, 2 * BR_W:3 * BR_W]
    g_c = pc_ref[:, 3 * BR_W:4 * BR_W]
    log_f = jnp.log(lb + (1.0 - lb) * _sigmoid(f_in))
    k = (1.0 - lb) * _sigmoid(-f_in)

    tri = (_iota((c, c), 0) >= _iota((c, c), 1)).astype(BF16)
    cum = _mm_mask(tri, log_f)
    cum_last = cum[c - 1:c, :]
    q_e = q * jnp.exp(cum)
    k_e = k * jnp.exp(cum_last - cum)
    p_end = jnp.exp(cum_last)

    col_m = _iota((m, m), 1)
    row_l = _iota((m, LANES), 0)
    for h in range(HG_HEADS):
        sl = slice(h * HG_D, (h + 1) * HG_D)
        st = st_scr[h]
        o_rows = []
        for i in range(c // m):
            r0 = i * m
            q_i = q[r0:r0 + m, sl]
            c_i = cum[r0:r0 + m, sl]
            k_i = k[r0:r0 + m, sl]
            diag = jnp.zeros((m, m), F32)
            for s in range(m):
                keep = row_l >= s
                decay = jnp.exp(jnp.where(keep, c_i - c_i[s:s + 1, :], 0.0))
                col = jnp.sum(jnp.where(keep, q_i * k_i[s:s + 1, :] * decay, 0.0), axis=-1, keepdims=True)
                diag = jnp.where(col_m == s, col, diag)
            o_i = _mm(diag, v[r0:r0 + m, sl])
            if i > 0:
                c_ref = cum[r0 - 1:r0, sl]
                q_s = q_i * jnp.exp(c_i - c_ref)
                k_s = k[0:r0, sl] * jnp.exp(c_ref - cum[0:r0, sl])
                o_i = o_i + _mm(_mm_nt(q_s, k_s), v[0:r0, sl])
            o_rows.append(o_i)
        o_h = _mm_nt(q_e[:, sl], st) + jnp.concatenate(o_rows, axis=0)
        st_scr[h] = st * p_end[:, sl] + _mm_tn(v[:, sl], k_e[:, sl])
        o_n = o_h * lax.rsqrt(jnp.mean(o_h * o_h, axis=-1, keepdims=True) + NORM_EPS) * ng_ref[:, sl]
        o_ref[:, sl] = o_n * _silu(g_c[:, sl])

    @pl.when(ci == n_chunks - 1)
    def _fin():
        for h in range(HG_HEADS):
            s_ref[h] = st_scr[h].T


def _hgrn(pc, lbp, ng, hg0, state_layer, layer, *, batch, seq, c, row_block0):
    nc = seq // c
    m = 16 if c % 16 == 0 else SUBLANES
    depth = lbp.shape[0]
    return pl.pallas_call(
        functools.partial(_hgrn_kernel, c=c, m=m, layer=layer, n_chunks=nc),
        out_shape=(jax.ShapeDtypeStruct((batch * seq, BR_W), F32),
                   jax.ShapeDtypeStruct((batch, HG_HEADS, HG_D, HG_D), F32)),
        grid=(batch, nc),
        in_specs=[pl.BlockSpec((c, HG_COLS), lambda b, ci: (row_block0 + b * nc + ci, 0)),
                  pl.BlockSpec((depth, BR_W), lambda b, ci: (0, 0)),
                  pl.BlockSpec((1, BR_W), lambda b, ci: (0, 0)),
                  pl.BlockSpec((None, None, HG_HEADS, HG_D, HG_D), lambda b, ci: (state_layer, b, 0, 0, 0))],
        out_specs=(pl.BlockSpec((c, BR_W), lambda b, ci: (b * nc + ci, 0)),
                   pl.BlockSpec((None, HG_HEADS, HG_D, HG_D), lambda b, ci: (b, 0, 0, 0))),
        scratch_shapes=[pltpu.VMEM((HG_HEADS, HG_D, HG_D), F32)],
        compiler_params=_cparams(2),
        name="hgrn2",
    )(pc, lbp, ng, hg0)


def _gdn_kernel(pd_ref, cs0_ref, s0_ref, cw_ref, alog_ref, dtb_ref, ng_ref, mc_ref, mst_ref,
                tri_ref, o_ref, s_ref, cs_ref, ext_scr, *, c, nb):
    ci = pl.program_id(1)
    keep = GD_CONV - 1

    @pl.when(ci == 0)
    def _init():
        for j in range(nb):
            ext_scr[j, SUBLANES - keep:SUBLANES, :] = cs0_ref[j]
        s_ref[...] = s0_ref[...]

    convs = []
    for j in range(nb):
        x = pd_ref[j * c:(j + 1) * c, 0:GD_QKV]
        ext_scr[j, SUBLANES:SUBLANES + c, :] = x
        conv = ext_scr[j, SUBLANES - keep:SUBLANES - keep + c, :] * cw_ref[0:1, :]
        for i in range(1, keep):
            conv = conv + ext_scr[j, SUBLANES - keep + i:SUBLANES - keep + i + c, :] * cw_ref[i:i + 1, :]
        convs.append(conv + x * cw_ref[keep:keep + 1, :])
        tail = ext_scr[j, SUBLANES + c - keep:SUBLANES + c, :]
        ext_scr[j, SUBLANES - keep:SUBLANES, :] = tail
        cs_ref[j] = tail

    act = _silu(jnp.concatenate(convs, axis=0))
    q = act[:, 0:BR_W]
    k = act[:, BR_W:2 * BR_W]
    v = act[:, 2 * BR_W:3 * BR_W]
    g_d = pd_ref[:, GD_QKV:GD_QKV + BR_W]
    ba = pd_ref[:, GD_QKV + BR_W:GD_QKV + BR_W + LANES]
    beta_all = _sigmoid(ba)
    g_all = -jnp.exp(alog_ref[...]) * _softplus(ba + dtb_ref[...])

    cum_all = _mm_mask(tri_ref[...], g_all)

    rows = GD_HEADS * nb * c
    stack = lambda parts: jnp.concatenate(parts, axis=0)
    l2n = lambda z: z * lax.rsqrt(jnp.sum(z * z, axis=-1, keepdims=True) + 1e-6)
    q4 = stack([l2n(q[:, h * GD_D:(h + 1) * GD_D]) for h in range(GD_HEADS)]) * (GD_D ** -0.5)
    k4 = stack([l2n(k[:, h * GD_D:(h + 1) * GD_D]) for h in range(GD_HEADS)])
    v4 = stack([v[:, h * GD_D:(h + 1) * GD_D] for h in range(GD_HEADS)])
    beta = stack([beta_all[:, h:h + 1] for h in range(GD_HEADS)])
    cum = stack([cum_all[:, GD_HEADS + h:GD_HEADS + h + 1] for h in range(GD_HEADS)])
    m_causal = mc_ref[...]
    causal = m_causal > 0.5
    eye = _iota((rows, rows), 0) == _iota((rows, rows), 1)
    cum_t = jnp.broadcast_to(cum, (rows, rows))
    cum_s = _mm_mask(jnp.ones((rows, rows), BF16), jnp.where(eye, cum_t, 0.0))
    dmat = jnp.where(causal, jnp.exp(jnp.where(causal, cum_t - cum_s, 0.0)), 0.0)
    a_mat = beta * _mm_nt(k4, k4) * (dmat * mst_ref[...])
    inv, = _unit_lower_inverse([-a_mat], c)
    e_cum = jnp.exp(cum)
    sol = _mm(inv, jnp.concatenate([(beta * e_cum) * k4, beta * v4], axis=1))
    blocks = [(h, j) for h in range(GD_HEADS) for j in range(nb)]
    rows_of = lambda h, j: slice((h * nb + j) * c, (h * nb + j + 1) * c)
    qw = [_mm(jnp.concatenate([q4[rows_of(h, j)], sol[rows_of(h, j), 0:GD_D]], axis=0), s_ref[j, h])
          for h, j in blocks]
    delta = sol[:, GD_D:2 * GD_D] - stack([z[c:2 * c] for z in qw])
    o4 = e_cum * stack([z[0:c] for z in qw]) + _mm(_mm_nt(q4, k4) * dmat, delta)
    for h, j in blocks:
        hs = rows_of(h, j)
        cum_h = cum[hs]
        cum_last = cum_h[c - 1:c, :]
        s_ref[j, h] = jnp.exp(cum_last) * s_ref[j, h] + _mm_tn(k4[hs] * jnp.exp(cum_last - cum_h), delta[hs])
    for h in range(GD_HEADS):
        sl = slice(h * GD_D, (h + 1) * GD_D)
        o_h = o4[h * nb * c:(h + 1) * nb * c]
        o_n = o_h * lax.rsqrt(jnp.mean(o_h * o_h, axis=-1, keepdims=True) + NORM_EPS) * ng_ref[:, sl]
        o_ref[:, sl] = o_n * _silu(g_d[:, sl])


def _gdn(pd, conv0, gd0, layer, cw, alog, dtb, ng, *, batch, seq, c, nb, row_block0):
    nc = seq // c
    assert batch % nb == 0 and (nb == 1 or nc == 1)
    keep = GD_CONV - 1
    rows = GD_HEADS * nb * c
    m_causal, m_strict = (jnp.asarray(m) for m in _block_masks(GD_HEADS * nb, c))
    tri = jnp.asarray(_block_masks(nb, c)[0]).astype(BF16)
    full = lambda *shape: pl.BlockSpec(shape, lambda b, ci: (0,) * len(shape))
    return pl.pallas_call(
        functools.partial(_gdn_kernel, c=c, nb=nb),
        out_shape=(jax.ShapeDtypeStruct((batch * seq, BR_W), F32),
                   jax.ShapeDtypeStruct((batch, GD_HEADS, GD_D, GD_D), F32),
                   jax.ShapeDtypeStruct((batch, keep, GD_QKV), F32)),
        grid=(batch // nb, nc),
        in_specs=[pl.BlockSpec((nb * c, GD_COLS_PAD), lambda b, ci: (row_block0 // nb + b * nc + ci, 0)),
                  pl.BlockSpec((None, nb, keep, GD_QKV), lambda b, ci: (layer, b, 0, 0)),
                  pl.BlockSpec((None, nb, GD_HEADS, GD_D, GD_D), lambda b, ci: (layer, b, 0, 0, 0)),
                  full(GD_CONV, GD_QKV), full(1, LANES), full(1, LANES), full(1, BR_W),
                  full(rows, rows), full(rows, rows), full(nb * c, nb * c)],
        out_specs=(pl.BlockSpec((nb * c, BR_W), lambda b, ci: (b * nc + ci, 0)),
                   pl.BlockSpec((nb, GD_HEADS, GD_D, GD_D), lambda b, ci: (b, 0, 0, 0)),
                   pl.BlockSpec((nb, keep, GD_QKV), lambda b, ci: (b, 0, 0))),
        scratch_shapes=[pltpu.VMEM((nb, SUBLANES + c, GD_QKV), F32)],
        compiler_params=_cparams(2),
        name="gdn",
    )(pd, conv0, gd0, cw, alog, dtb, ng, m_causal, m_strict, tri)


def _pick_chunk(seq):
    for c in (64, 56, 48, 40, 32, 24, 16, 8):
        if seq % c == 0:
            return c
    raise ValueError(f"sequence length {seq} is not a multiple of {SUBLANES}")


def _pick_tile(n, candidates):
    for t in candidates:
        if n % t == 0:
            return t
    raise ValueError(f"no tile in {candidates} divides {n}")


def _rope_tables(pos):
    half = RET_DK // 2
    inv = ROPE_BASE ** (-jnp.arange(half, dtype=F32) / half)
    ang = pos.astype(F32)[:, None] * inv[None, :]
    cos, sin = jnp.cos(ang), jnp.sin(ang)
    cos_t = jnp.tile(jnp.concatenate([cos, cos], axis=1), (1, RET_HEADS))
    sin_t = jnp.tile(jnp.concatenate([-sin, sin], axis=1), (1, RET_HEADS))
    return cos_t, sin_t


def kernel(x_prompt, x_sample, state_rwkv_wkv, state_rwkv_shift, state_ret, state_hgrn, state_gdn, state_gdn_conv, meta_tokens, norm_mix, w_in, rw_mu, rw_w0, rw_w2, rw_a0, rw_a2, rw_g2, rw_kk, rw_ka, rw_rk, rw_ln_g, rw_ln_b, hg_lb, hg_norm_g, gd_conv, gd_a_log, gd_dt_bias, gd_norm_g, w_branch, w_out, norm_ffn, w_up, w_down, norm_final):
    depth = norm_mix.shape[0]
    bp, tq, d = x_prompt.shape
    tp = tq + N_META
    bs, ts, _ = x_sample.shape
    n_p, n_s = bp * tp, bs * ts
    n_tok = n_p + n_s
    cp, cs = _pick_chunk(tp), _pick_chunk(ts)
    assert cs == ts and n_p % ts == 0
    tm = _pick_tile(n_tok, (464, 512, 256, 232, 128, 64, 32, 16, 8))
    tm_in = _pick_tile(n_tok, (928, 464, 512, 256, 232, 128, 64, 32, 16, 8))
    nb_s = _pick_tile(math.gcd(bs, n_p // ts), (8, 4, 2, 1))

    meta = jnp.broadcast_to(meta_tokens.astype(F32)[None], (bp, N_META, d))
    x = jnp.concatenate([jnp.concatenate([meta, x_prompt], axis=1).reshape(n_p, d), x_sample.reshape(n_s, d)], axis=0)

    cos_p, sin_p = _rope_tables(jnp.arange(tp, dtype=jnp.int32))
    cos_s, sin_s = _rope_tables(PAST_LEN + jnp.arange(ts, dtype=jnp.int32))

    head_of = jnp.arange(BR_W) // RW_HEAD
    bd = (head_of[:, None] == head_of[None, :]).astype(BF16)

    zeros = lambda *s: jnp.zeros((1, bp) + s, F32)
    z_wkv, z_shift = zeros(RW_HEADS, RW_HEAD, RW_HEAD), zeros(1, RW_COLS)
    z_ret, z_hg, z_gd = zeros(RET_HEADS, RET_DK, RET_DV), zeros(HG_HEADS, HG_D, HG_D), zeros(GD_HEADS, GD_D, GD_D)
    z_conv = zeros(GD_CONV - 1, GD_QKV)
    shift_s = state_rwkv_shift.reshape(depth, bs, 1, RW_COLS)

    o_a, o_b, o_c = RW_COLS, RW_COLS + RET_COLS, RW_COLS + RET_COLS + HG_COLS
    o_g = o_c + GD_QKV + 2 * GD_HEADS + BR_W
    new_p = [[] for _ in range(6)]
    new_s = [[] for _ in range(6)]
    for l in range(depth):
        wl = w_in[l]
        w_a = wl[:, 0:o_a].astype(BF16)
        w_b = wl[:, o_a:o_b].astype(BF16)
        w_c = wl[:, o_b:o_c].astype(BF16)
        w_d = jnp.concatenate([wl[:, o_c:o_c + GD_QKV], wl[:, o_c + GD_QKV + 2 * GD_HEADS:o_g],
                               wl[:, o_c + GD_QKV:o_c + GD_QKV + 2 * GD_HEADS],
                               jnp.zeros((d, LANES - 2 * GD_HEADS), F32)], axis=1).astype(BF16)
        w_g = wl[:, o_g:].astype(BF16)

        h = _rmsnorm(x, norm_mix[l], tm, BF16)
        pa = _matmul(h, w_a, tm_in, 896, "in_proj_a")
        pb = _matmul(h, w_b, tm_in, 768, "in_proj_b")
        pc = _matmul(h, w_c, tm_in, 1024, "in_proj_c")
        pd = _matmul(h, w_d, tm_in, GD_COLS_PAD, "in_proj_d")
        pg = _matmul(h, w_g, tm_in, 1024, "in_proj_g")

        row1 = lambda a: a.reshape(1, -1)
        pad_rows = lambda a, top: jnp.concatenate(
            [jnp.zeros((top, BR_W), F32), a, jnp.zeros((LANES - top - a.shape[0], BR_W), F32)], axis=0).astype(BF16)
        rw_prm = (row1(rw_mu[l]), row1(rw_w0[l]), pad_rows(rw_w2[l], 0), row1(rw_a0[l]), pad_rows(rw_a2[l], 64),
                  rw_g2[l].astype(BF16), row1(rw_kk[l]), row1(rw_ka[l]), row1(rw_rk[l]), row1(rw_ln_g[l]),
                  row1(rw_ln_b[l]), bd)
        lane_pad = lambda a: jnp.concatenate(
            [jnp.zeros((GD_HEADS,), F32), a, jnp.zeros((LANES - 2 * GD_HEADS,), F32)]).reshape(1, LANES)
        gd_prm = (gd_conv[l], lane_pad(gd_a_log[l]), lane_pad(gd_dt_bias[l]), row1(gd_norm_g[l]))
        hg_g = row1(hg_norm_g[l])

        outs = []
        for grp, (batch, seq, c, nb, rb0, sts, lay, cos, sin) in enumerate((
                (bp, tp, cp, 1, 0, (z_wkv, z_shift, z_ret, z_hg, z_gd, z_conv), 0, cos_p, sin_p),
                (bs, ts, cs, nb_s, n_p // ts,
                 (state_rwkv_wkv, shift_s, state_ret, state_hgrn, state_gdn, state_gdn_conv), l, cos_s, sin_s))):
            kw = dict(batch=batch, seq=seq, c=c, row_block0=rb0)
            oa, wkv, shift = _rwkv(pa, sts[1], sts[0], lay, rw_prm, nb=nb, **kw)
            ob, ret = _ret(pb, cos, sin, sts[2], lay, **kw)
            oc, hg = _hgrn(pc, hg_lb, hg_g, sts[3], lay, l, **kw)
            od, gd, conv = _gdn(pd, sts[5], sts[4], lay, *gd_prm, nb=nb, **kw)
            outs.append((oa, ob, oc, od))
            dst = new_p if grp == 0 else new_s
            for i, st in enumerate((wkv, shift.reshape(batch, RW_COLS), ret, hg, gd, conv)):
                dst[i].append(st)
        branch = [jnp.concatenate([outs[0][n], outs[1][n]], axis=0) for n in range(4)]
        wb = w_branch[l].astype(BF16)
        x = _merge(x, branch, pg, wb, w_out[l].astype(BF16), tm, 512)
        x = _ffn(x, norm_ffn[l], w_up[l].astype(BF16), w_down[l].astype(BF16), tm, 512)

    y = _rmsnorm(x, norm_final, tm, F32)
    y_prompt = y[:n_p].reshape(bp, tp, d)[:, N_META:]
    y_sample = y[n_p:].reshape(bs, ts, d)
    p_states = tuple(jnp.stack(z) for z in new_p)
    s_states = tuple(jnp.stack(z) for z in new_s)
    return (y_prompt, y_sample) + p_states + s_states
```

```python
import functools
import math

import jax
import jax.numpy as jnp
import numpy as np
from jax import lax
from jax.experimental import pallas as pl
from jax.experimental.pallas import tpu as pltpu

F32 = jnp.float32
BF16 = jnp.bfloat16

N_META = 16
PAST_LEN = 16384
NORM_EPS = 1e-6
RW_LN_EPS = 64e-5
ROPE_BASE = 10000.0

D_MODEL = 2048
BR_W = 512
RW_HEADS, RW_HEAD = 8, 64
RW_COLS = 1792
RW_GROUP = 4
RW_GW = RW_GROUP * RW_HEAD
RET_HEADS, RET_DK, RET_DV = 4, 64, 128
RET_COLS = 1536
HG_HEADS, HG_D = 4, 128
HG_COLS = 2048
GD_HEADS, GD_D = 4, 128
GD_CONV = 4
GD_QKV = 1536
GD_COLS_PAD = 2176

LANES = 128
SUBLANES = 8
VMEM_LIMIT = 56 * 1024 * 1024


def _cparams(n_axes):
    return pltpu.CompilerParams(dimension_semantics=("arbitrary",) * n_axes, vmem_limit_bytes=VMEM_LIMIT)


def _dg(a, b, ca, cb):
    return lax.dot_general(a, b, (((ca,), (cb,)), ((), ())), preferred_element_type=F32)


def _mm(a, b):
    return _dg(a.astype(BF16), b.astype(BF16), 1, 0)


def _mm_nt(a, b):
    return _dg(a.astype(BF16), b.astype(BF16), 1, 1)


def _mm_tn(a, b):
    return _dg(a.astype(BF16), b.astype(BF16), 0, 0)


def _split(x, n):
    parts, r = [], x
    for _ in range(n):
        p = r.astype(BF16)
        parts.append(p)
        r = r - p.astype(F32)
    return parts


def _mm_mask(m01, x):
    x0, x1, x2 = _split(x, 3)
    return _dg(m01, x0, 1, 0) + (_dg(m01, x1, 1, 0) + _dg(m01, x2, 1, 0))


def _iota(shape, dim):
    return lax.broadcasted_iota(jnp.int32, shape, dim)


def _softplus(x):
    return jnp.maximum(x, 0.0) + jnp.log(1.0 + jnp.exp(-jnp.abs(x)))


def _sigmoid(x):
    return jax.nn.sigmoid(x)


def _silu(x):
    return x * jax.nn.sigmoid(x)


def _unit_lower_inverse(lows, c):
    shape = lows[0].shape
    eye = (_iota(shape, 0) == _iota(shape, 1)).astype(F32)
    invs = [eye + low for low in lows]
    powers = list(lows)
    covered = 1
    while covered < c - 1:
        powers = [_mm(p, p) for p in powers]
        invs = [inv + _mm(inv, p) for inv, p in zip(invs, powers)]
        covered = 2 * covered + 1
    return invs


def _rmsnorm_kernel(x_ref, g_ref, o_ref):
    x = x_ref[...]
    y = x * lax.rsqrt(jnp.mean(x * x, axis=-1, keepdims=True) + NORM_EPS)
    o_ref[...] = (y * g_ref[...]).astype(o_ref.dtype)


def _rmsnorm(x, g, tm, out_dtype):
    n, d = x.shape
    return pl.pallas_call(
        _rmsnorm_kernel,
        out_shape=jax.ShapeDtypeStruct((n, d), out_dtype),
        grid=(n // tm,),
        in_specs=[pl.BlockSpec((tm, d), lambda i: (i, 0)), pl.BlockSpec((1, d), lambda i: (0, 0))],
        out_specs=pl.BlockSpec((tm, d), lambda i: (i, 0)),
        compiler_params=_cparams(1),
        name="rmsnorm",
    )(x, g.reshape(1, d))


def _matmul_kernel(x_ref, w_ref, o_ref):
    o_ref[...] = _dg(x_ref[...], w_ref[...], 1, 0)


def _matmul_sigmoid_kernel(x_ref, w_ref, o_ref):
    o_ref[...] = _sigmoid(_dg(x_ref[...], w_ref[...], 1, 0)).astype(o_ref.dtype)


def _matmul(x, w, tm, tn, name, gate=False):
    n, k = x.shape
    m = w.shape[1]
    return pl.pallas_call(
        _matmul_sigmoid_kernel if gate else _matmul_kernel,
        out_shape=jax.ShapeDtypeStruct((n, m), BF16 if gate else F32),
        grid=(n // tm, m // tn),
        in_specs=[pl.BlockSpec((tm, k), lambda i, j: (i, 0)), pl.BlockSpec((k, tn), lambda i, j: (0, j))],
        out_specs=pl.BlockSpec((tm, tn), lambda i, j: (i, j)),
        compiler_params=_cparams(2),
        name=name,
    )(x, w)


def _merge_kernel(x_ref, oa_ref, ob_ref, oc_ref, od_ref, g0_ref, g1_ref, g2_ref, g3_ref, wb_ref, wo_ref, y_ref):
    j = pl.program_id(1)

    @pl.when(j == 0)
    def _init():
        y_ref[...] = x_ref[...]

    merged = None
    for n, (o_ref, g_ref) in enumerate(((oa_ref, g0_ref), (ob_ref, g1_ref), (oc_ref, g2_ref), (od_ref, g3_ref))):
        term = g_ref[...].astype(F32) * _dg(o_ref[...], wb_ref[n], 1, 0)
        merged = term if merged is None else merged + term
    y_ref[...] += _dg(merged.astype(BF16), wo_ref[...], 1, 0)


def _merge(x, outs, pg, wb, wo, tm, tj):
    n, d = x.shape
    nj = d // tj
    o_spec = pl.BlockSpec((tm, BR_W), lambda i, j: (i, 0))
    g_specs = [pl.BlockSpec((tm, tj), functools.partial(lambda i, j, nb: (i, nb * nj + j), nb=nb)) for nb in range(4)]
    return pl.pallas_call(
        _merge_kernel,
        out_shape=jax.ShapeDtypeStruct((n, d), F32),
        grid=(n // tm, nj),
        in_specs=[pl.BlockSpec((tm, d), lambda i, j: (i, 0)), o_spec, o_spec, o_spec, o_spec, *g_specs,
                  pl.BlockSpec((4, BR_W, tj), lambda i, j: (0, 0, j)),
                  pl.BlockSpec((tj, d), lambda i, j: (j, 0))],
        out_specs=pl.BlockSpec((tm, d), lambda i, j: (i, 0)),
        compiler_params=_cparams(2),
        name="merge",
    )(x, *outs, pg, pg, pg, pg, wb, wo)


def _ffn_kernel(x_ref, g_ref, wu_ref, wg_ref, wd_ref, y_ref, h_ref, acc_ref):
    j = pl.program_id(1)

    @pl.when(j == 0)
    def _init():
        x = x_ref[...]
        y = x * lax.rsqrt(jnp.mean(x * x, axis=-1, keepdims=True) + NORM_EPS)
        h_ref[...] = (y * g_ref[...]).astype(BF16)
        acc_ref[...] = jnp.zeros_like(acc_ref)

    h = h_ref[...]
    up = _dg(h, wu_ref[...], 1, 0)
    gate = _dg(h, wg_ref[...], 1, 0)
    acc_ref[...] += _dg((_silu(gate) * up).astype(BF16), wd_ref[...], 1, 0)

    @pl.when(j == pl.num_programs(1) - 1)
    def _fin():
        y_ref[...] = x_ref[...] + acc_ref[...]


def _ffn(x, g, w_up, w_down, tm, tf):
    n, d = x.shape
    ff = w_down.shape[0]
    nf = ff // tf
    return pl.pallas_call(
        _ffn_kernel,
        out_shape=jax.ShapeDtypeStruct((n, d), F32),
        grid=(n // tm, nf),
        in_specs=[pl.BlockSpec((tm, d), lambda i, j: (i, 0)), pl.BlockSpec((1, d), lambda i, j: (0, 0)),
                  pl.BlockSpec((d, tf), lambda i, j: (0, j)),
                  pl.BlockSpec((d, tf), lambda i, j: (0, nf + j)),
                  pl.BlockSpec((tf, d), lambda i, j: (j, 0))],
        out_specs=pl.BlockSpec((tm, d), lambda i, j: (i, 0)),
        scratch_shapes=[pltpu.VMEM((tm, d), BF16), pltpu.VMEM((tm, d), F32)],
        compiler_params=_cparams(2),
        name="ffn",
    )(x, g.reshape(1, d), w_up, w_up, w_down)


def _rwkv_kernel(*refs, c, nb, ns, n_chunks, rw_pad, n_alias):
    refs = refs[n_alias:]
    tok_refs = refs[:nb]
    (sh0_ref, s0_ref, mu_ref, w0_ref, w2_ref, a0_ref, a2_ref, g2_ref, kkw_ref, ka_ref, rk_ref, lng_ref, lnb_ref,
     bd_ref, hm_ref, ms_ref, mi_ref, tri_ref, o_ref, s_ref, sh_ref, xs_scr, sbd_scr) = refs[nb:]
    ci = pl.program_id(1)
    seqs = range(nb)
    head_block = lambda h: (h // RW_GROUP, slice((h % RW_GROUP) * RW_HEAD, (h % RW_GROUP + 1) * RW_HEAD))

    @pl.when(ci == 0)
    def _init():
        sbd_scr[...] = jnp.zeros_like(sbd_scr)
        for j in seqs:
            xs_scr[j, SUBLANES - 1:SUBLANES, :] = sh0_ref[j]
            for h in range(RW_HEADS):
                g, sl = head_block(h)
                sbd_scr[j, g, sl, sl] = s0_ref[j, h]

    pa = jnp.concatenate([t[...] for t in tok_refs], axis=0)
    prevs = []
    for j in seqs:
        pa_j = pa[j * c:(j + 1) * c, :]
        xs_scr[j, SUBLANES:SUBLANES + c, :] = pa_j
        prevs.append(xs_scr[j, SUBLANES - 1:SUBLANES - 1 + c, :])
        last = pa_j[c - 1:c, :]
        xs_scr[j, SUBLANES - 1:SUBLANES, :] = last
        sh_ref[j] = last
    prev = jnp.concatenate(prevs, axis=0)

    xm = pa + (prev - pa) * mu_ref[...]
    r = xm[:, 0:512]
    k0 = xm[:, 512:1024]
    v = xm[:, 1024:1536]
    wa = xm[:, 1536:1664]
    gl = xm[:, 1664:1792]
    w_log = -_softplus(-(w0_ref[...] + _mm(jnp.tanh(wa), w2_ref[...]))) - 0.5
    ld = -jnp.exp(w_log)
    asig = _sigmoid(a0_ref[...] + _mm(wa, a2_ref[...]))
    g_a = _mm(_sigmoid(gl), g2_ref[...])
    bd = bd_ref[...]
    kkr = k0 * kkw_ref[...]
    kkn = kkr * lax.rsqrt(_mm(kkr * kkr, bd) + 1e-6)
    k1 = k0 * (1.0 + (asig - 1.0) * ka_ref[...])
    a_vec = -kkn
    b_vec = kkn * asig

    cum = _mm_mask(tri_ref[...], ld)
    cum_ends = [cum[(j + 1) * c - 1:(j + 1) * c, :] for j in seqs]
    cum_last = jnp.concatenate([jnp.broadcast_to(e, (c, BR_W)) for e in cum_ends], axis=0)
    a_t = a_vec * jnp.exp(cum - ld)
    r_t = r * jnp.exp(cum)
    inv_p = jnp.exp(-cum)
    b_t = b_vec * inv_p
    k_t = k1 * inv_p
    to_end = jnp.exp(cum_last - cum)
    b_e = b_vec * to_end
    k_e = k1 * to_end

    seq_rows = RW_GROUP * c
    rows = ns * seq_rows
    pad = [jnp.zeros((rw_pad - rows, RW_GW), F32)] if rw_pad > rows else []
    m_strict = ms_ref[...]
    m_incl = mi_ref[...]
    groups = range(RW_HEADS // RW_GROUP)
    units = [(s0, g) for s0 in range(0, nb, ns) for g in groups]
    lanes = lambda g: slice(g * RW_GW, (g + 1) * RW_GW)
    of_seq = lambda x, i: x[i * seq_rows:(i + 1) * seq_rows]

    def stack(x, unit):
        s0, g = unit
        return jnp.concatenate([x[j * c:(j + 1) * c, lanes(g)] * hm_ref[hh:hh + 1, :]
                                for j in range(s0, s0 + ns) for hh in range(RW_GROUP)], axis=0)

    a4 = [stack(a_t, un) for un in units]
    r4 = [stack(r_t, un) for un in units]
    v4 = [stack(v, un) for un in units]
    gram = [_mm_nt(jnp.concatenate([a4[n], r4[n]], axis=0),
                   jnp.concatenate([stack(b_t, un), *pad, stack(k_t, un), *pad], axis=0))
            for n, un in enumerate(units)]
    from_state = [[_mm_nt(jnp.concatenate([of_seq(a4[n], i), of_seq(r4[n], i)], axis=0), sbd_scr[s0 + i, g])
                   for i in range(ns)] for n, (s0, g) in enumerate(units)]
    fs_a = [jnp.concatenate([f[0:seq_rows] for f in fs], axis=0) for fs in from_state]
    fs_r = [jnp.concatenate([f[seq_rows:2 * seq_rows] for f in fs], axis=0) for fs in from_state]
    low2 = [gm[0:rows] * m_strict for gm in gram]
    rhs_u = [fs_a[n] + _mm(low2[n][:, rw_pad:2 * rw_pad], jnp.concatenate([v4[n], *pad], axis=0))
             for n in range(len(units))]
    inv = _unit_lower_inverse([lw[:, 0:rows] for lw in low2], c)
    u = [_mm(inv[n], rhs_u[n]) for n in range(len(units))]
    o_unit = {}
    for n, (s0, g) in enumerate(units):
        uv = jnp.concatenate([u[n], *pad, v4[n], *pad], axis=0)
        o4 = fs_r[n] + _mm(gram[n][rows:2 * rows] * m_incl, uv)
        for i in range(ns):
            o_j = o4[i * seq_rows:i * seq_rows + c]
            for hh in range(1, RW_GROUP):
                o_j = o_j + o4[i * seq_rows + hh * c:i * seq_rows + (hh + 1) * c]
            o_unit[(s0 + i, g)] = o_j
        b4 = stack(b_e, (s0, g))
        k4 = stack(k_e, (s0, g))
        for i in range(ns):
            ends = jnp.concatenate([of_seq(b4, i), of_seq(k4, i)], axis=0)
            uv_j = jnp.concatenate([of_seq(u[n], i), of_seq(v4[n], i)], axis=0)
            sbd_scr[s0 + i, g] = sbd_scr[s0 + i, g] * jnp.exp(cum_ends[s0 + i][:, lanes(g)]) + _mm_tn(uv_j, ends)

    @pl.when(ci == n_chunks - 1)
    def _fin():
        for j in seqs:
            for h in range(RW_HEADS):
                g, sl = head_block(h)
                s_ref[j, h] = sbd_scr[j, g, sl, sl]

    o = jnp.concatenate([jnp.concatenate([o_unit[(j, g)] for g in groups], axis=1) for j in seqs], axis=0)
    inv_n = 1.0 / RW_HEAD
    mean = _mm(o, bd) * inv_n
    dev = o - mean
    var = _mm(dev * dev, bd) * inv_n
    normed = dev * lax.rsqrt(var + RW_LN_EPS) * lng_ref[...] + lnb_ref[...]
    bonus = _mm(r * k1 * rk_ref[...], bd) * v
    out = (normed + bonus) * g_a
    for j in seqs:
        o_ref[j] = out[j * c:(j + 1) * c].astype(o_ref.dtype)


def _block_masks(n_blocks, c):
    r = np.arange(n_blocks * c)
    same = (r[:, None] // c) == (r[None, :] // c)
    causal = (same & (r[None, :] <= r[:, None])).astype(np.float32)
    strict = (same & (r[None, :] < r[:, None])).astype(np.float32)
    return causal, strict


def _rwkv_masks(c, nb):
    rows = nb * RW_GROUP * c
    rw_pad = -(-rows // LANES) * LANES
    lane_head = np.arange(RW_GW) // RW_HEAD
    hm = (lane_head[None, :] == np.arange(RW_GROUP)[:, None]).astype(np.float32)
    incl, strict = _block_masks(nb * RW_GROUP, c)
    widen = lambda m: np.tile(np.pad(m, ((0, 0), (0, rw_pad - rows))), (1, 2))
    return rw_pad, jnp.asarray(hm), jnp.asarray(widen(strict)), jnp.asarray(widen(incl))


def _state_outputs(tails, *, batch, nb, stacked):
    zeros = lambda t: (0,) * len(t)
    if stacked is None:
        shapes = [jax.ShapeDtypeStruct((batch,) + t, F32) for t in tails]
        specs = [pl.BlockSpec((nb,) + t, functools.partial(lambda b, ci, z: (b,) + z, z=zeros(t))) for t in tails]
        return shapes, specs, [], {}
    depth, layer, prev = stacked
    shapes = [jax.ShapeDtypeStruct((depth, batch) + t, F32) for t in tails]
    specs = [pl.BlockSpec((None, nb) + t, functools.partial(lambda b, ci, z: (layer, b) + z, z=zeros(t))) for t in tails]
    prev = [] if prev is None else list(prev)
    return shapes, specs, prev, {i: 1 + i for i in range(len(prev))}


ANY_SPEC = pl.BlockSpec(memory_space=pl.ANY)


def _token_specs(cols, *, c, nb, nc, row_block0):
    return [pl.BlockSpec((c, cols), functools.partial(lambda b, ci, j: (row_block0 + (b * nb + j) * nc + ci, 0), j=j))
            for j in range(nb)]


def _rwkv(pa, shift0, wkv0, layer, prm, *, batch, seq, c, nb, ns, row_block0, stacked):
    nc = seq // c
    assert batch % nb == 0 and nb % ns == 0
    rw_pad, hm, m_strict, m_incl = _rwkv_masks(c, ns)
    rows = ns * RW_GROUP * c
    tri = jnp.asarray(_block_masks(nb, c)[0]).astype(BF16)
    st_shapes, st_specs, prev, aliases = _state_outputs(
        [(RW_HEADS, RW_HEAD, RW_HEAD), (1, RW_COLS)], batch=batch, nb=nb, stacked=stacked)
    kern = functools.partial(_rwkv_kernel, c=c, nb=nb, ns=ns, n_chunks=nc, rw_pad=rw_pad, n_alias=len(prev))
    vec = lambda n: pl.BlockSpec((1, n), lambda b, ci: (0, 0))
    mat = lambda r_, n: pl.BlockSpec((r_, n), lambda b, ci: (0, 0))
    prm = tuple(prm) + (hm, m_strict, m_incl, tri)
    return pl.pallas_call(
        kern,
        out_shape=(jax.ShapeDtypeStruct((batch, seq, BR_W), BF16), *st_shapes),
        grid=(batch // nb, nc),
        input_output_aliases=aliases,
        in_specs=[*([ANY_SPEC] * len(prev)), *_token_specs(RW_COLS, c=c, nb=nb, nc=nc, row_block0=row_block0),
                  pl.BlockSpec((None, nb, 1, RW_COLS), lambda b, ci: (layer, b, 0, 0)),
                  pl.BlockSpec((None, nb, RW_HEADS, RW_HEAD, RW_HEAD), lambda b, ci: (layer, b, 0, 0, 0)),
                  vec(RW_COLS), vec(512), mat(128, 512), vec(512), mat(128, 512), mat(128, 512),
                  vec(512), vec(512), vec(512), vec(512), vec(512), mat(512, 512),
                  mat(RW_GROUP, RW_GW), mat(rows, 2 * rw_pad), mat(rows, 2 * rw_pad), mat(nb * c, nb * c)],
        out_specs=(pl.BlockSpec((nb, c, BR_W), lambda b, ci: (b, ci, 0)), *st_specs),
        scratch_shapes=[pltpu.VMEM((nb, SUBLANES + c, RW_COLS), F32),
                        pltpu.VMEM((nb, RW_HEADS // RW_GROUP, RW_GW, RW_GW), F32)],
        compiler_params=_cparams(2),
        name="rwkv7",
    )(*prev, *([pa] * nb), shift0, wkv0, *prm)


def _ret_kernel(*refs, c, nb, n_alias):
    refs = refs[n_alias:]
    tok_refs = refs[:nb]
    cos_ref, sin_ref, mask_ref, s0_ref, o_ref, s_ref = refs[nb:]
    ci = pl.program_id(1)

    @pl.when(ci == 0)
    def _init():
        s_ref[...] = s0_ref[...]

    rows = nb * c
    width = RET_HEADS * RET_DK
    half = RET_DK // 2
    cos = jnp.concatenate([cos_ref[...]] * nb, axis=0)
    sin = jnp.concatenate([sin_ref[...]] * nb, axis=0)
    first_half = (_iota((rows, width), 1) & half) == 0

    def rot(x):
        partner = jnp.where(first_half, pltpu.roll(x, width - half, 1), pltpu.roll(x, half, 1))
        return x * cos + partner * sin

    pb = jnp.concatenate([t[...] for t in tok_refs], axis=0)
    q = rot(pb[:, 0:width])
    k = rot(pb[:, width:2 * width]) * (RET_DK ** -0.5)
    v = pb[:, 2 * width:2 * width + BR_W]
    g_b = pb[:, 2 * width + BR_W:2 * width + 2 * BR_W]

    mask = mask_ref[...]
    causal = mask > 0.5
    dist = jnp.where(causal, (_iota((rows, rows), 0) - _iota((rows, rows), 1)).astype(F32), 0.0)
    t1 = jnp.sum(mask, axis=-1, keepdims=True)
    for h in range(RET_HEADS):
        log_gamma = math.log1p(-(2.0 ** (-5.0 - h)))
        dmat = jnp.where(causal, jnp.exp(dist * log_gamma), 0.0)
        q_h = q[:, h * RET_DK:(h + 1) * RET_DK]
        k_h = k[:, h * RET_DK:(h + 1) * RET_DK]
        v_h = v[:, h * RET_DV:(h + 1) * RET_DV]
        k_dec = k_h * jnp.exp((c - t1) * log_gamma)
        from_state = []
        for j in range(nb):
            js = slice(j * c, (j + 1) * c)
            s_h = s_ref[j, h]
            from_state.append(_mm(q_h[js], s_h))
            s_ref[j, h] = math.exp(c * log_gamma) * s_h + _mm_tn(k_dec[js], v_h[js])
        o_h = _mm(_mm_nt(q_h, k_h) * dmat, v_h) + jnp.exp(t1 * log_gamma) * jnp.concatenate(from_state, axis=0)
        o_n = o_h * lax.rsqrt(jnp.mean(o_h * o_h, axis=-1, keepdims=True) + NORM_EPS)
        o_h = o_n * _silu(g_b[:, h * RET_DV:(h + 1) * RET_DV])
        for j in range(nb):
            o_ref[j, :, h * RET_DV:(h + 1) * RET_DV] = o_h[j * c:(j + 1) * c].astype(o_ref.dtype)


def _ret(pb, cos, sin, ret0, layer, *, batch, seq, c, nb, row_block0, stacked):
    nc = seq // c
    assert batch % nb == 0
    width = RET_HEADS * RET_DK
    mask = jnp.asarray(_block_masks(nb, c)[0])
    st_shapes, st_specs, prev, aliases = _state_outputs(
        [(RET_HEADS, RET_DK, RET_DV)], batch=batch, nb=nb, stacked=stacked)
    return pl.pallas_call(
        functools.partial(_ret_kernel, c=c, nb=nb, n_alias=len(prev)),
        out_shape=(jax.ShapeDtypeStruct((batch, seq, BR_W), BF16), *st_shapes),
        grid=(batch // nb, nc),
        input_output_aliases=aliases,
        in_specs=[*([ANY_SPEC] * len(prev)), *_token_specs(RET_COLS, c=c, nb=nb, nc=nc, row_block0=row_block0),
                  pl.BlockSpec((c, width), lambda b, ci: (ci, 0)),
                  pl.BlockSpec((c, width), lambda b, ci: (ci, 0)),
                  pl.BlockSpec((nb * c, nb * c), lambda b, ci: (0, 0)),
                  pl.BlockSpec((None, nb, RET_HEADS, RET_DK, RET_DV), lambda b, ci: (layer, b, 0, 0, 0))],
        out_specs=(pl.BlockSpec((nb, c, BR_W), lambda b, ci: (b, ci, 0)), *st_specs),
        compiler_params=_cparams(2),
        name="retention",
    )(*prev, *([pb] * nb), cos, sin, mask, ret0)


def _hgrn_kernel(*refs, c, m, nb, layer, n_chunks, n_alias):
    refs = refs[n_alias:]
    tok_refs = refs[:nb]
    lbp_ref, ng_ref, tri_ref, s0_ref, o_ref, s_ref, st_scr = refs[nb:]
    ci = pl.program_id(1)

    @pl.when(ci == 0)
    def _init():
        for j in range(nb):
            for h in range(HG_HEADS):
                st_scr[j, h] = s0_ref[j, h].T

    lbp = lbp_ref[...]
    e = jnp.exp(lbp - jnp.max(lbp, axis=0, keepdims=True))
    soft = e / jnp.sum(e, axis=0, keepdims=True)
    lb = jnp.zeros((1, BR_W), F32)
    for j in range(1, layer + 1):
        lb = lb + soft[j:j + 1, :]

    pc = jnp.concatenate([t[...] for t in tok_refs], axis=0)
    q_all = _silu(pc[:, 0:BR_W])
    f_in = pc[:, BR_W:2 * BR_W]
    v_all = pc[:, 2 * BR_W:3 * BR_W]
    g_c = pc[:, 3 * BR_W:4 * BR_W]
    log_f = jnp.log(lb + (1.0 - lb) * _sigmoid(f_in))
    k_all = (1.0 - lb) * _sigmoid(-f_in)
    cum_all = _mm_mask(tri_ref[...], log_f)

    col_m = _iota((m, m), 1)
    row_l = _iota((m, LANES), 0)
    outs = {}
    for j in range(nb):
        js = slice(j * c, (j + 1) * c)
        q, k, v, cum = q_all[js], k_all[js], v_all[js], cum_all[js]
        cum_last = cum[c - 1:c, :]
        q_e = q * jnp.exp(cum)
        k_e = k * jnp.exp(cum_last - cum)
        p_end = jnp.exp(cum_last)
        for h in range(HG_HEADS):
            sl = slice(h * HG_D, (h + 1) * HG_D)
            st = st_scr[j, h]
            o_rows = []
            for i in range(c // m):
                r0 = i * m
                q_i = q[r0:r0 + m, sl]
                c_i = cum[r0:r0 + m, sl]
                k_i = k[r0:r0 + m, sl]
                diag = jnp.zeros((m, m), F32)
                for s in range(m):
                    keep = row_l >= s
                    decay = jnp.exp(jnp.where(keep, c_i - c_i[s:s + 1, :], 0.0))
                    col = jnp.sum(jnp.where(keep, q_i * k_i[s:s + 1, :] * decay, 0.0), axis=-1, keepdims=True)
                    diag = jnp.where(col_m == s, col, diag)
                o_i = _mm(diag, v[r0:r0 + m, sl])
                if i > 0:
                    c_ref = cum[r0 - 1:r0, sl]
                    q_s = q_i * jnp.exp(c_i - c_ref)
                    k_s = k[0:r0, sl] * jnp.exp(c_ref - cum[0:r0, sl])
                    o_i = o_i + _mm(_mm_nt(q_s, k_s), v[0:r0, sl])
                o_rows.append(o_i)
            outs[(j, h)] = _mm_nt(q_e[:, sl], st) + jnp.concatenate(o_rows, axis=0)
            st_scr[j, h] = st * p_end[:, sl] + _mm_tn(v[:, sl], k_e[:, sl])
    for h in range(HG_HEADS):
        sl = slice(h * HG_D, (h + 1) * HG_D)
        o_h = jnp.concatenate([outs[(j, h)] for j in range(nb)], axis=0)
        o_n = o_h * lax.rsqrt(jnp.mean(o_h * o_h, axis=-1, keepdims=True) + NORM_EPS) * ng_ref[:, sl]
        o_h = o_n * _silu(g_c[:, sl])
        for j in range(nb):
            o_ref[j, :, sl] = o_h[j * c:(j + 1) * c].astype(o_ref.dtype)

    @pl.when(ci == n_chunks - 1)
    def _fin():
        for j in range(nb):
            for h in range(HG_HEADS):
                s_ref[j, h] = st_scr[j, h].T


def _hgrn(pc, lbp, ng, hg0, state_layer, layer, *, batch, seq, c, nb, row_block0, stacked):
    nc = seq // c
    assert batch % nb == 0
    m = 16 if c % 16 == 0 else SUBLANES
    depth = lbp.shape[0]
    tri = jnp.asarray(_block_masks(nb, c)[0]).astype(BF16)
    st_shapes, st_specs, prev, aliases = _state_outputs([(HG_HEADS, HG_D, HG_D)], batch=batch, nb=nb, stacked=stacked)
    return pl.pallas_call(
        functools.partial(_hgrn_kernel, c=c, m=m, nb=nb, layer=layer, n_chunks=nc, n_alias=len(prev)),
        out_shape=(jax.ShapeDtypeStruct((batch, seq, BR_W), BF16), *st_shapes),
        grid=(batch // nb, nc),
        input_output_aliases=aliases,
        in_specs=[*([ANY_SPEC] * len(prev)), *_token_specs(HG_COLS, c=c, nb=nb, nc=nc, row_block0=row_block0),
                  pl.BlockSpec((depth, BR_W), lambda b, ci: (0, 0)),
                  pl.BlockSpec((1, BR_W), lambda b, ci: (0, 0)),
                  pl.BlockSpec((nb * c, nb * c), lambda b, ci: (0, 0)),
                  pl.BlockSpec((None, nb, HG_HEADS, HG_D, HG_D), lambda b, ci: (state_layer, b, 0, 0, 0))],
        out_specs=(pl.BlockSpec((nb, c, BR_W), lambda b, ci: (b, ci, 0)), *st_specs),
        scratch_shapes=[pltpu.VMEM((nb, HG_HEADS, HG_D, HG_D), F32)],
        compiler_params=_cparams(2),
        name="hgrn2",
    )(*prev, *([pc] * nb), lbp, ng, tri, hg0)


def _gdn_kernel(*refs, c, nb, ns, n_alias):
    refs = refs[n_alias:]
    tok_refs = refs[:nb]
    (cs0_ref, s0_ref, cw_ref, alog_ref, dtb_ref, ng_ref, mc_ref, mst_ref, tri_ref,
     o_ref, s_ref, cs_ref, ext_scr) = refs[nb:]
    ci = pl.program_id(1)
    keep = GD_CONV - 1

    @pl.when(ci == 0)
    def _init():
        for j in range(nb):
            ext_scr[j, SUBLANES - keep:SUBLANES, :] = cs0_ref[j]
        s_ref[...] = s0_ref[...]

    convs = []
    for j in range(nb):
        x = tok_refs[j][:, 0:GD_QKV]
        ext_scr[j, SUBLANES:SUBLANES + c, :] = x
        conv = ext_scr[j, SUBLANES - keep:SUBLANES - keep + c, :] * cw_ref[0:1, :]
        for i in range(1, keep):
            conv = conv + ext_scr[j, SUBLANES - keep + i:SUBLANES - keep + i + c, :] * cw_ref[i:i + 1, :]
        convs.append(conv + x * cw_ref[keep:keep + 1, :])
        tail = ext_scr[j, SUBLANES + c - keep:SUBLANES + c, :]
        ext_scr[j, SUBLANES - keep:SUBLANES, :] = tail
        cs_ref[j] = tail

    act = _silu(jnp.concatenate(convs, axis=0))
    q = act[:, 0:BR_W]
    k = act[:, BR_W:2 * BR_W]
    v = act[:, 2 * BR_W:3 * BR_W]
    gate_ba = jnp.concatenate([t[:, GD_QKV:GD_COLS_PAD] for t in tok_refs], axis=0)
    g_d = gate_ba[:, 0:BR_W]
    ba = gate_ba[:, BR_W:BR_W + LANES]
    beta_all = _sigmoid(ba)
    g_all = -jnp.exp(alog_ref[...]) * _softplus(ba + dtb_ref[...])

    cum_all = _mm_mask(tri_ref[...], g_all)

    ur = ns * c
    rows = GD_HEADS * ur
    stack = lambda parts: jnp.concatenate(parts, axis=0)
    l2n = lambda z: z * lax.rsqrt(jnp.sum(z * z, axis=-1, keepdims=True) + 1e-6)
    head = lambda z, h: z[:, h * GD_D:(h + 1) * GD_D]
    qn = jnp.concatenate([l2n(head(q, h)) for h in range(GD_HEADS)], axis=1) * (GD_D ** -0.5)
    kn = jnp.concatenate([l2n(head(k, h)) for h in range(GD_HEADS)], axis=1)
    units = range(nb // ns)
    heads = range(GD_HEADS)
    stack_u = lambda z, un: stack([head(z[un * ur:(un + 1) * ur], h) for h in heads])
    col_u = lambda z, lane0, un: stack([z[un * ur:(un + 1) * ur, lane0 + h:lane0 + h + 1] for h in heads])
    q4 = [stack_u(qn, un) for un in units]
    k4 = [stack_u(kn, un) for un in units]
    v4 = [stack_u(v, un) for un in units]
    beta = [col_u(beta_all, 0, un) for un in units]
    cum = [col_u(cum_all, GD_HEADS, un) for un in units]
    causal = mc_ref[...] > 0.5
    m_strict = mst_ref[...]
    eye = _iota((rows, rows), 0) == _iota((rows, rows), 1)
    ones = jnp.ones((rows, rows), BF16)
    cum_t = [jnp.broadcast_to(cm, (rows, rows)) for cm in cum]
    cum_s = [_mm_mask(ones, jnp.where(eye, ct, 0.0)) for ct in cum_t]
    dmat = [jnp.where(causal, jnp.exp(jnp.where(causal, ct - cs_, 0.0)), 0.0) for ct, cs_ in zip(cum_t, cum_s)]
    a_mat = [beta[un] * _mm_nt(k4[un], k4[un]) * (dmat[un] * m_strict) for un in units]
    inv = _unit_lower_inverse([-a for a in a_mat], c)
    e_cum = [jnp.exp(cm) for cm in cum]
    sol = [_mm(inv[un], jnp.concatenate([(beta[un] * e_cum[un]) * k4[un], beta[un] * v4[un]], axis=1)) for un in units]
    blocks = [(h, i) for h in heads for i in range(ns)]
    rows_of = lambda h, i: slice((h * ns + i) * c, (h * ns + i + 1) * c)
    qw = [[_mm(jnp.concatenate([q4[un][rows_of(h, i)], sol[un][rows_of(h, i), 0:GD_D]], axis=0), s_ref[un * ns + i, h])
           for h, i in blocks] for un in units]
    delta = [sol[un][:, GD_D:2 * GD_D] - stack([z[c:2 * c] for z in qw[un]]) for un in units]
    o4 = [e_cum[un] * stack([z[0:c] for z in qw[un]]) + _mm(_mm_nt(q4[un], k4[un]) * dmat[un], delta[un])
          for un in units]
    for un in units:
        for h, i in blocks:
            hs = rows_of(h, i)
            cum_h = cum[un][hs]
            cum_last = cum_h[c - 1:c, :]
            s_ref[un * ns + i, h] = (jnp.exp(cum_last) * s_ref[un * ns + i, h]
                                     + _mm_tn(k4[un][hs] * jnp.exp(cum_last - cum_h), delta[un][hs]))
    for h in heads:
        sl = slice(h * GD_D, (h + 1) * GD_D)
        o_h = stack([o4[un][h * ur:(h + 1) * ur] for un in units])
        o_n = o_h * lax.rsqrt(jnp.mean(o_h * o_h, axis=-1, keepdims=True) + NORM_EPS) * ng_ref[:, sl]
        o_h = o_n * _silu(g_d[:, sl])
        for j in range(nb):
            o_ref[j, :, sl] = o_h[j * c:(j + 1) * c].astype(o_ref.dtype)


def _gdn(pd, conv0, gd0, layer, cw, alog, dtb, ng, *, batch, seq, c, nb, ns, row_block0, stacked):
    nc = seq // c
    assert batch % nb == 0 and nb % ns == 0
    keep = GD_CONV - 1
    rows = GD_HEADS * ns * c
    m_causal, m_strict = (jnp.asarray(m) for m in _block_masks(GD_HEADS * ns, c))
    tri = jnp.asarray(_block_masks(nb, c)[0]).astype(BF16)
    full = lambda *shape: pl.BlockSpec(shape, lambda b, ci: (0,) * len(shape))
    st_shapes, st_specs, prev, aliases = _state_outputs(
        [(GD_HEADS, GD_D, GD_D), (keep, GD_QKV)], batch=batch, nb=nb, stacked=stacked)
    return pl.pallas_call(
        functools.partial(_gdn_kernel, c=c, nb=nb, ns=ns, n_alias=len(prev)),
        out_shape=(jax.ShapeDtypeStruct((batch, seq, BR_W), BF16), *st_shapes),
        grid=(batch // nb, nc),
        input_output_aliases=aliases,
        in_specs=[*([ANY_SPEC] * len(prev)), *_token_specs(GD_COLS_PAD, c=c, nb=nb, nc=nc, row_block0=row_block0),
                  pl.BlockSpec((None, nb, keep, GD_QKV), lambda b, ci: (layer, b, 0, 0)),
                  pl.BlockSpec((None, nb, GD_HEADS, GD_D, GD_D), lambda b, ci: (layer, b, 0, 0, 0)),
                  full(GD_CONV, GD_QKV), full(1, LANES), full(1, LANES), full(1, BR_W),
                  full(rows, rows), full(rows, rows), full(nb * c, nb * c)],
        out_specs=(pl.BlockSpec((nb, c, BR_W), lambda b, ci: (b, ci, 0)), *st_specs),
        scratch_shapes=[pltpu.VMEM((nb, SUBLANES + c, GD_QKV), F32)],
        compiler_params=_cparams(2),
        name="gdn",
    )(*prev, *([pd] * nb), conv0, gd0, cw, alog, dtb, ng, m_causal, m_strict, tri)


def _pick_chunk(seq):
    for c in (64, 56, 48, 40, 32, 24, 16, 8):
        if seq % c == 0:
            return c
    raise ValueError(f"sequence length {seq} is not a multiple of {SUBLANES}")


def _pick_tile(n, candidates):
    for t in candidates:
        if n % t == 0:
            return t
    raise ValueError(f"no tile in {candidates} divides {n}")


def _rope_tables(pos):
    half = RET_DK // 2
    inv = ROPE_BASE ** (-jnp.arange(half, dtype=F32) / half)
    ang = pos.astype(F32)[:, None] * inv[None, :]
    cos, sin = jnp.cos(ang), jnp.sin(ang)
    cos_t = jnp.tile(jnp.concatenate([cos, cos], axis=1), (1, RET_HEADS))
    sin_t = jnp.tile(jnp.concatenate([-sin, sin], axis=1), (1, RET_HEADS))
    return cos_t, sin_t


def kernel(x_prompt, x_sample, state_rwkv_wkv, state_rwkv_shift, state_ret, state_hgrn, state_gdn, state_gdn_conv, meta_tokens, norm_mix, w_in, rw_mu, rw_w0, rw_w2, rw_a0, rw_a2, rw_g2, rw_kk, rw_ka, rw_rk, rw_ln_g, rw_ln_b, hg_lb, hg_norm_g, gd_conv, gd_a_log, gd_dt_bias, gd_norm_g, w_branch, w_out, norm_ffn, w_up, w_down, norm_final):
    depth = norm_mix.shape[0]
    bp, tq, d = x_prompt.shape
    tp = tq + N_META
    bs, ts, _ = x_sample.shape
    n_p, n_s = bp * tp, bs * ts
    n_tok = n_p + n_s
    cp, cs = _pick_chunk(tp), _pick_chunk(ts)
    assert cs == ts and n_p % ts == 0
    tm = _pick_tile(n_tok, (464, 512, 256, 232, 128, 64, 32, 16, 8))
    tm_in = _pick_tile(n_tok, (928, 464, 512, 256, 232, 128, 64, 32, 16, 8))
    nb_s = _pick_tile(bs, (8, 4, 2, 1))
    nb_p = _pick_tile(bp, (4, 2, 1))

    meta = jnp.broadcast_to(meta_tokens.astype(F32)[None], (bp, N_META, d))
    x = jnp.concatenate([jnp.concatenate([meta, x_prompt], axis=1).reshape(n_p, d), x_sample.reshape(n_s, d)], axis=0)

    cos_p, sin_p = _rope_tables(jnp.arange(tp, dtype=jnp.int32))
    cos_s, sin_s = _rope_tables(PAST_LEN + jnp.arange(ts, dtype=jnp.int32))

    head_of = jnp.arange(BR_W) // RW_HEAD
    bd = (head_of[:, None] == head_of[None, :]).astype(BF16)

    zeros = lambda *s: jnp.zeros((1, bp) + s, F32)
    z_wkv, z_shift = zeros(RW_HEADS, RW_HEAD, RW_HEAD), zeros(1, RW_COLS)
    z_ret, z_hg, z_gd = zeros(RET_HEADS, RET_DK, RET_DV), zeros(HG_HEADS, HG_D, HG_D), zeros(GD_HEADS, GD_D, GD_D)
    z_conv = zeros(GD_CONV - 1, GD_QKV)
    shift_s = state_rwkv_shift.reshape(depth, bs, 1, RW_COLS)

    o_a, o_b, o_c = RW_COLS, RW_COLS + RET_COLS, RW_COLS + RET_COLS + HG_COLS
    o_g = o_c + GD_QKV + 2 * GD_HEADS + BR_W
    new_p = [[] for _ in range(6)]
    s_stacks = [None] * 6
    for l in range(depth):
        wl = w_in[l]
        w_a = wl[:, 0:o_a].astype(BF16)
        w_b = wl[:, o_a:o_b].astype(BF16)
        w_c = wl[:, o_b:o_c].astype(BF16)
        w_d = jnp.concatenate([wl[:, o_c:o_c + GD_QKV], wl[:, o_c + GD_QKV + 2 * GD_HEADS:o_g],
                               wl[:, o_c + GD_QKV:o_c + GD_QKV + 2 * GD_HEADS],
                               jnp.zeros((d, LANES - 2 * GD_HEADS), F32)], axis=1).astype(BF16)
        w_g = wl[:, o_g:].astype(BF16)

        h = _rmsnorm(x, norm_mix[l], tm, BF16)
        pa = _matmul(h, w_a, tm_in, 896, "in_proj_a")
        pb = _matmul(h, w_b, tm_in, 768, "in_proj_b")
        pc = _matmul(h, w_c, tm_in, 1024, "in_proj_c")
        pd = _matmul(h, w_d, tm_in, GD_COLS_PAD, "in_proj_d")
        gates = _matmul(h, w_g, tm_in, 1024, "in_proj_g", gate=True)

        row1 = lambda a: a.reshape(1, -1)
        pad_rows = lambda a, top: jnp.concatenate(
            [jnp.zeros((top, BR_W), F32), a, jnp.zeros((LANES - top - a.shape[0], BR_W), F32)], axis=0).astype(BF16)
        rw_prm = (row1(rw_mu[l]), row1(rw_w0[l]), pad_rows(rw_w2[l], 0), row1(rw_a0[l]), pad_rows(rw_a2[l], 64),
                  rw_g2[l].astype(BF16), row1(rw_kk[l]), row1(rw_ka[l]), row1(rw_rk[l]), row1(rw_ln_g[l]),
                  row1(rw_ln_b[l]), bd)
        lane_pad = lambda a: jnp.concatenate(
            [jnp.zeros((GD_HEADS,), F32), a, jnp.zeros((LANES - 2 * GD_HEADS,), F32)]).reshape(1, LANES)
        gd_prm = (gd_conv[l], lane_pad(gd_a_log[l]), lane_pad(gd_dt_bias[l]), row1(gd_norm_g[l]))
        hg_g = row1(hg_norm_g[l])

        outs = []
        for grp, (batch, seq, c, nb, ns, rb0, sts, lay, cos, sin) in enumerate((
                (bp, tp, cp, nb_p, 1, 0, (z_wkv, z_shift, z_ret, z_hg, z_gd, z_conv), 0, cos_p, sin_p),
                (bs, ts, cs, nb_s, nb_s, n_p // ts,
                 (state_rwkv_wkv, shift_s, state_ret, state_hgrn, state_gdn, state_gdn_conv), l, cos_s, sin_s))):
            kw = dict(batch=batch, seq=seq, c=c, nb=nb, row_block0=rb0)
            stk = lambda *prev: None if grp == 0 else (depth, l, prev if l > 0 else None)
            oa, wkv, shift = _rwkv(pa, sts[1], sts[0], lay, rw_prm, ns=ns, stacked=stk(*s_stacks[0:2]), **kw)
            ob, ret = _ret(pb, cos, sin, sts[2], lay, stacked=stk(*s_stacks[2:3]), **kw)
            oc, hg = _hgrn(pc, hg_lb, hg_g, sts[3], lay, l, stacked=stk(*s_stacks[3:4]), **kw)
            od, gd, conv = _gdn(pd, sts[5], sts[4], lay, *gd_prm, ns=ns, stacked=stk(*s_stacks[4:6]), **kw)
            outs.append(tuple(o.reshape(batch * seq, BR_W) for o in (oa, ob, oc, od)))
            if grp == 0:
                for i, st in enumerate((wkv, shift.reshape(batch, RW_COLS), ret, hg, gd, conv)):
                    new_p[i].append(st)
            else:
                s_stacks = [wkv, shift, ret, hg, gd, conv]
        branch = [jnp.concatenate([outs[0][n], outs[1][n]], axis=0) for n in range(4)]
        wb = w_branch[l].astype(BF16)
        x = _merge(x, branch, gates, wb, w_out[l].astype(BF16), tm_in, 256)
        x = _ffn(x, norm_ffn[l], w_up[l].astype(BF16), w_down[l].astype(BF16), tm, 512)

    y = _rmsnorm(x, norm_final, tm, F32)
    y_prompt = y[:n_p].reshape(bp, tp, d)[:, N_META:]
    y_sample = y[n_p:].reshape(bs, ts, d)
    p_states = tuple(jnp.stack(z) for z in new_p)
    s_wkv, s_shift, s_ret, s_hg, s_gd, s_conv = s_stacks
    return (y_prompt, y_sample) + p_states + (s_wkv, s_shift.reshape(depth, bs, RW_COLS), s_ret, s_hg, s_gd, s_conv)
```

```python
import functools
import math

import jax
import jax.numpy as jnp
import numpy as np
from jax import lax
from jax.experimental import pallas as pl
from jax.experimental.pallas import tpu as pltpu

F32 = jnp.float32
BF16 = jnp.bfloat16

N_META = 16
PAST_LEN = 16384
NORM_EPS = 1e-6
RW_LN_EPS = 64e-5
ROPE_BASE = 10000.0

D_MODEL = 2048
BR_W = 512
RW_HEADS, RW_HEAD = 8, 64
RW_COLS = 1792
RW_GROUP = 4
RW_GW = RW_GROUP * RW_HEAD
RET_HEADS, RET_DK, RET_DV = 4, 64, 128
RET_COLS = 1536
HG_HEADS, HG_D = 4, 128
HG_COLS = 2048
GD_HEADS, GD_D = 4, 128
GD_CONV = 4
GD_QKV = 1536
GD_COLS_PAD = 2176

LANES = 128
SUBLANES = 8
VMEM_LIMIT = 56 * 1024 * 1024


def _cparams(n_axes):
    return pltpu.CompilerParams(dimension_semantics=("arbitrary",) * n_axes, vmem_limit_bytes=VMEM_LIMIT)


def _dg(a, b, ca, cb):
    return lax.dot_general(a, b, (((ca,), (cb,)), ((), ())), preferred_element_type=F32)


def _mm(a, b):
    return _dg(a.astype(BF16), b.astype(BF16), 1, 0)


def _mm_nt(a, b):
    return _dg(a.astype(BF16), b.astype(BF16), 1, 1)


def _mm_tn(a, b):
    return _dg(a.astype(BF16), b.astype(BF16), 0, 0)


def _split(x, n):
    parts, r = [], x
    for _ in range(n):
        p = r.astype(BF16)
        parts.append(p)
        r = r - p.astype(F32)
    return parts


def _mm_mask(m01, x):
    x0, x1, x2 = _split(x, 3)
    return _dg(m01, x0, 1, 0) + (_dg(m01, x1, 1, 0) + _dg(m01, x2, 1, 0))


def _iota(shape, dim):
    return lax.broadcasted_iota(jnp.int32, shape, dim)


def _softplus(x):
    return jnp.maximum(x, 0.0) + jnp.log(1.0 + jnp.exp(-jnp.abs(x)))


def _sigmoid(x):
    return jax.nn.sigmoid(x)


def _silu(x):
    return x * jax.nn.sigmoid(x)


def _unit_lower_inverse(lows, c):
    shape = lows[0].shape
    eye = (_iota(shape, 0) == _iota(shape, 1)).astype(F32)
    invs = [eye + low for low in lows]
    powers = list(lows)
    covered = 1
    while covered < c - 1:
        powers = [_mm(p, p) for p in powers]
        invs = [inv + _mm(inv, p) for inv, p in zip(invs, powers)]
        covered = 2 * covered + 1
    return invs


def _rmsnorm_kernel(x_ref, g_ref, o_ref):
    x = x_ref[...]
    y = x * lax.rsqrt(jnp.mean(x * x, axis=-1, keepdims=True) + NORM_EPS)
    o_ref[...] = (y * g_ref[...]).astype(o_ref.dtype)


def _rmsnorm(x, g, tm, out_dtype):
    n, d = x.shape
    return pl.pallas_call(
        _rmsnorm_kernel,
        out_shape=jax.ShapeDtypeStruct((n, d), out_dtype),
        grid=(n // tm,),
        in_specs=[pl.BlockSpec((tm, d), lambda i: (i, 0)), pl.BlockSpec((1, d), lambda i: (0, 0))],
        out_specs=pl.BlockSpec((tm, d), lambda i: (i, 0)),
        compiler_params=_cparams(1),
        name="rmsnorm",
    )(x, g.reshape(1, d))


def _matmul_kernel(x_ref, w_ref, o_ref):
    o_ref[...] = _dg(x_ref[...], w_ref[...], 1, 0)


def _matmul_sigmoid_kernel(x_ref, w_ref, o_ref):
    o_ref[...] = _sigmoid(_dg(x_ref[...], w_ref[...], 1, 0)).astype(o_ref.dtype)


def _matmul(x, w, tm, tn, name, gate=False):
    n, k = x.shape
    m = w.shape[1]
    return pl.pallas_call(
        _matmul_sigmoid_kernel if gate else _matmul_kernel,
        out_shape=jax.ShapeDtypeStruct((n, m), BF16 if gate else F32),
        grid=(n // tm, m // tn),
        in_specs=[pl.BlockSpec((tm, k), lambda i, j: (i, 0)), pl.BlockSpec((k, tn), lambda i, j: (0, j))],
        out_specs=pl.BlockSpec((tm, tn), lambda i, j: (i, j)),
        compiler_params=_cparams(2),
        name=name,
    )(x, w)


def _merge_kernel(x_ref, oa_ref, ob_ref, oc_ref, od_ref, g0_ref, g1_ref, g2_ref, g3_ref, wb_ref, wo_ref, y_ref):
    j = pl.program_id(1)

    @pl.when(j == 0)
    def _init():
        y_ref[...] = x_ref[...]

    merged = None
    for n, (o_ref, g_ref) in enumerate(((oa_ref, g0_ref), (ob_ref, g1_ref), (oc_ref, g2_ref), (od_ref, g3_ref))):
        term = g_ref[...].astype(F32) * _dg(o_ref[...], wb_ref[n], 1, 0)
        merged = term if merged is None else merged + term
    y_ref[...] += _dg(merged.astype(BF16), wo_ref[...], 1, 0)


def _merge(x, outs, pg, wb, wo, tm, tj):
    n, d = x.shape
    nj = d // tj
    o_spec = pl.BlockSpec((tm, BR_W), lambda i, j: (i, 0))
    g_specs = [pl.BlockSpec((tm, tj), functools.partial(lambda i, j, nb: (i, nb * nj + j), nb=nb)) for nb in range(4)]
    return pl.pallas_call(
        _merge_kernel,
        out_shape=jax.ShapeDtypeStruct((n, d), F32),
        grid=(n // tm, nj),
        in_specs=[pl.BlockSpec((tm, d), lambda i, j: (i, 0)), o_spec, o_spec, o_spec, o_spec, *g_specs,
                  pl.BlockSpec((4, BR_W, tj), lambda i, j: (0, 0, j)),
                  pl.BlockSpec((tj, d), lambda i, j: (j, 0))],
        out_specs=pl.BlockSpec((tm, d), lambda i, j: (i, 0)),
        compiler_params=_cparams(2),
        name="merge",
    )(x, *outs, pg, pg, pg, pg, wb, wo)


def _rms(x, g):
    return x * lax.rsqrt(jnp.mean(x * x, axis=-1, keepdims=True) + NORM_EPS) * g


def _ffn_kernel(x_ref, g_ref, gn_ref, wu_ref, wg_ref, wd_ref, *out_and_scratch, last):
    *outs, h_ref, acc_ref = out_and_scratch
    j = pl.program_id(1)

    @pl.when(j == 0)
    def _init():
        h_ref[...] = _rms(x_ref[...], g_ref[...]).astype(BF16)
        acc_ref[...] = jnp.zeros_like(acc_ref)

    h = h_ref[...]
    up = _dg(h, wu_ref[...], 1, 0)
    gate = _dg(h, wg_ref[...], 1, 0)
    acc_ref[...] += _dg((_silu(gate) * up).astype(BF16), wd_ref[...], 1, 0)

    @pl.when(j == pl.num_programs(1) - 1)
    def _fin():
        y = x_ref[...] + acc_ref[...]
        if last:
            outs[0][...] = _rms(y, gn_ref[...])
        else:
            outs[0][...] = y
            outs[1][...] = _rms(y, gn_ref[...]).astype(BF16)


def _ffn(x, g, g_next, w_up, w_down, tm, tf, last):
    n, d = x.shape
    ff = w_down.shape[0]
    nf = ff // tf
    row_spec = pl.BlockSpec((tm, d), lambda i, j: (i, 0))
    vec_spec = pl.BlockSpec((1, d), lambda i, j: (0, 0))
    if last:
        out_shape, out_specs = jax.ShapeDtypeStruct((n, d), F32), row_spec
    else:
        out_shape = (jax.ShapeDtypeStruct((n, d), F32), jax.ShapeDtypeStruct((n, d), BF16))
        out_specs = (row_spec, row_spec)
    return pl.pallas_call(
        functools.partial(_ffn_kernel, last=last),
        out_shape=out_shape,
        grid=(n // tm, nf),
        in_specs=[row_spec, vec_spec, vec_spec,
                  pl.BlockSpec((d, tf), lambda i, j: (0, j)),
                  pl.BlockSpec((d, tf), lambda i, j: (0, nf + j)),
                  pl.BlockSpec((tf, d), lambda i, j: (j, 0))],
        out_specs=out_specs,
        scratch_shapes=[pltpu.VMEM((tm, d), BF16), pltpu.VMEM((tm, d), F32)],
        compiler_params=_cparams(2),
        name="ffn",
    )(x, g.reshape(1, d), g_next.reshape(1, d), w_up, w_up, w_down)


def _rwkv_kernel(*refs, c, nb, ns, n_chunks, rw_pad, n_alias):
    refs = refs[n_alias:]
    tok_refs = refs[:nb]
    (sh0_ref, s0_ref, mu_ref, w0_ref, w2_ref, a0_ref, a2_ref, g2_ref, kkw_ref, ka_ref, rk_ref, lng_ref, lnb_ref,
     bd_ref, hm_ref, ms_ref, mi_ref, tri_ref, o_ref, s_ref, sh_ref, xs_scr, sbd_scr) = refs[nb:]
    ci = pl.program_id(1)
    seqs = range(nb)
    head_block = lambda h: (h // RW_GROUP, slice((h % RW_GROUP) * RW_HEAD, (h % RW_GROUP + 1) * RW_HEAD))

    @pl.when(ci == 0)
    def _init():
        sbd_scr[...] = jnp.zeros_like(sbd_scr)
        for j in seqs:
            xs_scr[j, SUBLANES - 1:SUBLANES, :] = sh0_ref[j]
            for h in range(RW_HEADS):
                g, sl = head_block(h)
                sbd_scr[j, g, sl, sl] = s0_ref[j, h]

    pa = jnp.concatenate([t[...] for t in tok_refs], axis=0)
    prevs = []
    for j in seqs:
        pa_j = pa[j * c:(j + 1) * c, :]
        xs_scr[j, SUBLANES:SUBLANES + c, :] = pa_j
        prevs.append(xs_scr[j, SUBLANES - 1:SUBLANES - 1 + c, :])
        last = pa_j[c - 1:c, :]
        xs_scr[j, SUBLANES - 1:SUBLANES, :] = last
        sh_ref[j] = last
    prev = jnp.concatenate(prevs, axis=0)

    xm = pa + (prev - pa) * mu_ref[...]
    r = xm[:, 0:512]
    k0 = xm[:, 512:1024]
    v = xm[:, 1024:1536]
    wa = xm[:, 1536:1664]
    gl = xm[:, 1664:1792]
    w_log = -_softplus(-(w0_ref[...] + _mm(jnp.tanh(wa), w2_ref[...]))) - 0.5
    ld = -jnp.exp(w_log)
    asig = _sigmoid(a0_ref[...] + _mm(wa, a2_ref[...]))
    g_a = _mm(_sigmoid(gl), g2_ref[...])
    bd = bd_ref[...]
    kkr = k0 * kkw_ref[...]
    kkn = kkr * lax.rsqrt(_mm(kkr * kkr, bd) + 1e-6)
    k1 = k0 * (1.0 + (asig - 1.0) * ka_ref[...])
    a_vec = -kkn
    b_vec = kkn * asig

    cum = _mm_mask(tri_ref[...], ld)
    cum_ends = [cum[(j + 1) * c - 1:(j + 1) * c, :] for j in seqs]
    cum_last = jnp.concatenate([jnp.broadcast_to(e, (c, BR_W)) for e in cum_ends], axis=0)
    a_t = a_vec * jnp.exp(cum - ld)
    r_t = r * jnp.exp(cum)
    inv_p = jnp.exp(-cum)
    b_t = b_vec * inv_p
    k_t = k1 * inv_p
    to_end = jnp.exp(cum_last - cum)
    b_e = b_vec * to_end
    k_e = k1 * to_end

    seq_rows = RW_GROUP * c
    rows = ns * seq_rows
    pad = [jnp.zeros((rw_pad - rows, RW_GW), F32)] if rw_pad > rows else []
    m_strict = ms_ref[...]
    m_incl = mi_ref[...]
    groups = range(RW_HEADS // RW_GROUP)
    units = [(s0, g) for s0 in range(0, nb, ns) for g in groups]
    lanes = lambda g: slice(g * RW_GW, (g + 1) * RW_GW)
    of_seq = lambda x, i: x[i * seq_rows:(i + 1) * seq_rows]

    def stack(x, unit):
        s0, g = unit
        return jnp.concatenate([x[j * c:(j + 1) * c, lanes(g)] * hm_ref[hh:hh + 1, :]
                                for j in range(s0, s0 + ns) for hh in range(RW_GROUP)], axis=0)

    a4 = [stack(a_t, un) for un in units]
    r4 = [stack(r_t, un) for un in units]
    v4 = [stack(v, un) for un in units]
    gram = [_mm_nt(jnp.concatenate([a4[n], r4[n]], axis=0),
                   jnp.concatenate([stack(b_t, un), *pad, stack(k_t, un), *pad], axis=0))
            for n, un in enumerate(units)]
    from_state = [[_mm_nt(jnp.concatenate([of_seq(a4[n], i), of_seq(r4[n], i)], axis=0), sbd_scr[s0 + i, g])
                   for i in range(ns)] for n, (s0, g) in enumerate(units)]
    fs_a = [jnp.concatenate([f[0:seq_rows] for f in fs], axis=0) for fs in from_state]
    fs_r = [jnp.concatenate([f[seq_rows:2 * seq_rows] for f in fs], axis=0) for fs in from_state]
    low2 = [gm[0:rows] * m_strict for gm in gram]
    rhs_u = [fs_a[n] + _mm(low2[n][:, rw_pad:2 * rw_pad], jnp.concatenate([v4[n], *pad], axis=0))
             for n in range(len(units))]
    inv = _unit_lower_inverse([lw[:, 0:rows] for lw in low2], c)
    u = [_mm(inv[n], rhs_u[n]) for n in range(len(units))]
    o_unit = {}
    for n, (s0, g) in enumerate(units):
        uv = jnp.concatenate([u[n], *pad, v4[n], *pad], axis=0)
        o4 = fs_r[n] + _mm(gram[n][rows:2 * rows] * m_incl, uv)
        for i in range(ns):
            o_j = o4[i * seq_rows:i * seq_rows + c]
            for hh in range(1, RW_GROUP):
                o_j = o_j + o4[i * seq_rows + hh * c:i * seq_rows + (hh + 1) * c]
            o_unit[(s0 + i, g)] = o_j
        b4 = stack(b_e, (s0, g))
        k4 = stack(k_e, (s0, g))
        for i in range(ns):
            ends = jnp.concatenate([of_seq(b4, i), of_seq(k4, i)], axis=0)
            uv_j = jnp.concatenate([of_seq(u[n], i), of_seq(v4[n], i)], axis=0)
            sbd_scr[s0 + i, g] = sbd_scr[s0 + i, g] * jnp.exp(cum_ends[s0 + i][:, lanes(g)]) + _mm_tn(uv_j, ends)

    @pl.when(ci == n_chunks - 1)
    def _fin():
        for j in seqs:
            for h in range(RW_HEADS):
                g, sl = head_block(h)
                s_ref[j, h] = sbd_scr[j, g, sl, sl]

    o = jnp.concatenate([jnp.concatenate([o_unit[(j, g)] for g in groups], axis=1) for j in seqs], axis=0)
    inv_n = 1.0 / RW_HEAD
    mean = _mm(o, bd) * inv_n
    dev = o - mean
    var = _mm(dev * dev, bd) * inv_n
    normed = dev * lax.rsqrt(var + RW_LN_EPS) * lng_ref[...] + lnb_ref[...]
    bonus = _mm(r * k1 * rk_ref[...], bd) * v
    out = (normed + bonus) * g_a
    for j in seqs:
        o_ref[j] = out[j * c:(j + 1) * c].astype(o_ref.dtype)


def _block_masks(n_blocks, c):
    r = np.arange(n_blocks * c)
    same = (r[:, None] // c) == (r[None, :] // c)
    causal = (same & (r[None, :] <= r[:, None])).astype(np.float32)
    strict = (same & (r[None, :] < r[:, None])).astype(np.float32)
    return causal, strict


def _rwkv_masks(c, nb):
    rows = nb * RW_GROUP * c
    rw_pad = -(-rows // LANES) * LANES
    lane_head = np.arange(RW_GW) // RW_HEAD
    hm = (lane_head[None, :] == np.arange(RW_GROUP)[:, None]).astype(np.float32)
    incl, strict = _block_masks(nb * RW_GROUP, c)
    widen = lambda m: np.tile(np.pad(m, ((0, 0), (0, rw_pad - rows))), (1, 2))
    return rw_pad, jnp.asarray(hm), jnp.asarray(widen(strict)), jnp.asarray(widen(incl))


def _state_outputs(tails, *, batch, nb, stacked):
    zeros = lambda t: (0,) * len(t)
    if stacked is None:
        shapes = [jax.ShapeDtypeStruct((batch,) + t, F32) for t in tails]
        specs = [pl.BlockSpec((nb,) + t, functools.partial(lambda b, ci, z: (b,) + z, z=zeros(t))) for t in tails]
        return shapes, specs, [], {}
    depth, layer, prev = stacked
    shapes = [jax.ShapeDtypeStruct((depth, batch) + t, F32) for t in tails]
    specs = [pl.BlockSpec((None, nb) + t, functools.partial(lambda b, ci, z: (layer, b) + z, z=zeros(t))) for t in tails]
    prev = [] if prev is None else list(prev)
    return shapes, specs, prev, {i: 1 + i for i in range(len(prev))}


ANY_SPEC = pl.BlockSpec(memory_space=pl.ANY)


def _token_specs(cols, *, c, nb, nc, row_block0):
    return [pl.BlockSpec((c, cols), functools.partial(lambda b, ci, j: (row_block0 + (b * nb + j) * nc + ci, 0), j=j))
            for j in range(nb)]


def _rwkv(pa, shift0, wkv0, layer, prm, *, batch, seq, c, nb, ns, row_block0, stacked):
    nc = seq // c
    assert batch % nb == 0 and nb % ns == 0
    rw_pad, hm, m_strict, m_incl = _rwkv_masks(c, ns)
    rows = ns * RW_GROUP * c
    tri = jnp.asarray(_block_masks(nb, c)[0]).astype(BF16)
    st_shapes, st_specs, prev, aliases = _state_outputs(
        [(RW_HEADS, RW_HEAD, RW_HEAD), (1, RW_COLS)], batch=batch, nb=nb, stacked=stacked)
    kern = functools.partial(_rwkv_kernel, c=c, nb=nb, ns=ns, n_chunks=nc, rw_pad=rw_pad, n_alias=len(prev))
    vec = lambda n: pl.BlockSpec((1, n), lambda b, ci: (0, 0))
    mat = lambda r_, n: pl.BlockSpec((r_, n), lambda b, ci: (0, 0))
    prm = tuple(prm) + (hm, m_strict, m_incl, tri)
    return pl.pallas_call(
        kern,
        out_shape=(jax.ShapeDtypeStruct((batch, seq, BR_W), BF16), *st_shapes),
        grid=(batch // nb, nc),
        input_output_aliases=aliases,
        in_specs=[*([ANY_SPEC] * len(prev)), *_token_specs(RW_COLS, c=c, nb=nb, nc=nc, row_block0=row_block0),
                  pl.BlockSpec((None, nb, 1, RW_COLS), lambda b, ci: (layer, b, 0, 0)),
                  pl.BlockSpec((None, nb, RW_HEADS, RW_HEAD, RW_HEAD), lambda b, ci: (layer, b, 0, 0, 0)),
                  vec(RW_COLS), vec(512), mat(128, 512), vec(512), mat(128, 512), mat(128, 512),
                  vec(512), vec(512), vec(512), vec(512), vec(512), mat(512, 512),
                  mat(RW_GROUP, RW_GW), mat(rows, 2 * rw_pad), mat(rows, 2 * rw_pad), mat(nb * c, nb * c)],
        out_specs=(pl.BlockSpec((nb, c, BR_W), lambda b, ci: (b, ci, 0)), *st_specs),
        scratch_shapes=[pltpu.VMEM((nb, SUBLANES + c, RW_COLS), F32),
                        pltpu.VMEM((nb, RW_HEADS // RW_GROUP, RW_GW, RW_GW), F32)],
        compiler_params=_cparams(2),
        name="rwkv7",
    )(*prev, *([pa] * nb), shift0, wkv0, *prm)


def _ret_kernel(*refs, c, nb, n_alias):
    refs = refs[n_alias:]
    tok_refs = refs[:nb]
    cos_ref, sin_ref, mask_ref, s0_ref, o_ref, s_ref = refs[nb:]
    ci = pl.program_id(1)

    @pl.when(ci == 0)
    def _init():
        s_ref[...] = s0_ref[...]

    rows = nb * c
    width = RET_HEADS * RET_DK
    half = RET_DK // 2
    cos = jnp.concatenate([cos_ref[...]] * nb, axis=0)
    sin = jnp.concatenate([sin_ref[...]] * nb, axis=0)
    first_half = (_iota((rows, width), 1) & half) == 0

    def rot(x):
        partner = jnp.where(first_half, pltpu.roll(x, width - half, 1), pltpu.roll(x, half, 1))
        return x * cos + partner * sin

    pb = jnp.concatenate([t[...] for t in tok_refs], axis=0)
    q = rot(pb[:, 0:width])
    k = rot(pb[:, width:2 * width]) * (RET_DK ** -0.5)
    v = pb[:, 2 * width:2 * width + BR_W]
    g_b = pb[:, 2 * width + BR_W:2 * width + 2 * BR_W]

    mask = mask_ref[...]
    causal = mask > 0.5
    dist = jnp.where(causal, (_iota((rows, rows), 0) - _iota((rows, rows), 1)).astype(F32), 0.0)
    t1 = jnp.sum(mask, axis=-1, keepdims=True)
    for h in range(RET_HEADS):
        log_gamma = math.log1p(-(2.0 ** (-5.0 - h)))
        dmat = jnp.where(causal, jnp.exp(dist * log_gamma), 0.0)
        q_h = q[:, h * RET_DK:(h + 1) * RET_DK]
        k_h = k[:, h * RET_DK:(h + 1) * RET_DK]
        v_h = v[:, h * RET_DV:(h + 1) * RET_DV]
        k_dec = k_h * jnp.exp((c - t1) * log_gamma)
        from_state = []
        for j in range(nb):
            js = slice(j * c, (j + 1) * c)
            s_h = s_ref[j, h]
            from_state.append(_mm(q_h[js], s_h))
            s_ref[j, h] = math.exp(c * log_gamma) * s_h + _mm_tn(k_dec[js], v_h[js])
        o_h = _mm(_mm_nt(q_h, k_h) * dmat, v_h) + jnp.exp(t1 * log_gamma) * jnp.concatenate(from_state, axis=0)
        o_n = o_h * lax.rsqrt(jnp.mean(o_h * o_h, axis=-1, keepdims=True) + NORM_EPS)
        o_h = o_n * _silu(g_b[:, h * RET_DV:(h + 1) * RET_DV])
        for j in range(nb):
            o_ref[j, :, h * RET_DV:(h + 1) * RET_DV] = o_h[j * c:(j + 1) * c].astype(o_ref.dtype)


def _ret(pb, cos, sin, ret0, layer, *, batch, seq, c, nb, row_block0, stacked):
    nc = seq // c
    assert batch % nb == 0
    width = RET_HEADS * RET_DK
    mask = jnp.asarray(_block_masks(nb, c)[0])
    st_shapes, st_specs, prev, aliases = _state_outputs(
        [(RET_HEADS, RET_DK, RET_DV)], batch=batch, nb=nb, stacked=stacked)
    return pl.pallas_call(
        functools.partial(_ret_kernel, c=c, nb=nb, n_alias=len(prev)),
        out_shape=(jax.ShapeDtypeStruct((batch, seq, BR_W), BF16), *st_shapes),
        grid=(batch // nb, nc),
        input_output_aliases=aliases,
        in_specs=[*([ANY_SPEC] * len(prev)), *_token_specs(RET_COLS, c=c, nb=nb, nc=nc, row_block0=row_block0),
                  pl.BlockSpec((c, width), lambda b, ci: (ci, 0)),
                  pl.BlockSpec((c, width), lambda b, ci: (ci, 0)),
                  pl.BlockSpec((nb * c, nb * c), lambda b, ci: (0, 0)),
                  pl.BlockSpec((None, nb, RET_HEADS, RET_DK, RET_DV), lambda b, ci: (layer, b, 0, 0, 0))],
        out_specs=(pl.BlockSpec((nb, c, BR_W), lambda b, ci: (b, ci, 0)), *st_specs),
        compiler_params=_cparams(2),
        name="retention",
    )(*prev, *([pb] * nb), cos, sin, mask, ret0)


def _hgrn_kernel(*refs, c, m, nb, layer, n_chunks, n_alias):
    refs = refs[n_alias:]
    tok_refs = refs[:nb]
    lbp_ref, ng_ref, tri_ref, s0_ref, o_ref, s_ref, st_scr = refs[nb:]
    ci = pl.program_id(1)

    @pl.when(ci == 0)
    def _init():
        for j in range(nb):
            for h in range(HG_HEADS):
                st_scr[j, h] = s0_ref[j, h].T

    lbp = lbp_ref[...]
    e = jnp.exp(lbp - jnp.max(lbp, axis=0, keepdims=True))
    soft = e / jnp.sum(e, axis=0, keepdims=True)
    lb = jnp.zeros((1, BR_W), F32)
    for j in range(1, layer + 1):
        lb = lb + soft[j:j + 1, :]

    pc = jnp.concatenate([t[...] for t in tok_refs], axis=0)
    q_all = _silu(pc[:, 0:BR_W])
    f_in = pc[:, BR_W:2 * BR_W]
    v_all = pc[:, 2 * BR_W:3 * BR_W]
    g_c = pc[:, 3 * BR_W:4 * BR_W]
    log_f = jnp.log(lb + (1.0 - lb) * _sigmoid(f_in))
    k_all = (1.0 - lb) * _sigmoid(-f_in)
    cum_all = _mm_mask(tri_ref[...], log_f)

    col_m = _iota((m, m), 1)
    row_l = _iota((m, LANES), 0)
    outs = {}
    for j in range(nb):
        js = slice(j * c, (j + 1) * c)
        q, k, v, cum = q_all[js], k_all[js], v_all[js], cum_all[js]
        cum_last = cum[c - 1:c, :]
        q_e = q * jnp.exp(cum)
        k_e = k * jnp.exp(cum_last - cum)
        p_end = jnp.exp(cum_last)
        for h in range(HG_HEADS):
            sl = slice(h * HG_D, (h + 1) * HG_D)
            st = st_scr[j, h]
            o_rows = []
            for i in range(c // m):
                r0 = i * m
                q_i = q[r0:r0 + m, sl]
                c_i = cum[r0:r0 + m, sl]
                k_i = k[r0:r0 + m, sl]
                diag = jnp.zeros((m, m), F32)
                for s in range(m):
                    keep = row_l >= s
                    decay = jnp.exp(jnp.where(keep, c_i - c_i[s:s + 1, :], 0.0))
                    col = jnp.sum(jnp.where(keep, q_i * k_i[s:s + 1, :] * decay, 0.0), axis=-1, keepdims=True)
                    diag = jnp.where(col_m == s, col, diag)
                o_i = _mm(diag, v[r0:r0 + m, sl])
                if i > 0:
                    c_ref = cum[r0 - 1:r0, sl]
                    q_s = q_i * jnp.exp(c_i - c_ref)
                    k_s = k[0:r0, sl] * jnp.exp(c_ref - cum[0:r0, sl])
                    o_i = o_i + _mm(_mm_nt(q_s, k_s), v[0:r0, sl])
                o_rows.append(o_i)
            outs[(j, h)] = _mm_nt(q_e[:, sl], st) + jnp.concatenate(o_rows, axis=0)
            st_scr[j, h] = st * p_end[:, sl] + _mm_tn(v[:, sl], k_e[:, sl])
    for h in range(HG_HEADS):
        sl = slice(h * HG_D, (h + 1) * HG_D)
        o_h = jnp.concatenate([outs[(j, h)] for j in range(nb)], axis=0)
        o_n = o_h * lax.rsqrt(jnp.mean(o_h * o_h, axis=-1, keepdims=True) + NORM_EPS) * ng_ref[:, sl]
        o_h = o_n * _silu(g_c[:, sl])
        for j in range(nb):
            o_ref[j, :, sl] = o_h[j * c:(j + 1) * c].astype(o_ref.dtype)

    @pl.when(ci == n_chunks - 1)
    def _fin():
        for j in range(nb):
            for h in range(HG_HEADS):
                s_ref[j, h] = st_scr[j, h].T


def _hgrn(pc, lbp, ng, hg0, state_layer, layer, *, batch, seq, c, nb, row_block0, stacked):
    nc = seq // c
    assert batch % nb == 0
    m = 16 if c % 16 == 0 else SUBLANES
    depth = lbp.shape[0]
    tri = jnp.asarray(_block_masks(nb, c)[0]).astype(BF16)
    st_shapes, st_specs, prev, aliases = _state_outputs([(HG_HEADS, HG_D, HG_D)], batch=batch, nb=nb, stacked=stacked)
    return pl.pallas_call(
        functools.partial(_hgrn_kernel, c=c, m=m, nb=nb, layer=layer, n_chunks=nc, n_alias=len(prev)),
        out_shape=(jax.ShapeDtypeStruct((batch, seq, BR_W), BF16), *st_shapes),
        grid=(batch // nb, nc),
        input_output_aliases=aliases,
        in_specs=[*([ANY_SPEC] * len(prev)), *_token_specs(HG_COLS, c=c, nb=nb, nc=nc, row_block0=row_block0),
                  pl.BlockSpec((depth, BR_W), lambda b, ci: (0, 0)),
                  pl.BlockSpec((1, BR_W), lambda b, ci: (0, 0)),
                  pl.BlockSpec((nb * c, nb * c), lambda b, ci: (0, 0)),
                  pl.BlockSpec((None, nb, HG_HEADS, HG_D, HG_D), lambda b, ci: (state_layer, b, 0, 0, 0))],
        out_specs=(pl.BlockSpec((nb, c, BR_W), lambda b, ci: (b, ci, 0)), *st_specs),
        scratch_shapes=[pltpu.VMEM((nb, HG_HEADS, HG_D, HG_D), F32)],
        compiler_params=_cparams(2),
        name="hgrn2",
    )(*prev, *([pc] * nb), lbp, ng, tri, hg0)


def _gdn_kernel(*refs, c, nb, ns, n_alias):
    refs = refs[n_alias:]
    tok_refs = refs[:nb]
    (cs0_ref, s0_ref, cw_ref, alog_ref, dtb_ref, ng_ref, mc_ref, mst_ref, tri_ref,
     o_ref, s_ref, cs_ref, ext_scr) = refs[nb:]
    ci = pl.program_id(1)
    keep = GD_CONV - 1

    @pl.when(ci == 0)
    def _init():
        for j in range(nb):
            ext_scr[j, SUBLANES - keep:SUBLANES, :] = cs0_ref[j]
        s_ref[...] = s0_ref[...]

    convs = []
    for j in range(nb):
        x = tok_refs[j][:, 0:GD_QKV]
        ext_scr[j, SUBLANES:SUBLANES + c, :] = x
        conv = ext_scr[j, SUBLANES - keep:SUBLANES - keep + c, :] * cw_ref[0:1, :]
        for i in range(1, keep):
            conv = conv + ext_scr[j, SUBLANES - keep + i:SUBLANES - keep + i + c, :] * cw_ref[i:i + 1, :]
        convs.append(conv + x * cw_ref[keep:keep + 1, :])
        tail = ext_scr[j, SUBLANES + c - keep:SUBLANES + c, :]
        ext_scr[j, SUBLANES - keep:SUBLANES, :] = tail
        cs_ref[j] = tail

    act = _silu(jnp.concatenate(convs, axis=0))
    q = act[:, 0:BR_W]
    k = act[:, BR_W:2 * BR_W]
    v = act[:, 2 * BR_W:3 * BR_W]
    gate_ba = jnp.concatenate([t[:, GD_QKV:GD_COLS_PAD] for t in tok_refs], axis=0)
    g_d = gate_ba[:, 0:BR_W]
    ba = gate_ba[:, BR_W:BR_W + LANES]
    beta_all = _sigmoid(ba)
    g_all = -jnp.exp(alog_ref[...]) * _softplus(ba + dtb_ref[...])

    cum_all = _mm_mask(tri_ref[...], g_all)

    ur = ns * c
    rows = GD_HEADS * ur
    stack = lambda parts: jnp.concatenate(parts, axis=0)
    l2n = lambda z: z * lax.rsqrt(jnp.sum(z * z, axis=-1, keepdims=True) + 1e-6)
    head = lambda z, h: z[:, h * GD_D:(h + 1) * GD_D]
    qn = jnp.concatenate([l2n(head(q, h)) for h in range(GD_HEADS)], axis=1) * (GD_D ** -0.5)
    kn = jnp.concatenate([l2n(head(k, h)) for h in range(GD_HEADS)], axis=1)
    units = range(nb // ns)
    heads = range(GD_HEADS)
    stack_u = lambda z, un: stack([head(z[un * ur:(un + 1) * ur], h) for h in heads])
    col_u = lambda z, lane0, un: stack([z[un * ur:(un + 1) * ur, lane0 + h:lane0 + h + 1] for h in heads])
    q4 = [stack_u(qn, un) for un in units]
    k4 = [stack_u(kn, un) for un in units]
    v4 = [stack_u(v, un) for un in units]
    beta = [col_u(beta_all, 0, un) for un in units]
    cum = [col_u(cum_all, GD_HEADS, un) for un in units]
    causal = mc_ref[...] > 0.5
    m_strict = mst_ref[...]
    eye = _iota((rows, rows), 0) == _iota((rows, rows), 1)
    ones = jnp.ones((rows, rows), BF16)
    cum_t = [jnp.broadcast_to(cm, (rows, rows)) for cm in cum]
    cum_s = [_mm_mask(ones, jnp.where(eye, ct, 0.0)) for ct in cum_t]
    dmat = [jnp.where(causal, jnp.exp(jnp.where(causal, ct - cs_, 0.0)), 0.0) for ct, cs_ in zip(cum_t, cum_s)]
    a_mat = [beta[un] * _mm_nt(k4[un], k4[un]) * (dmat[un] * m_strict) for un in units]
    inv = _unit_lower_inverse([-a for a in a_mat], c)
    e_cum = [jnp.exp(cm) for cm in cum]
    sol = [_mm(inv[un], jnp.concatenate([(beta[un] * e_cum[un]) * k4[un], beta[un] * v4[un]], axis=1)) for un in units]
    blocks = [(h, i) for h in heads for i in range(ns)]
    rows_of = lambda h, i: slice((h * ns + i) * c, (h * ns + i + 1) * c)
    qw = [[_mm(jnp.concatenate([q4[un][rows_of(h, i)], sol[un][rows_of(h, i), 0:GD_D]], axis=0), s_ref[un * ns + i, h])
           for h, i in blocks] for un in units]
    delta = [sol[un][:, GD_D:2 * GD_D] - stack([z[c:2 * c] for z in qw[un]]) for un in units]
    o4 = [e_cum[un] * stack([z[0:c] for z in qw[un]]) + _mm(_mm_nt(q4[un], k4[un]) * dmat[un], delta[un])
          for un in units]
    for un in units:
        for h, i in blocks:
            hs = rows_of(h, i)
            cum_h = cum[un][hs]
            cum_last = cum_h[c - 1:c, :]
            s_ref[un * ns + i, h] = (jnp.exp(cum_last) * s_ref[un * ns + i, h]
                                     + _mm_tn(k4[un][hs] * jnp.exp(cum_last - cum_h), delta[un][hs]))
    for h in heads:
        sl = slice(h * GD_D, (h + 1) * GD_D)
        o_h = stack([o4[un][h * ur:(h + 1) * ur] for un in units])
        o_n = o_h * lax.rsqrt(jnp.mean(o_h * o_h, axis=-1, keepdims=True) + NORM_EPS) * ng_ref[:, sl]
        o_h = o_n * _silu(g_d[:, sl])
        for j in range(nb):
            o_ref[j, :, sl] = o_h[j * c:(j + 1) * c].astype(o_ref.dtype)


def _gdn(pd, conv0, gd0, layer, cw, alog, dtb, ng, *, batch, seq, c, nb, ns, row_block0, stacked):
    nc = seq // c
    assert batch % nb == 0 and nb % ns == 0
    keep = GD_CONV - 1
    rows = GD_HEADS * ns * c
    m_causal, m_strict = (jnp.asarray(m) for m in _block_masks(GD_HEADS * ns, c))
    tri = jnp.asarray(_block_masks(nb, c)[0]).astype(BF16)
    full = lambda *shape: pl.BlockSpec(shape, lambda b, ci: (0,) * len(shape))
    st_shapes, st_specs, prev, aliases = _state_outputs(
        [(GD_HEADS, GD_D, GD_D), (keep, GD_QKV)], batch=batch, nb=nb, stacked=stacked)
    return pl.pallas_call(
        functools.partial(_gdn_kernel, c=c, nb=nb, ns=ns, n_alias=len(prev)),
        out_shape=(jax.ShapeDtypeStruct((batch, seq, BR_W), BF16), *st_shapes),
        grid=(batch // nb, nc),
        input_output_aliases=aliases,
        in_specs=[*([ANY_SPEC] * len(prev)), *_token_specs(GD_COLS_PAD, c=c, nb=nb, nc=nc, row_block0=row_block0),
                  pl.BlockSpec((None, nb, keep, GD_QKV), lambda b, ci: (layer, b, 0, 0)),
                  pl.BlockSpec((None, nb, GD_HEADS, GD_D, GD_D), lambda b, ci: (layer, b, 0, 0, 0)),
                  full(GD_CONV, GD_QKV), full(1, LANES), full(1, LANES), full(1, BR_W),
                  full(rows, rows), full(rows, rows), full(nb * c, nb * c)],
        out_specs=(pl.BlockSpec((nb, c, BR_W), lambda b, ci: (b, ci, 0)), *st_specs),
        scratch_shapes=[pltpu.VMEM((nb, SUBLANES + c, GD_QKV), F32)],
        compiler_params=_cparams(2),
        name="gdn",
    )(*prev, *([pd] * nb), conv0, gd0, cw, alog, dtb, ng, m_causal, m_strict, tri)


def _pick_chunk(seq):
    for c in (64, 56, 48, 40, 32, 24, 16, 8):
        if seq % c == 0:
            return c
    raise ValueError(f"sequence length {seq} is not a multiple of {SUBLANES}")


def _pick_tile(n, candidates):
    for t in candidates:
        if n % t == 0:
            return t
    raise ValueError(f"no tile in {candidates} divides {n}")


def _rope_tables(pos):
    half = RET_DK // 2
    inv = ROPE_BASE ** (-jnp.arange(half, dtype=F32) / half)
    ang = pos.astype(F32)[:, None] * inv[None, :]
    cos, sin = jnp.cos(ang), jnp.sin(ang)
    cos_t = jnp.tile(jnp.concatenate([cos, cos], axis=1), (1, RET_HEADS))
    sin_t = jnp.tile(jnp.concatenate([-sin, sin], axis=1), (1, RET_HEADS))
    return cos_t, sin_t


def kernel(x_prompt, x_sample, state_rwkv_wkv, state_rwkv_shift, state_ret, state_hgrn, state_gdn, state_gdn_conv, meta_tokens, norm_mix, w_in, rw_mu, rw_w0, rw_w2, rw_a0, rw_a2, rw_g2, rw_kk, rw_ka, rw_rk, rw_ln_g, rw_ln_b, hg_lb, hg_norm_g, gd_conv, gd_a_log, gd_dt_bias, gd_norm_g, w_branch, w_out, norm_ffn, w_up, w_down, norm_final):
    depth = norm_mix.shape[0]
    bp, tq, d = x_prompt.shape
    tp = tq + N_META
    bs, ts, _ = x_sample.shape
    n_p, n_s = bp * tp, bs * ts
    n_tok = n_p + n_s
    cp, cs = _pick_chunk(tp), _pick_chunk(ts)
    assert cs == ts and n_p % ts == 0
    tm = _pick_tile(n_tok, (464, 512, 256, 232, 128, 64, 32, 16, 8))
    tm_in = _pick_tile(n_tok, (928, 464, 512, 256, 232, 128, 64, 32, 16, 8))
    nb_s = _pick_tile(bs, (8, 4, 2, 1))
    nb_p = _pick_tile(bp, (4, 2, 1))

    meta = jnp.broadcast_to(meta_tokens.astype(F32)[None], (bp, N_META, d))
    x = jnp.concatenate([jnp.concatenate([meta, x_prompt], axis=1).reshape(n_p, d), x_sample.reshape(n_s, d)], axis=0)

    cos_p, sin_p = _rope_tables(jnp.arange(tp, dtype=jnp.int32))
    cos_s, sin_s = _rope_tables(PAST_LEN + jnp.arange(ts, dtype=jnp.int32))

    head_of = jnp.arange(BR_W) // RW_HEAD
    bd = (head_of[:, None] == head_of[None, :]).astype(BF16)

    zeros = lambda *s: jnp.zeros((1, bp) + s, F32)
    z_wkv, z_shift = zeros(RW_HEADS, RW_HEAD, RW_HEAD), zeros(1, RW_COLS)
    z_ret, z_hg, z_gd = zeros(RET_HEADS, RET_DK, RET_DV), zeros(HG_HEADS, HG_D, HG_D), zeros(GD_HEADS, GD_D, GD_D)
    z_conv = zeros(GD_CONV - 1, GD_QKV)
    shift_s = state_rwkv_shift.reshape(depth, bs, 1, RW_COLS)

    o_a, o_b, o_c = RW_COLS, RW_COLS + RET_COLS, RW_COLS + RET_COLS + HG_COLS
    o_g = o_c + GD_QKV + 2 * GD_HEADS + BR_W
    new_p = [[] for _ in range(6)]
    s_stacks = [jnp.zeros((depth, bs) + tail, F32) for tail in (
        (RW_HEADS, RW_HEAD, RW_HEAD), (1, RW_COLS), (RET_HEADS, RET_DK, RET_DV), (HG_HEADS, HG_D, HG_D),
        (GD_HEADS, GD_D, GD_D), (GD_CONV - 1, GD_QKV))]
    h = _rmsnorm(x, norm_mix[0], tm, BF16)
    for l in range(depth):
        wl = w_in[l]
        w_a = wl[:, 0:o_a].astype(BF16)
        w_b = wl[:, o_a:o_b].astype(BF16)
        w_c = wl[:, o_b:o_c].astype(BF16)
        w_d = jnp.concatenate([wl[:, o_c:o_c + GD_QKV], wl[:, o_c + GD_QKV + 2 * GD_HEADS:o_g],
                               wl[:, o_c + GD_QKV:o_c + GD_QKV + 2 * GD_HEADS],
                               jnp.zeros((d, LANES - 2 * GD_HEADS), F32)], axis=1).astype(BF16)
        w_g = wl[:, o_g:].astype(BF16)

        pa = _matmul(h, w_a, tm_in, 896, "in_proj_a")
        pb = _matmul(h, w_b, tm_in, 768, "in_proj_b")
        pc = _matmul(h, w_c, tm_in, 1024, "in_proj_c")
        pd = _matmul(h, w_d, tm_in, GD_COLS_PAD, "in_proj_d")
        gates = _matmul(h, w_g, tm_in, 1024, "in_proj_g", gate=True)

        row1 = lambda a: a.reshape(1, -1)
        pad_rows = lambda a, top: jnp.concatenate(
            [jnp.zeros((top, BR_W), F32), a, jnp.zeros((LANES - top - a.shape[0], BR_W), F32)], axis=0).astype(BF16)
        rw_prm = (row1(rw_mu[l]), row1(rw_w0[l]), pad_rows(rw_w2[l], 0), row1(rw_a0[l]), pad_rows(rw_a2[l], 64),
                  rw_g2[l].astype(BF16), row1(rw_kk[l]), row1(rw_ka[l]), row1(rw_rk[l]), row1(rw_ln_g[l]),
                  row1(rw_ln_b[l]), bd)
        lane_pad = lambda a: jnp.concatenate(
            [jnp.zeros((GD_HEADS,), F32), a, jnp.zeros((LANES - 2 * GD_HEADS,), F32)]).reshape(1, LANES)
        gd_prm = (gd_conv[l], lane_pad(gd_a_log[l]), lane_pad(gd_dt_bias[l]), row1(gd_norm_g[l]))
        hg_g = row1(hg_norm_g[l])

        outs = []
        for grp, (batch, seq, c, nb, ns, rb0, sts, lay, cos, sin) in enumerate((
                (bp, tp, cp, nb_p, 1, 0, (z_wkv, z_shift, z_ret, z_hg, z_gd, z_conv), 0, cos_p, sin_p),
                (bs, ts, cs, nb_s, nb_s, n_p // ts,
                 (state_rwkv_wkv, shift_s, state_ret, state_hgrn, state_gdn, state_gdn_conv), l, cos_s, sin_s))):
            kw = dict(batch=batch, seq=seq, c=c, nb=nb, row_block0=rb0)
            stk = lambda *prev: None if grp == 0 else (depth, l, prev)
            oa, wkv, shift = _rwkv(pa, sts[1], sts[0], lay, rw_prm, ns=ns, stacked=stk(*s_stacks[0:2]), **kw)
            ob, ret = _ret(pb, cos, sin, sts[2], lay, stacked=stk(*s_stacks[2:3]), **kw)
            oc, hg = _hgrn(pc, hg_lb, hg_g, sts[3], lay, l, stacked=stk(*s_stacks[3:4]), **kw)
            od, gd, conv = _gdn(pd, sts[5], sts[4], lay, *gd_prm, ns=ns, stacked=stk(*s_stacks[4:6]), **kw)
            outs.append(tuple(o.reshape(batch * seq, BR_W) for o in (oa, ob, oc, od)))
            if grp == 0:
                for i, st in enumerate((wkv, shift.reshape(batch, RW_COLS), ret, hg, gd, conv)):
                    new_p[i].append(st)
            else:
                s_stacks = [wkv, shift, ret, hg, gd, conv]
        branch = [jnp.concatenate([outs[0][n], outs[1][n]], axis=0) for n in range(4)]
        wb = w_branch[l].astype(BF16)
        x = _merge(x, branch, gates, wb, w_out[l].astype(BF16), tm_in, 256)
        last = l == depth - 1
        res = _ffn(x, norm_ffn[l], norm_final if last else norm_mix[l + 1], w_up[l].astype(BF16),
                   w_down[l].astype(BF16), tm, 512, last)
        if last:
            y = res
        else:
            x, h = res

    y_prompt = y[:n_p].reshape(bp, tp, d)[:, N_META:]
    y_sample = y[n_p:].reshape(bs, ts, d)
    p_states = tuple(jnp.stack(z) for z in new_p)
    s_wkv, s_shift, s_ret, s_hg, s_gd, s_conv = s_stacks
    return (y_prompt, y_sample) + p_states + (s_wkv, s_shift.reshape(depth, bs, RW_COLS), s_ret, s_hg, s_gd, s_conv)
```

```python
import functools
import math

import jax
import jax.numpy as jnp
import numpy as np
from jax import lax
from jax.experimental import pallas as pl
from jax.experimental.pallas import tpu as pltpu

F32 = jnp.float32
BF16 = jnp.bfloat16

N_META = 16
PAST_LEN = 16384
NORM_EPS = 1e-6
RW_LN_EPS = 64e-5
ROPE_BASE = 10000.0

D_MODEL = 2048
BR_W = 512
RW_HEADS, RW_HEAD = 8, 64
RW_COLS = 1792
RW_GROUP = 4
RW_GW = RW_GROUP * RW_HEAD
RET_HEADS, RET_DK, RET_DV = 4, 64, 128
RET_COLS = 1536
HG_HEADS, HG_D = 4, 128
HG_COLS = 2048
GD_HEADS, GD_D = 4, 128
GD_CONV = 4
GD_QKV = 1536
GD_COLS_PAD = 2176

LANES = 128
SUBLANES = 8
VMEM_LIMIT = 56 * 1024 * 1024


def _cparams(n_axes):
    return pltpu.CompilerParams(dimension_semantics=("arbitrary",) * n_axes, vmem_limit_bytes=VMEM_LIMIT)


def _dg(a, b, ca, cb):
    return lax.dot_general(a, b, (((ca,), (cb,)), ((), ())), preferred_element_type=F32)


def _mm(a, b):
    return _dg(a.astype(BF16), b.astype(BF16), 1, 0)


def _mm_nt(a, b):
    return _dg(a.astype(BF16), b.astype(BF16), 1, 1)


def _mm_tn(a, b):
    return _dg(a.astype(BF16), b.astype(BF16), 0, 0)


def _split(x, n):
    parts, r = [], x
    for _ in range(n):
        p = r.astype(BF16)
        parts.append(p)
        r = r - p.astype(F32)
    return parts


def _mm_mask(m01, x):
    x0, x1, x2 = _split(x, 3)
    return _dg(m01, x0, 1, 0) + (_dg(m01, x1, 1, 0) + _dg(m01, x2, 1, 0))


def _iota(shape, dim):
    return lax.broadcasted_iota(jnp.int32, shape, dim)


def _softplus(x):
    return jnp.maximum(x, 0.0) + jnp.log(1.0 + jnp.exp(-jnp.abs(x)))


def _sigmoid(x):
    return jax.nn.sigmoid(x)


def _silu(x):
    return x * jax.nn.sigmoid(x)


def _unit_lower_inverse(lows, c):
    shape = lows[0].shape
    eye = (_iota(shape, 0) == _iota(shape, 1)).astype(F32)
    invs = [eye + low for low in lows]
    powers = list(lows)
    covered = 1
    while covered < c - 1:
        powers = [_mm(p, p) for p in powers]
        invs = [inv + _mm(inv, p) for inv, p in zip(invs, powers)]
        covered = 2 * covered + 1
    return invs


def _rmsnorm_kernel(x_ref, g_ref, o_ref):
    x = x_ref[...]
    y = x * lax.rsqrt(jnp.mean(x * x, axis=-1, keepdims=True) + NORM_EPS)
    o_ref[...] = (y * g_ref[...]).astype(o_ref.dtype)


def _rmsnorm(x, g, tm, out_dtype):
    n, d = x.shape
    return pl.pallas_call(
        _rmsnorm_kernel,
        out_shape=jax.ShapeDtypeStruct((n, d), out_dtype),
        grid=(n // tm,),
        in_specs=[pl.BlockSpec((tm, d), lambda i: (i, 0)), pl.BlockSpec((1, d), lambda i: (0, 0))],
        out_specs=pl.BlockSpec((tm, d), lambda i: (i, 0)),
        compiler_params=_cparams(1),
        name="rmsnorm",
    )(x, g.reshape(1, d))


def _matmul_kernel(x_ref, w_ref, o_ref):
    o_ref[...] = _dg(x_ref[...], w_ref[...], 1, 0)


def _matmul_sigmoid_kernel(x_ref, w_ref, o_ref):
    o_ref[...] = _sigmoid(_dg(x_ref[...], w_ref[...], 1, 0)).astype(o_ref.dtype)


def _matmul(x, w, layer, tm, tn, name, gate=False):
    n, k = x.shape
    m = w.shape[2]
    return pl.pallas_call(
        _matmul_sigmoid_kernel if gate else _matmul_kernel,
        out_shape=jax.ShapeDtypeStruct((n, m), BF16 if gate else F32),
        grid=(n // tm, m // tn),
        in_specs=[pl.BlockSpec((tm, k), lambda i, j: (i, 0)), pl.BlockSpec((None, k, tn), lambda i, j: (layer, 0, j))],
        out_specs=pl.BlockSpec((tm, tn), lambda i, j: (i, j)),
        compiler_params=_cparams(2),
        name=name,
    )(x, w)


def _merge_kernel(x_ref, oa_ref, ob_ref, oc_ref, od_ref, g0_ref, g1_ref, g2_ref, g3_ref, wb_ref, wo_ref, y_ref):
    j = pl.program_id(1)

    @pl.when(j == 0)
    def _init():
        y_ref[...] = x_ref[...]

    merged = None
    for n, (o_ref, g_ref) in enumerate(((oa_ref, g0_ref), (ob_ref, g1_ref), (oc_ref, g2_ref), (od_ref, g3_ref))):
        term = g_ref[...].astype(F32) * _dg(o_ref[...], wb_ref[n], 1, 0)
        merged = term if merged is None else merged + term
    y_ref[...] += _dg(merged.astype(BF16), wo_ref[...], 1, 0)


def _merge(x, outs, pg, wb, wo, layer, tm, tj):
    n, d = x.shape
    nj = d // tj
    o_spec = pl.BlockSpec((tm, BR_W), lambda i, j: (i, 0))
    g_specs = [pl.BlockSpec((tm, tj), functools.partial(lambda i, j, nb: (i, nb * nj + j), nb=nb)) for nb in range(4)]
    return pl.pallas_call(
        _merge_kernel,
        out_shape=jax.ShapeDtypeStruct((n, d), F32),
        grid=(n // tm, nj),
        in_specs=[pl.BlockSpec((tm, d), lambda i, j: (i, 0)), o_spec, o_spec, o_spec, o_spec, *g_specs,
                  pl.BlockSpec((None, 4, BR_W, tj), lambda i, j: (layer, 0, 0, j)),
                  pl.BlockSpec((None, tj, d), lambda i, j: (layer, j, 0))],
        out_specs=pl.BlockSpec((tm, d), lambda i, j: (i, 0)),
        compiler_params=_cparams(2),
        name="merge",
    )(x, *outs, pg, pg, pg, pg, wb, wo)


def _rms(x, g):
    return x * lax.rsqrt(jnp.mean(x * x, axis=-1, keepdims=True) + NORM_EPS) * g


def _ffn_kernel(x_ref, g_ref, gn_ref, wu_ref, wg_ref, wd_ref, *out_and_scratch, last):
    *outs, h_ref, acc_ref = out_and_scratch
    j = pl.program_id(1)

    @pl.when(j == 0)
    def _init():
        h_ref[...] = _rms(x_ref[...], g_ref[...]).astype(BF16)
        acc_ref[...] = jnp.zeros_like(acc_ref)

    h = h_ref[...]
    up = _dg(h, wu_ref[...], 1, 0)
    gate = _dg(h, wg_ref[...], 1, 0)
    acc_ref[...] += _dg((_silu(gate) * up).astype(BF16), wd_ref[...], 1, 0)

    @pl.when(j == pl.num_programs(1) - 1)
    def _fin():
        y = x_ref[...] + acc_ref[...]
        if last:
            outs[0][...] = _rms(y, gn_ref[...])
        else:
            outs[0][...] = y
            outs[1][...] = _rms(y, gn_ref[...]).astype(BF16)


def _ffn(x, g, g_next, w_up, w_down, layer, tm, tf, last):
    n, d = x.shape
    ff = w_down.shape[1]
    nf = ff // tf
    row_spec = pl.BlockSpec((tm, d), lambda i, j: (i, 0))
    vec_spec = pl.BlockSpec((1, d), lambda i, j: (0, 0))
    if last:
        out_shape, out_specs = jax.ShapeDtypeStruct((n, d), F32), row_spec
    else:
        out_shape = (jax.ShapeDtypeStruct((n, d), F32), jax.ShapeDtypeStruct((n, d), BF16))
        out_specs = (row_spec, row_spec)
    return pl.pallas_call(
        functools.partial(_ffn_kernel, last=last),
        out_shape=out_shape,
        grid=(n // tm, nf),
        in_specs=[row_spec, vec_spec, vec_spec,
                  pl.BlockSpec((None, d, tf), lambda i, j: (layer, 0, j)),
                  pl.BlockSpec((None, d, tf), lambda i, j: (layer, 0, nf + j)),
                  pl.BlockSpec((None, tf, d), lambda i, j: (layer, j, 0))],
        out_specs=out_specs,
        scratch_shapes=[pltpu.VMEM((tm, d), BF16), pltpu.VMEM((tm, d), F32)],
        compiler_params=_cparams(2),
        name="ffn",
    )(x, g.reshape(1, d), g_next.reshape(1, d), w_up, w_up, w_down)


def _rwkv_kernel(*refs, c, nb, ns, n_chunks, rw_pad, n_alias):
    refs = refs[n_alias:]
    tok_refs = refs[:1]
    (sh0_ref, s0_ref, mu_ref, w0_ref, w2_ref, a0_ref, a2_ref, g2_ref, kkw_ref, ka_ref, rk_ref, lng_ref, lnb_ref,
     bd_ref, hm_ref, ms_ref, mi_ref, tri_ref, o_ref, s_ref, sh_ref, xs_scr, sbd_scr) = refs[1:]
    ci = pl.program_id(1)
    seqs = range(nb)
    head_block = lambda h: (h // RW_GROUP, slice((h % RW_GROUP) * RW_HEAD, (h % RW_GROUP + 1) * RW_HEAD))

    @pl.when(ci == 0)
    def _init():
        sbd_scr[...] = jnp.zeros_like(sbd_scr)
        for j in seqs:
            xs_scr[j, SUBLANES - 1:SUBLANES, :] = sh0_ref[j]
            for h in range(RW_HEADS):
                g, sl = head_block(h)
                sbd_scr[j, g, sl, sl] = s0_ref[j, h]

    pa = jnp.concatenate([t[...] for t in tok_refs], axis=0)
    prevs = []
    for j in seqs:
        pa_j = pa[j * c:(j + 1) * c, :]
        xs_scr[j, SUBLANES:SUBLANES + c, :] = pa_j
        prevs.append(xs_scr[j, SUBLANES - 1:SUBLANES - 1 + c, :])
        last = pa_j[c - 1:c, :]
        xs_scr[j, SUBLANES - 1:SUBLANES, :] = last
        sh_ref[j] = last
    prev = jnp.concatenate(prevs, axis=0)

    xm = pa + (prev - pa) * mu_ref[...]
    r = xm[:, 0:512]
    k0 = xm[:, 512:1024]
    v = xm[:, 1024:1536]
    wa = xm[:, 1536:1664]
    gl = xm[:, 1664:1792]
    w_log = -_softplus(-(w0_ref[...] + _mm(jnp.tanh(wa), w2_ref[...]))) - 0.5
    ld = -jnp.exp(w_log)
    asig = _sigmoid(a0_ref[...] + _mm(wa, a2_ref[...]))
    g_a = _mm(_sigmoid(gl), g2_ref[...])
    bd = bd_ref[...]
    kkr = k0 * kkw_ref[...]
    kkn = kkr * lax.rsqrt(_mm(kkr * kkr, bd) + 1e-6)
    k1 = k0 * (1.0 + (asig - 1.0) * ka_ref[...])
    a_vec = -kkn
    b_vec = kkn * asig

    cum = _mm_mask(tri_ref[...], ld)
    cum_ends = [cum[(j + 1) * c - 1:(j + 1) * c, :] for j in seqs]
    cum_last = jnp.concatenate([jnp.broadcast_to(e, (c, BR_W)) for e in cum_ends], axis=0)
    a_t = a_vec * jnp.exp(cum - ld)
    r_t = r * jnp.exp(cum)
    inv_p = jnp.exp(-cum)
    b_t = b_vec * inv_p
    k_t = k1 * inv_p
    to_end = jnp.exp(cum_last - cum)
    b_e = b_vec * to_end
    k_e = k1 * to_end

    seq_rows = RW_GROUP * c
    rows = ns * seq_rows
    pad = [jnp.zeros((rw_pad - rows, RW_GW), F32)] if rw_pad > rows else []
    m_strict = ms_ref[...]
    m_incl = mi_ref[...]
    groups = range(RW_HEADS // RW_GROUP)
    units = [(s0, g) for s0 in range(0, nb, ns) for g in groups]
    lanes = lambda g: slice(g * RW_GW, (g + 1) * RW_GW)
    of_seq = lambda x, i: x[i * seq_rows:(i + 1) * seq_rows]

    def stack(x, unit):
        s0, g = unit
        return jnp.concatenate([x[j * c:(j + 1) * c, lanes(g)] * hm_ref[hh:hh + 1, :]
                                for j in range(s0, s0 + ns) for hh in range(RW_GROUP)], axis=0)

    a4 = [stack(a_t, un) for un in units]
    r4 = [stack(r_t, un) for un in units]
    v4 = [stack(v, un) for un in units]
    gram = [_mm_nt(jnp.concatenate([a4[n], r4[n]], axis=0),
                   jnp.concatenate([stack(b_t, un), *pad, stack(k_t, un), *pad], axis=0))
            for n, un in enumerate(units)]
    from_state = [[_mm_nt(jnp.concatenate([of_seq(a4[n], i), of_seq(r4[n], i)], axis=0), sbd_scr[s0 + i, g])
                   for i in range(ns)] for n, (s0, g) in enumerate(units)]
    fs_a = [jnp.concatenate([f[0:seq_rows] for f in fs], axis=0) for fs in from_state]
    fs_r = [jnp.concatenate([f[seq_rows:2 * seq_rows] for f in fs], axis=0) for fs in from_state]
    low2 = [gm[0:rows] * m_strict for gm in gram]
    rhs_u = [fs_a[n] + _mm(low2[n][:, rw_pad:2 * rw_pad], jnp.concatenate([v4[n], *pad], axis=0))
             for n in range(len(units))]
    inv = _unit_lower_inverse([lw[:, 0:rows] for lw in low2], c)
    u = [_mm(inv[n], rhs_u[n]) for n in range(len(units))]
    o_unit = {}
    for n, (s0, g) in enumerate(units):
        uv = jnp.concatenate([u[n], *pad, v4[n], *pad], axis=0)
        o4 = fs_r[n] + _mm(gram[n][rows:2 * rows] * m_incl, uv)
        for i in range(ns):
            o_j = o4[i * seq_rows:i * seq_rows + c]
            for hh in range(1, RW_GROUP):
                o_j = o_j + o4[i * seq_rows + hh * c:i * seq_rows + (hh + 1) * c]
            o_unit[(s0 + i, g)] = o_j
        b4 = stack(b_e, (s0, g))
        k4 = stack(k_e, (s0, g))
        for i in range(ns):
            ends = jnp.concatenate([of_seq(b4, i), of_seq(k4, i)], axis=0)
            uv_j = jnp.concatenate([of_seq(u[n], i), of_seq(v4[n], i)], axis=0)
            sbd_scr[s0 + i, g] = sbd_scr[s0 + i, g] * jnp.exp(cum_ends[s0 + i][:, lanes(g)]) + _mm_tn(uv_j, ends)

    @pl.when(ci == n_chunks - 1)
    def _fin():
        for j in seqs:
            for h in range(RW_HEADS):
                g, sl = head_block(h)
                s_ref[j, h] = sbd_scr[j, g, sl, sl]

    o = jnp.concatenate([jnp.concatenate([o_unit[(j, g)] for g in groups], axis=1) for j in seqs], axis=0)
    inv_n = 1.0 / RW_HEAD
    mean = _mm(o, bd) * inv_n
    dev = o - mean
    var = _mm(dev * dev, bd) * inv_n
    normed = dev * lax.rsqrt(var + RW_LN_EPS) * lng_ref[...] + lnb_ref[...]
    bonus = _mm(r * k1 * rk_ref[...], bd) * v
    out = (normed + bonus) * g_a
    o_ref[...] = out.astype(o_ref.dtype)


def _block_masks(n_blocks, c):
    r = np.arange(n_blocks * c)
    same = (r[:, None] // c) == (r[None, :] // c)
    causal = (same & (r[None, :] <= r[:, None])).astype(np.float32)
    strict = (same & (r[None, :] < r[:, None])).astype(np.float32)
    return causal, strict


def _rwkv_masks(c, nb):
    rows = nb * RW_GROUP * c
    rw_pad = -(-rows // LANES) * LANES
    lane_head = np.arange(RW_GW) // RW_HEAD
    hm = (lane_head[None, :] == np.arange(RW_GROUP)[:, None]).astype(np.float32)
    incl, strict = _block_masks(nb * RW_GROUP, c)
    widen = lambda m: np.tile(np.pad(m, ((0, 0), (0, rw_pad - rows))), (1, 2))
    return rw_pad, jnp.asarray(hm), jnp.asarray(widen(strict)), jnp.asarray(widen(incl))


ANY_SPEC = pl.BlockSpec(memory_space=pl.ANY)


def _mixer_io(cols, tails, *, o_prev, batch, c, nb, nc, row_block0, stacked):
    zeros = lambda t: (0,) * len(t)
    if stacked is None:
        prev = []
        st_shapes = [jax.ShapeDtypeStruct((batch,) + t, F32) for t in tails]
        st_specs = [pl.BlockSpec((nb,) + t, functools.partial(lambda b, ci, z: (b,) + z, z=zeros(t))) for t in tails]
    else:
        depth, layer, prev = stacked
        prev = list(prev)
        st_shapes = [jax.ShapeDtypeStruct((depth, batch) + t, F32) for t in tails]
        st_specs = [pl.BlockSpec((None, nb) + t, functools.partial(lambda b, ci, z: (layer, b) + z, z=zeros(t)))
                    for t in tails]
    tok_idx = lambda b, ci: (row_block0 + b * nc + ci, 0)
    aliased = [o_prev, *prev]
    in_specs = [ANY_SPEC] * len(aliased) + [pl.BlockSpec((nb * c, cols), tok_idx)]
    out_shape = (jax.ShapeDtypeStruct(o_prev.shape, o_prev.dtype), *st_shapes)
    out_specs = (pl.BlockSpec((nb * c, BR_W), tok_idx), *st_specs)
    return in_specs, out_shape, out_specs, {i: i for i in range(len(aliased))}, aliased


def _rwkv(pa, shift0, wkv0, layer, prm, *, o_prev, batch, seq, c, nb, ns, row_block0, stacked):
    nc = seq // c
    assert batch % nb == 0 and nb % ns == 0
    rw_pad, hm, m_strict, m_incl = _rwkv_masks(c, ns)
    rows = ns * RW_GROUP * c
    tri = jnp.asarray(_block_masks(nb, c)[0]).astype(BF16)
    io_specs, out_shape, out_specs, aliases, aliased = _mixer_io(
        RW_COLS, [(RW_HEADS, RW_HEAD, RW_HEAD), (1, RW_COLS)], o_prev=o_prev, batch=batch, c=c, nb=nb, nc=nc,
        row_block0=row_block0, stacked=stacked)
    kern = functools.partial(_rwkv_kernel, c=c, nb=nb, ns=ns, n_chunks=nc, rw_pad=rw_pad, n_alias=len(aliased))
    vec = lambda n: pl.BlockSpec((1, n), lambda b, ci: (0, 0))
    mat = lambda r_, n: pl.BlockSpec((r_, n), lambda b, ci: (0, 0))
    prm = tuple(prm) + (hm, m_strict, m_incl, tri)
    return pl.pallas_call(
        kern,
        out_shape=out_shape,
        grid=(batch // nb, nc),
        input_output_aliases=aliases,
        in_specs=[*io_specs,
                  pl.BlockSpec((None, nb, 1, RW_COLS), lambda b, ci: (layer, b, 0, 0)),
                  pl.BlockSpec((None, nb, RW_HEADS, RW_HEAD, RW_HEAD), lambda b, ci: (layer, b, 0, 0, 0)),
                  vec(RW_COLS), vec(512), mat(128, 512), vec(512), mat(128, 512), mat(128, 512),
                  vec(512), vec(512), vec(512), vec(512), vec(512), mat(512, 512),
                  mat(RW_GROUP, RW_GW), mat(rows, 2 * rw_pad), mat(rows, 2 * rw_pad), mat(nb * c, nb * c)],
        out_specs=out_specs,
        scratch_shapes=[pltpu.VMEM((nb, SUBLANES + c, RW_COLS), F32),
                        pltpu.VMEM((nb, RW_HEADS // RW_GROUP, RW_GW, RW_GW), F32)],
        compiler_params=_cparams(2),
        name="rwkv7",
    )(*aliased, pa, shift0, wkv0, *prm)


def _ret_kernel(*refs, c, nb, n_alias):
    refs = refs[n_alias:]
    tok_refs = refs[:1]
    cos_ref, sin_ref, mask_ref, s0_ref, o_ref, s_ref = refs[1:]
    ci = pl.program_id(1)

    @pl.when(ci == 0)
    def _init():
        s_ref[...] = s0_ref[...]

    rows = nb * c
    width = RET_HEADS * RET_DK
    half = RET_DK // 2
    cos = jnp.concatenate([cos_ref[...]] * nb, axis=0)
    sin = jnp.concatenate([sin_ref[...]] * nb, axis=0)
    first_half = (_iota((rows, width), 1) & half) == 0

    def rot(x):
        partner = jnp.where(first_half, pltpu.roll(x, width - half, 1), pltpu.roll(x, half, 1))
        return x * cos + partner * sin

    pb = jnp.concatenate([t[...] for t in tok_refs], axis=0)
    q = rot(pb[:, 0:width])
    k = rot(pb[:, width:2 * width]) * (RET_DK ** -0.5)
    v = pb[:, 2 * width:2 * width + BR_W]
    g_b = pb[:, 2 * width + BR_W:2 * width + 2 * BR_W]

    mask = mask_ref[...]
    causal = mask > 0.5
    dist = jnp.where(causal, (_iota((rows, rows), 0) - _iota((rows, rows), 1)).astype(F32), 0.0)
    t1 = jnp.sum(mask, axis=-1, keepdims=True)
    for h in range(RET_HEADS):
        log_gamma = math.log1p(-(2.0 ** (-5.0 - h)))
        dmat = jnp.where(causal, jnp.exp(dist * log_gamma), 0.0)
        q_h = q[:, h * RET_DK:(h + 1) * RET_DK]
        k_h = k[:, h * RET_DK:(h + 1) * RET_DK]
        v_h = v[:, h * RET_DV:(h + 1) * RET_DV]
        k_dec = k_h * jnp.exp((c - t1) * log_gamma)
        from_state = []
        for j in range(nb):
            js = slice(j * c, (j + 1) * c)
            s_h = s_ref[j, h]
            from_state.append(_mm(q_h[js], s_h))
            s_ref[j, h] = math.exp(c * log_gamma) * s_h + _mm_tn(k_dec[js], v_h[js])
        o_h = _mm(_mm_nt(q_h, k_h) * dmat, v_h) + jnp.exp(t1 * log_gamma) * jnp.concatenate(from_state, axis=0)
        o_n = o_h * lax.rsqrt(jnp.mean(o_h * o_h, axis=-1, keepdims=True) + NORM_EPS)
        o_ref[:, h * RET_DV:(h + 1) * RET_DV] = (o_n * _silu(g_b[:, h * RET_DV:(h + 1) * RET_DV])).astype(o_ref.dtype)


def _ret(pb, cos, sin, ret0, layer, *, o_prev, batch, seq, c, nb, row_block0, stacked):
    nc = seq // c
    assert batch % nb == 0
    width = RET_HEADS * RET_DK
    mask = jnp.asarray(_block_masks(nb, c)[0])
    io_specs, out_shape, out_specs, aliases, aliased = _mixer_io(
        RET_COLS, [(RET_HEADS, RET_DK, RET_DV)], o_prev=o_prev, batch=batch, c=c, nb=nb, nc=nc,
        row_block0=row_block0, stacked=stacked)
    return pl.pallas_call(
        functools.partial(_ret_kernel, c=c, nb=nb, n_alias=len(aliased)),
        out_shape=out_shape,
        grid=(batch // nb, nc),
        input_output_aliases=aliases,
        in_specs=[*io_specs,
                  pl.BlockSpec((c, width), lambda b, ci: (ci, 0)),
                  pl.BlockSpec((c, width), lambda b, ci: (ci, 0)),
                  pl.BlockSpec((nb * c, nb * c), lambda b, ci: (0, 0)),
                  pl.BlockSpec((None, nb, RET_HEADS, RET_DK, RET_DV), lambda b, ci: (layer, b, 0, 0, 0))],
        out_specs=out_specs,
        compiler_params=_cparams(2),
        name="retention",
    )(*aliased, pb, cos, sin, mask, ret0)


def _hgrn_kernel(*refs, c, m, nb, layer, n_chunks, n_alias):
    refs = refs[n_alias:]
    tok_refs = refs[:1]
    lbp_ref, ng_ref, tri_ref, s0_ref, o_ref, s_ref, st_scr = refs[1:]
    ci = pl.program_id(1)

    @pl.when(ci == 0)
    def _init():
        for j in range(nb):
            for h in range(HG_HEADS):
                st_scr[j, h] = s0_ref[j, h].T

    lbp = lbp_ref[...]
    e = jnp.exp(lbp - jnp.max(lbp, axis=0, keepdims=True))
    soft = e / jnp.sum(e, axis=0, keepdims=True)
    lb = jnp.zeros((1, BR_W), F32)
    for j in range(1, layer + 1):
        lb = lb + soft[j:j + 1, :]

    pc = jnp.concatenate([t[...] for t in tok_refs], axis=0)
    q_all = _silu(pc[:, 0:BR_W])
    f_in = pc[:, BR_W:2 * BR_W]
    v_all = pc[:, 2 * BR_W:3 * BR_W]
    g_c = pc[:, 3 * BR_W:4 * BR_W]
    log_f = jnp.log(lb + (1.0 - lb) * _sigmoid(f_in))
    k_all = (1.0 - lb) * _sigmoid(-f_in)
    cum_all = _mm_mask(tri_ref[...], log_f)

    col_m = _iota((m, m), 1)
    row_l = _iota((m, LANES), 0)
    outs = {}
    for j in range(nb):
        js = slice(j * c, (j + 1) * c)
        q, k, v, cum = q_all[js], k_all[js], v_all[js], cum_all[js]
        cum_last = cum[c - 1:c, :]
        q_e = q * jnp.exp(cum)
        k_e = k * jnp.exp(cum_last - cum)
        p_end = jnp.exp(cum_last)
        for h in range(HG_HEADS):
            sl = slice(h * HG_D, (h + 1) * HG_D)
            st = st_scr[j, h]
            o_rows = []
            for i in range(c // m):
                r0 = i * m
                q_i = q[r0:r0 + m, sl]
                c_i = cum[r0:r0 + m, sl]
                k_i = k[r0:r0 + m, sl]
                diag = jnp.zeros((m, m), F32)
                for s in range(m):
                    keep = row_l >= s
                    decay = jnp.exp(jnp.where(keep, c_i - c_i[s:s + 1, :], 0.0))
                    col = jnp.sum(jnp.where(keep, q_i * k_i[s:s + 1, :] * decay, 0.0), axis=-1, keepdims=True)
                    diag = jnp.where(col_m == s, col, diag)
                o_i = _mm(diag, v[r0:r0 + m, sl])
                if i > 0:
                    c_ref = cum[r0 - 1:r0, sl]
                    q_s = q_i * jnp.exp(c_i - c_ref)
                    k_s = k[0:r0, sl] * jnp.exp(c_ref - cum[0:r0, sl])
                    o_i = o_i + _mm(_mm_nt(q_s, k_s), v[0:r0, sl])
                o_rows.append(o_i)
            outs[(j, h)] = _mm_nt(q_e[:, sl], st) + jnp.concatenate(o_rows, axis=0)
            st_scr[j, h] = st * p_end[:, sl] + _mm_tn(v[:, sl], k_e[:, sl])
    for h in range(HG_HEADS):
        sl = slice(h * HG_D, (h + 1) * HG_D)
        o_h = jnp.concatenate([outs[(j, h)] for j in range(nb)], axis=0)
        o_n = o_h * lax.rsqrt(jnp.mean(o_h * o_h, axis=-1, keepdims=True) + NORM_EPS) * ng_ref[:, sl]
        o_ref[:, sl] = (o_n * _silu(g_c[:, sl])).astype(o_ref.dtype)

    @pl.when(ci == n_chunks - 1)
    def _fin():
        for j in range(nb):
            for h in range(HG_HEADS):
                s_ref[j, h] = st_scr[j, h].T


def _hgrn(pc, lbp, ng, hg0, state_layer, layer, *, o_prev, batch, seq, c, nb, row_block0, stacked):
    nc = seq // c
    assert batch % nb == 0
    m = 16 if c % 16 == 0 else SUBLANES
    depth = lbp.shape[0]
    tri = jnp.asarray(_block_masks(nb, c)[0]).astype(BF16)
    io_specs, out_shape, out_specs, aliases, aliased = _mixer_io(
        HG_COLS, [(HG_HEADS, HG_D, HG_D)], o_prev=o_prev, batch=batch, c=c, nb=nb, nc=nc,
        row_block0=row_block0, stacked=stacked)
    return pl.pallas_call(
        functools.partial(_hgrn_kernel, c=c, m=m, nb=nb, layer=layer, n_chunks=nc, n_alias=len(aliased)),
        out_shape=out_shape,
        grid=(batch // nb, nc),
        input_output_aliases=aliases,
        in_specs=[*io_specs,
                  pl.BlockSpec((depth, BR_W), lambda b, ci: (0, 0)),
                  pl.BlockSpec((1, BR_W), lambda b, ci: (0, 0)),
                  pl.BlockSpec((nb * c, nb * c), lambda b, ci: (0, 0)),
                  pl.BlockSpec((None, nb, HG_HEADS, HG_D, HG_D), lambda b, ci: (state_layer, b, 0, 0, 0))],
        out_specs=out_specs,
        scratch_shapes=[pltpu.VMEM((nb, HG_HEADS, HG_D, HG_D), F32)],
        compiler_params=_cparams(2),
        name="hgrn2",
    )(*aliased, pc, lbp, ng, tri, hg0)


def _gdn_kernel(*refs, c, nb, ns, n_alias):
    refs = refs[n_alias:]
    tok_refs = refs[:1]
    (cs0_ref, s0_ref, cw_ref, alog_ref, dtb_ref, ng_ref, mc_ref, mst_ref, tri_ref,
     o_ref, s_ref, cs_ref, ext_scr) = refs[1:]
    ci = pl.program_id(1)
    keep = GD_CONV - 1

    @pl.when(ci == 0)
    def _init():
        for j in range(nb):
            ext_scr[j, SUBLANES - keep:SUBLANES, :] = cs0_ref[j]
        s_ref[...] = s0_ref[...]

    convs = []
    for j in range(nb):
        x = tok_refs[0][j * c:(j + 1) * c, 0:GD_QKV]
        ext_scr[j, SUBLANES:SUBLANES + c, :] = x
        conv = ext_scr[j, SUBLANES - keep:SUBLANES - keep + c, :] * cw_ref[0:1, :]
        for i in range(1, keep):
            conv = conv + ext_scr[j, SUBLANES - keep + i:SUBLANES - keep + i + c, :] * cw_ref[i:i + 1, :]
        convs.append(conv + x * cw_ref[keep:keep + 1, :])
        tail = ext_scr[j, SUBLANES + c - keep:SUBLANES + c, :]
        ext_scr[j, SUBLANES - keep:SUBLANES, :] = tail
        cs_ref[j] = tail

    act = _silu(jnp.concatenate(convs, axis=0))
    q = act[:, 0:BR_W]
    k = act[:, BR_W:2 * BR_W]
    v = act[:, 2 * BR_W:3 * BR_W]
    gate_ba = jnp.concatenate([t[:, GD_QKV:GD_COLS_PAD] for t in tok_refs], axis=0)
    g_d = gate_ba[:, 0:BR_W]
    ba = gate_ba[:, BR_W:BR_W + LANES]
    beta_all = _sigmoid(ba)
    g_all = -jnp.exp(alog_ref[...]) * _softplus(ba + dtb_ref[...])

    cum_all = _mm_mask(tri_ref[...], g_all)

    ur = ns * c
    rows = GD_HEADS * ur
    stack = lambda parts: jnp.concatenate(parts, axis=0)
    l2n = lambda z: z * lax.rsqrt(jnp.sum(z * z, axis=-1, keepdims=True) + 1e-6)
    head = lambda z, h: z[:, h * GD_D:(h + 1) * GD_D]
    qn = jnp.concatenate([l2n(head(q, h)) for h in range(GD_HEADS)], axis=1) * (GD_D ** -0.5)
    kn = jnp.concatenate([l2n(head(k, h)) for h in range(GD_HEADS)], axis=1)
    units = range(nb // ns)
    heads = range(GD_HEADS)
    stack_u = lambda z, un: stack([head(z[un * ur:(un + 1) * ur], h) for h in heads])
    col_u = lambda z, lane0, un: stack([z[un * ur:(un + 1) * ur, lane0 + h:lane0 + h + 1] for h in heads])
    q4 = [stack_u(qn, un) for un in units]
    k4 = [stack_u(kn, un) for un in units]
    v4 = [stack_u(v, un) for un in units]
    beta = [col_u(beta_all, 0, un) for un in units]
    cum = [col_u(cum_all, GD_HEADS, un) for un in units]
    causal = mc_ref[...] > 0.5
    m_strict = mst_ref[...]
    eye = _iota((rows, rows), 0) == _iota((rows, rows), 1)
    ones = jnp.ones((rows, rows), BF16)
    cum_t = [jnp.broadcast_to(cm, (rows, rows)) for cm in cum]
    cum_s = [_mm_mask(ones, jnp.where(eye, ct, 0.0)) for ct in cum_t]
    dmat = [jnp.where(causal, jnp.exp(jnp.where(causal, ct - cs_, 0.0)), 0.0) for ct, cs_ in zip(cum_t, cum_s)]
    a_mat = [beta[un] * _mm_nt(k4[un], k4[un]) * (dmat[un] * m_strict) for un in units]
    inv = _unit_lower_inverse([-a for a in a_mat], c)
    e_cum = [jnp.exp(cm) for cm in cum]
    sol = [_mm(inv[un], jnp.concatenate([(beta[un] * e_cum[un]) * k4[un], beta[un] * v4[un]], axis=1)) for un in units]
    blocks = [(h, i) for h in heads for i in range(ns)]
    rows_of = lambda h, i: slice((h * ns + i) * c, (h * ns + i + 1) * c)
    qw = [[_mm(jnp.concatenate([q4[un][rows_of(h, i)], sol[un][rows_of(h, i), 0:GD_D]], axis=0), s_ref[un * ns + i, h])
           for h, i in blocks] for un in units]
    delta = [sol[un][:, GD_D:2 * GD_D] - stack([z[c:2 * c] for z in qw[un]]) for un in units]
    o4 = [e_cum[un] * stack([z[0:c] for z in qw[un]]) + _mm(_mm_nt(q4[un], k4[un]) * dmat[un], delta[un])
          for un in units]
    for un in units:
        for h, i in blocks:
            hs = rows_of(h, i)
            cum_h = cum[un][hs]
            cum_last = cum_h[c - 1:c, :]
            s_ref[un * ns + i, h] = (jnp.exp(cum_last) * s_ref[un * ns + i, h]
                                     + _mm_tn(k4[un][hs] * jnp.exp(cum_last - cum_h), delta[un][hs]))
    for h in heads:
        sl = slice(h * GD_D, (h + 1) * GD_D)
        o_h = stack([o4[un][h * ur:(h + 1) * ur] for un in units])
        o_n = o_h * lax.rsqrt(jnp.mean(o_h * o_h, axis=-1, keepdims=True) + NORM_EPS) * ng_ref[:, sl]
        o_ref[:, sl] = (o_n * _silu(g_d[:, sl])).astype(o_ref.dtype)


def _gdn(pd, conv0, gd0, layer, cw, alog, dtb, ng, *, o_prev, batch, seq, c, nb, ns, row_block0, stacked):
    nc = seq // c
    assert batch % nb == 0 and nb % ns == 0
    keep = GD_CONV - 1
    rows = GD_HEADS * ns * c
    m_causal, m_strict = (jnp.asarray(m) for m in _block_masks(GD_HEADS * ns, c))
    tri = jnp.asarray(_block_masks(nb, c)[0]).astype(BF16)
    full = lambda *shape: pl.BlockSpec(shape, lambda b, ci: (0,) * len(shape))
    io_specs, out_shape, out_specs, aliases, aliased = _mixer_io(
        GD_COLS_PAD, [(GD_HEADS, GD_D, GD_D), (keep, GD_QKV)], o_prev=o_prev, batch=batch, c=c, nb=nb, nc=nc,
        row_block0=row_block0, stacked=stacked)
    return pl.pallas_call(
        functools.partial(_gdn_kernel, c=c, nb=nb, ns=ns, n_alias=len(aliased)),
        out_shape=out_shape,
        grid=(batch // nb, nc),
        input_output_aliases=aliases,
        in_specs=[*io_specs,
                  pl.BlockSpec((None, nb, keep, GD_QKV), lambda b, ci: (layer, b, 0, 0)),
                  pl.BlockSpec((None, nb, GD_HEADS, GD_D, GD_D), lambda b, ci: (layer, b, 0, 0, 0)),
                  full(GD_CONV, GD_QKV), full(1, LANES), full(1, LANES), full(1, BR_W),
                  full(rows, rows), full(rows, rows), full(nb * c, nb * c)],
        out_specs=out_specs,
        scratch_shapes=[pltpu.VMEM((nb, SUBLANES + c, GD_QKV), F32)],
        compiler_params=_cparams(2),
        name="gdn",
    )(*aliased, pd, conv0, gd0, cw, alog, dtb, ng, m_causal, m_strict, tri)


def _pick_chunk(seq):
    for c in (64, 56, 48, 40, 32, 24, 16, 8):
        if seq % c == 0:
            return c
    raise ValueError(f"sequence length {seq} is not a multiple of {SUBLANES}")


def _pick_tile(n, candidates):
    for t in candidates:
        if n % t == 0:
            return t
    raise ValueError(f"no tile in {candidates} divides {n}")


def _rope_tables(pos):
    half = RET_DK // 2
    inv = ROPE_BASE ** (-jnp.arange(half, dtype=F32) / half)
    ang = pos.astype(F32)[:, None] * inv[None, :]
    cos, sin = jnp.cos(ang), jnp.sin(ang)
    cos_t = jnp.tile(jnp.concatenate([cos, cos], axis=1), (1, RET_HEADS))
    sin_t = jnp.tile(jnp.concatenate([-sin, sin], axis=1), (1, RET_HEADS))
    return cos_t, sin_t


def kernel(x_prompt, x_sample, state_rwkv_wkv, state_rwkv_shift, state_ret, state_hgrn, state_gdn, state_gdn_conv, meta_tokens, norm_mix, w_in, rw_mu, rw_w0, rw_w2, rw_a0, rw_a2, rw_g2, rw_kk, rw_ka, rw_rk, rw_ln_g, rw_ln_b, hg_lb, hg_norm_g, gd_conv, gd_a_log, gd_dt_bias, gd_norm_g, w_branch, w_out, norm_ffn, w_up, w_down, norm_final):
    depth = norm_mix.shape[0]
    bp, tq, d = x_prompt.shape
    tp = tq + N_META
    bs, ts, _ = x_sample.shape
    n_p, n_s = bp * tp, bs * ts
    n_tok = n_p + n_s
    cp, cs = _pick_chunk(tp), _pick_chunk(ts)
    tm = _pick_tile(n_tok, (464, 512, 256, 232, 128, 64, 32, 16, 8))
    tm_in = _pick_tile(n_tok, (928, 464, 512, 256, 232, 128, 64, 32, 16, 8))
    nb_s = _pick_tile(bs, (8, 4, 2, 1))
    nb_p = _pick_tile(bp, (4, 2, 1))
    ncp, gp = tp // cp, bp // nb_p
    assert cs == ts and n_p % (nb_s * ts) == 0

    meta = jnp.broadcast_to(meta_tokens.astype(F32)[None], (bp, N_META, d))
    xp = jnp.concatenate([meta, x_prompt], axis=1).reshape(gp, nb_p, ncp, cp, d)
    x = jnp.concatenate([xp.transpose(0, 2, 1, 3, 4).reshape(n_p, d), x_sample.reshape(n_s, d)], axis=0)

    cos_p, sin_p = _rope_tables(jnp.arange(tp, dtype=jnp.int32))
    cos_s, sin_s = _rope_tables(PAST_LEN + jnp.arange(ts, dtype=jnp.int32))

    head_of = jnp.arange(BR_W) // RW_HEAD
    bd = (head_of[:, None] == head_of[None, :]).astype(BF16)

    zeros = lambda *s: jnp.zeros((1, bp) + s, F32)
    z_wkv, z_shift = zeros(RW_HEADS, RW_HEAD, RW_HEAD), zeros(1, RW_COLS)
    z_ret, z_hg, z_gd = zeros(RET_HEADS, RET_DK, RET_DV), zeros(HG_HEADS, HG_D, HG_D), zeros(GD_HEADS, GD_D, GD_D)
    z_conv = zeros(GD_CONV - 1, GD_QKV)
    shift_s = state_rwkv_shift.reshape(depth, bs, 1, RW_COLS)

    o_a, o_b, o_c = RW_COLS, RW_COLS + RET_COLS, RW_COLS + RET_COLS + HG_COLS
    o_ba = o_c + GD_QKV
    o_g = o_ba + 2 * GD_HEADS + BR_W
    w_a = w_in[:, :, 0:o_a].astype(BF16)
    w_b = w_in[:, :, o_a:o_b].astype(BF16)
    w_c = w_in[:, :, o_b:o_c].astype(BF16)
    w_d = jnp.concatenate([w_in[:, :, o_c:o_ba], w_in[:, :, o_ba + 2 * GD_HEADS:o_g], w_in[:, :, o_ba:o_ba + 2 * GD_HEADS],
                           jnp.zeros((depth, d, LANES - 2 * GD_HEADS), F32)], axis=2).astype(BF16)
    w_g = w_in[:, :, o_g:].astype(BF16)
    wb_all, wo_all = w_branch.astype(BF16), w_out.astype(BF16)
    wu_all, wd_all = w_up.astype(BF16), w_down.astype(BF16)

    new_p = [[] for _ in range(6)]
    s_stacks = [jnp.zeros((depth, bs) + tail, F32) for tail in (
        (RW_HEADS, RW_HEAD, RW_HEAD), (1, RW_COLS), (RET_HEADS, RET_DK, RET_DV), (HG_HEADS, HG_D, HG_D),
        (GD_HEADS, GD_D, GD_D), (GD_CONV - 1, GD_QKV))]
    h = _rmsnorm(x, norm_mix[0], tm, BF16)
    for l in range(depth):
        pa = _matmul(h, w_a, l, tm_in, 896, "in_proj_a")
        pb = _matmul(h, w_b, l, tm_in, 768, "in_proj_b")
        pc = _matmul(h, w_c, l, tm_in, 1024, "in_proj_c")
        pd = _matmul(h, w_d, l, tm_in, GD_COLS_PAD, "in_proj_d")
        gates = _matmul(h, w_g, l, tm_in, 1024, "in_proj_g", gate=True)

        row1 = lambda a: a.reshape(1, -1)
        pad_rows = lambda a, top: jnp.concatenate(
            [jnp.zeros((top, BR_W), F32), a, jnp.zeros((LANES - top - a.shape[0], BR_W), F32)], axis=0).astype(BF16)
        rw_prm = (row1(rw_mu[l]), row1(rw_w0[l]), pad_rows(rw_w2[l], 0), row1(rw_a0[l]), pad_rows(rw_a2[l], 64),
                  rw_g2[l].astype(BF16), row1(rw_kk[l]), row1(rw_ka[l]), row1(rw_rk[l]), row1(rw_ln_g[l]),
                  row1(rw_ln_b[l]), bd)
        lane_pad = lambda a: jnp.concatenate(
            [jnp.zeros((GD_HEADS,), F32), a, jnp.zeros((LANES - 2 * GD_HEADS,), F32)]).reshape(1, LANES)
        gd_prm = (gd_conv[l], lane_pad(gd_a_log[l]), lane_pad(gd_dt_bias[l]), row1(gd_norm_g[l]))
        hg_g = row1(hg_norm_g[l])

        oa = ob = oc = od = jnp.zeros((n_tok, BR_W), BF16)
        for grp, (batch, seq, c, nb, ns, rb0, sts, lay, cos, sin) in enumerate((
                (bp, tp, cp, nb_p, 1, 0, (z_wkv, z_shift, z_ret, z_hg, z_gd, z_conv), 0, cos_p, sin_p),
                (bs, ts, cs, nb_s, nb_s, n_p // (nb_s * ts),
                 (state_rwkv_wkv, shift_s, state_ret, state_hgrn, state_gdn, state_gdn_conv), l, cos_s, sin_s))):
            kw = dict(batch=batch, seq=seq, c=c, nb=nb, row_block0=rb0)
            stk = lambda *prev: None if grp == 0 else (depth, l, prev)
            oa, wkv, shift = _rwkv(pa, sts[1], sts[0], lay, rw_prm, o_prev=oa, ns=ns, stacked=stk(*s_stacks[0:2]), **kw)
            ob, ret = _ret(pb, cos, sin, sts[2], lay, o_prev=ob, stacked=stk(*s_stacks[2:3]), **kw)
            oc, hg = _hgrn(pc, hg_lb, hg_g, sts[3], lay, l, o_prev=oc, stacked=stk(*s_stacks[3:4]), **kw)
            od, gd, conv = _gdn(pd, sts[5], sts[4], lay, *gd_prm, o_prev=od, ns=ns, stacked=stk(*s_stacks[4:6]), **kw)
            if grp == 0:
                for i, st in enumerate((wkv, shift.reshape(batch, RW_COLS), ret, hg, gd, conv)):
                    new_p[i].append(st)
            else:
                s_stacks = [wkv, shift, ret, hg, gd, conv]
        x = _merge(x, (oa, ob, oc, od), gates, wb_all, wo_all, l, tm_in, 256)
        last = l == depth - 1
        res = _ffn(x, norm_ffn[l], norm_final if last else norm_mix[l + 1], wu_all, wd_all, l, tm, 512, last)
        if last:
            y = res
        else:
            x, h = res

    y_prompt = y[:n_p].reshape(gp, ncp, nb_p, cp, d).transpose(0, 2, 1, 3, 4).reshape(bp, tp, d)[:, N_META:]
    y_sample = y[n_p:].reshape(bs, ts, d)
    p_states = tuple(jnp.stack(z) for z in new_p)
    s_wkv, s_shift, s_ret, s_hg, s_gd, s_conv = s_stacks
    return (y_prompt, y_sample) + p_states + (s_wkv, s_shift.reshape(depth, bs, RW_COLS), s_ret, s_hg, s_gd, s_conv)
```

```python
import functools
import math

import jax
import jax.numpy as jnp
import numpy as np
from jax import lax
from jax.experimental import pallas as pl
from jax.experimental.pallas import tpu as pltpu

F32 = jnp.float32
BF16 = jnp.bfloat16

N_META = 16
PAST_LEN = 16384
NORM_EPS = 1e-6
RW_LN_EPS = 64e-5
ROPE_BASE = 10000.0

D_MODEL = 2048
BR_W = 512
RW_HEADS, RW_HEAD = 8, 64
RW_COLS = 1792
RW_GROUP = 4
RW_GW = RW_GROUP * RW_HEAD
RET_HEADS, RET_DK, RET_DV = 4, 64, 128
RET_COLS = 1536
HG_HEADS, HG_D = 4, 128
HG_COLS = 2048
GD_HEADS, GD_D = 4, 128
GD_CONV = 4
GD_QKV = 1536
GD_COLS_PAD = 2176

LANES = 128
SUBLANES = 8
VMEM_LIMIT = 56 * 1024 * 1024


def _cparams(n_axes):
    return pltpu.CompilerParams(dimension_semantics=("arbitrary",) * n_axes, vmem_limit_bytes=VMEM_LIMIT)


def _dg(a, b, ca, cb):
    return lax.dot_general(a, b, (((ca,), (cb,)), ((), ())), preferred_element_type=F32)


def _mm(a, b):
    return _dg(a.astype(BF16), b.astype(BF16), 1, 0)


def _mm_nt(a, b):
    return _dg(a.astype(BF16), b.astype(BF16), 1, 1)


def _mm_tn(a, b):
    return _dg(a.astype(BF16), b.astype(BF16), 0, 0)


def _split(x, n):
    parts, r = [], x
    for _ in range(n):
        p = r.astype(BF16)
        parts.append(p)
        r = r - p.astype(F32)
    return parts


def _mm_mask(m01, x):
    x0, x1, x2 = _split(x, 3)
    return _dg(m01, x0, 1, 0) + (_dg(m01, x1, 1, 0) + _dg(m01, x2, 1, 0))


def _iota(shape, dim):
    return lax.broadcasted_iota(jnp.int32, shape, dim)


def _softplus(x):
    return jnp.maximum(x, 0.0) + jnp.log(1.0 + jnp.exp(-jnp.abs(x)))


def _sigmoid(x):
    return jax.nn.sigmoid(x)


def _silu(x):
    return x * jax.nn.sigmoid(x)


def _unit_lower_inverse(lows, c):
    shape = lows[0].shape
    eye = (_iota(shape, 0) == _iota(shape, 1)).astype(F32)
    invs = [eye + low for low in lows]
    powers = list(lows)
    covered = 1
    while covered < c - 1:
        powers = [_mm(p, p) for p in powers]
        invs = [inv + _mm(inv, p) for inv, p in zip(invs, powers)]
        covered = 2 * covered + 1
    return invs


def _rmsnorm_kernel(x_ref, g_ref, o_ref):
    x = x_ref[...]
    y = x * lax.rsqrt(jnp.mean(x * x, axis=-1, keepdims=True) + NORM_EPS)
    o_ref[...] = (y * g_ref[...]).astype(o_ref.dtype)


def _rmsnorm(x, g, tm, out_dtype):
    n, d = x.shape
    return pl.pallas_call(
        _rmsnorm_kernel,
        out_shape=jax.ShapeDtypeStruct((n, d), out_dtype),
        grid=(n // tm,),
        in_specs=[pl.BlockSpec((tm, d), lambda i: (i, 0)), pl.BlockSpec((1, d), lambda i: (0, 0))],
        out_specs=pl.BlockSpec((tm, d), lambda i: (i, 0)),
        compiler_params=_cparams(1),
        name="rmsnorm",
    )(x, g.reshape(1, d))


def _matmul_kernel(x_ref, w_ref, o_ref):
    o_ref[...] = _dg(x_ref[...], w_ref[...], 1, 0)


def _matmul_sigmoid_kernel(x_ref, w_ref, o_ref):
    o_ref[...] = _sigmoid(_dg(x_ref[...], w_ref[...], 1, 0)).astype(o_ref.dtype)


def _matmul(x, w, layer, tm, tn, name, gate=False):
    n, k = x.shape
    m = w.shape[2]
    return pl.pallas_call(
        _matmul_sigmoid_kernel if gate else _matmul_kernel,
        out_shape=jax.ShapeDtypeStruct((n, m), BF16 if gate else F32),
        grid=(n // tm, m // tn),
        in_specs=[pl.BlockSpec((tm, k), lambda i, j: (i, 0)), pl.BlockSpec((None, k, tn), lambda i, j: (layer, 0, j))],
        out_specs=pl.BlockSpec((tm, tn), lambda i, j: (i, j)),
        compiler_params=_cparams(2),
        name=name,
    )(x, w)


IN_A, IN_B, IN_C = RW_COLS, RW_COLS + RET_COLS, RW_COLS + RET_COLS + HG_COLS
IN_BA = IN_C + GD_QKV
IN_G = IN_BA + 2 * GD_HEADS + BR_W
GATE_COLS = 4 * D_MODEL


def _w_in_prep_kernel(w_ref, a_ref, b_ref, c_ref, d_ref, g_ref):
    cast = lambda lo, hi: w_ref[:, lo:hi].astype(BF16)
    a_ref[...] = cast(0, IN_A)
    b_ref[...] = cast(IN_A, IN_B)
    c_ref[...] = cast(IN_B, IN_C)
    d_ref[:, 0:GD_QKV] = cast(IN_C, IN_BA)
    d_ref[:, GD_QKV:GD_QKV + BR_W] = cast(IN_BA + 2 * GD_HEADS, IN_G)
    ba = jnp.concatenate([w_ref[:, IN_BA:IN_BA + 2 * GD_HEADS],
                          jnp.zeros((w_ref.shape[0], LANES - 2 * GD_HEADS), F32)], axis=1)
    d_ref[:, GD_QKV + BR_W:GD_COLS_PAD] = ba.astype(BF16)
    step = 8 * LANES
    for lo in range(0, GATE_COLS, step):
        g_ref[:, lo:lo + step] = cast(IN_G + lo, IN_G + lo + step)


def _w_in_prep(w_in, tk):
    depth, k, n = w_in.shape
    out = lambda cols: (jax.ShapeDtypeStruct((depth, k, cols), BF16), pl.BlockSpec((None, tk, cols), lambda l, i: (l, i, 0)))
    shapes, specs = zip(*(out(cols) for cols in (RW_COLS, RET_COLS, HG_COLS, GD_COLS_PAD, GATE_COLS)))
    return pl.pallas_call(
        _w_in_prep_kernel,
        out_shape=shapes,
        grid=(depth, k // tk),
        in_specs=[pl.BlockSpec((None, tk, n), lambda l, i: (l, i, 0))],
        out_specs=specs,
        compiler_params=_cparams(2),
        name="w_in_prep",
    )(w_in)


def _merge_kernel(x_ref, oa_ref, ob_ref, oc_ref, od_ref, g0_ref, g1_ref, g2_ref, g3_ref, wb_ref, wo_ref, y_ref):
    j = pl.program_id(1)

    @pl.when(j == 0)
    def _init():
        y_ref[...] = x_ref[...]

    merged = None
    for n, (o_ref, g_ref) in enumerate(((oa_ref, g0_ref), (ob_ref, g1_ref), (oc_ref, g2_ref), (od_ref, g3_ref))):
        term = g_ref[...].astype(F32) * _dg(o_ref[...], wb_ref[n], 1, 0)
        merged = term if merged is None else merged + term
    y_ref[...] += _dg(merged.astype(BF16), wo_ref[...], 1, 0)


def _merge(x, outs, pg, wb, wo, layer, tm, tj):
    n, d = x.shape
    nj = d // tj
    o_spec = pl.BlockSpec((tm, BR_W), lambda i, j: (i, 0))
    g_specs = [pl.BlockSpec((tm, tj), functools.partial(lambda i, j, nb: (i, nb * nj + j), nb=nb)) for nb in range(4)]
    return pl.pallas_call(
        _merge_kernel,
        out_shape=jax.ShapeDtypeStruct((n, d), F32),
        grid=(n // tm, nj),
        in_specs=[pl.BlockSpec((tm, d), lambda i, j: (i, 0)), o_spec, o_spec, o_spec, o_spec, *g_specs,
                  pl.BlockSpec((None, 4, BR_W, tj), lambda i, j: (layer, 0, 0, j)),
                  pl.BlockSpec((None, tj, d), lambda i, j: (layer, j, 0))],
        out_specs=pl.BlockSpec((tm, d), lambda i, j: (i, 0)),
        compiler_params=_cparams(2),
        name="merge",
    )(x, *outs, pg, pg, pg, pg, wb, wo)


def _rms(x, g):
    return x * lax.rsqrt(jnp.mean(x * x, axis=-1, keepdims=True) + NORM_EPS) * g


def _ffn_kernel(x_ref, g_ref, gn_ref, wu_ref, wg_ref, wd_ref, *out_and_scratch, last):
    *outs, h_ref = out_and_scratch
    y_ref = outs[0]
    j = pl.program_id(1)

    @pl.when(j == 0)
    def _init():
        x = x_ref[...]
        h_ref[...] = _rms(x, g_ref[...]).astype(BF16)
        y_ref[...] = x

    h = h_ref[...]
    up = _dg(h, wu_ref[...], 1, 0)
    gate = _dg(h, wg_ref[...], 1, 0)
    y_ref[...] += _dg((_silu(gate) * up).astype(BF16), wd_ref[...], 1, 0)

    @pl.when(j == pl.num_programs(1) - 1)
    def _fin():
        if last:
            y_ref[...] = _rms(y_ref[...], gn_ref[...])
        else:
            outs[1][...] = _rms(y_ref[...], gn_ref[...]).astype(BF16)


def _ffn(x, g, g_next, w_up, w_down, layer, tm, tf, last):
    n, d = x.shape
    ff = w_down.shape[1]
    nf = ff // tf
    row_spec = pl.BlockSpec((tm, d), lambda i, j: (i, 0))
    vec_spec = pl.BlockSpec((1, d), lambda i, j: (0, 0))
    if last:
        out_shape, out_specs = jax.ShapeDtypeStruct((n, d), F32), row_spec
    else:
        out_shape = (jax.ShapeDtypeStruct((n, d), F32), jax.ShapeDtypeStruct((n, d), BF16))
        out_specs = (row_spec, row_spec)
    return pl.pallas_call(
        functools.partial(_ffn_kernel, last=last),
        out_shape=out_shape,
        grid=(n // tm, nf),
        in_specs=[pl.BlockSpec((tm, d), lambda i, j: (i, 0), pipeline_mode=pl.Buffered(1)), vec_spec, vec_spec,
                  pl.BlockSpec((None, d, tf), lambda i, j: (layer, 0, j)),
                  pl.BlockSpec((None, d, tf), lambda i, j: (layer, 0, nf + j)),
                  pl.BlockSpec((None, tf, d), lambda i, j: (layer, j, 0))],
        out_specs=out_specs,
        scratch_shapes=[pltpu.VMEM((tm, d), BF16)],
        compiler_params=_cparams(2),
        name="ffn",
    )(x, g.reshape(1, d), g_next.reshape(1, d), w_up, w_up, w_down)


def _rwkv_kernel(*refs, c, nb, ns, n_chunks, rw_pad, n_alias):
    refs = refs[n_alias:]
    tok_refs = refs[:1]
    (sh0_ref, s0_ref, mu_ref, w0_ref, w2_ref, a0_ref, a2_ref, g2_ref, kkw_ref, ka_ref, rk_ref, lng_ref, lnb_ref,
     bd_ref, hm_ref, ms_ref, mi_ref, tri_ref, o_ref, s_ref, sh_ref, xs_scr, sbd_scr) = refs[1:]
    ci = pl.program_id(1)
    seqs = range(nb)
    head_block = lambda h: (h // RW_GROUP, slice((h % RW_GROUP) * RW_HEAD, (h % RW_GROUP + 1) * RW_HEAD))

    @pl.when(ci == 0)
    def _init():
        sbd_scr[...] = jnp.zeros_like(sbd_scr)
        for j in seqs:
            xs_scr[j, SUBLANES - 1:SUBLANES, :] = sh0_ref[j]
            for h in range(RW_HEADS):
                g, sl = head_block(h)
                sbd_scr[j, g, sl, sl] = s0_ref[j, h]

    pa = jnp.concatenate([t[...] for t in tok_refs], axis=0)
    prevs = []
    for j in seqs:
        pa_j = pa[j * c:(j + 1) * c, :]
        xs_scr[j, SUBLANES:SUBLANES + c, :] = pa_j
        prevs.append(xs_scr[j, SUBLANES - 1:SUBLANES - 1 + c, :])
        last = pa_j[c - 1:c, :]
        xs_scr[j, SUBLANES - 1:SUBLANES, :] = last
        sh_ref[j] = last
    prev = jnp.concatenate(prevs, axis=0)

    xm = pa + (prev - pa) * mu_ref[...]
    r = xm[:, 0:512]
    k0 = xm[:, 512:1024]
    v = xm[:, 1024:1536]
    wa = xm[:, 1536:1664]
    gl = xm[:, 1664:1792]
    w_log = -_softplus(-(w0_ref[...] + _mm(jnp.tanh(wa), w2_ref[...]))) - 0.5
    ld = -jnp.exp(w_log)
    asig = _sigmoid(a0_ref[...] + _mm(wa, a2_ref[...]))
    g_a = _mm(_sigmoid(gl), g2_ref[...])
    bd = bd_ref[...]
    kkr = k0 * kkw_ref[...]
    kkn = kkr * lax.rsqrt(_mm(kkr * kkr, bd) + 1e-6)
    k1 = k0 * (1.0 + (asig - 1.0) * ka_ref[...])
    a_vec = -kkn
    b_vec = kkn * asig

    cum = _mm_mask(tri_ref[...], ld)
    cum_ends = [cum[(j + 1) * c - 1:(j + 1) * c, :] for j in seqs]
    cum_last = jnp.concatenate([jnp.broadcast_to(e, (c, BR_W)) for e in cum_ends], axis=0)
    a_t = a_vec * jnp.exp(cum - ld)
    r_t = r * jnp.exp(cum)
    inv_p = jnp.exp(-cum)
    b_t = b_vec * inv_p
    k_t = k1 * inv_p
    to_end = jnp.exp(cum_last - cum)
    b_e = b_vec * to_end
    k_e = k1 * to_end

    seq_rows = RW_GROUP * c
    rows = ns * seq_rows
    pad = [jnp.zeros((rw_pad - rows, RW_GW), F32)] if rw_pad > rows else []
    m_strict = ms_ref[...]
    m_incl = mi_ref[...]
    groups = range(RW_HEADS // RW_GROUP)
    units = [(s0, g) for s0 in range(0, nb, ns) for g in groups]
    lanes = lambda g: slice(g * RW_GW, (g + 1) * RW_GW)
    of_seq = lambda x, i: x[i * seq_rows:(i + 1) * seq_rows]

    def stack(x, unit):
        s0, g = unit
        return jnp.concatenate([x[j * c:(j + 1) * c, lanes(g)] * hm_ref[hh:hh + 1, :]
                                for j in range(s0, s0 + ns) for hh in range(RW_GROUP)], axis=0)

    a4 = [stack(a_t, un) for un in units]
    r4 = [stack(r_t, un) for un in units]
    v4 = [stack(v, un) for un in units]
    gram = [_mm_nt(jnp.concatenate([a4[n], r4[n]], axis=0),
                   jnp.concatenate([stack(b_t, un), *pad, stack(k_t, un), *pad], axis=0))
            for n, un in enumerate(units)]
    from_state = [[_mm_nt(jnp.concatenate([of_seq(a4[n], i), of_seq(r4[n], i)], axis=0), sbd_scr[s0 + i, g])
                   for i in range(ns)] for n, (s0, g) in enumerate(units)]
    fs_a = [jnp.concatenate([f[0:seq_rows] for f in fs], axis=0) for fs in from_state]
    fs_r = [jnp.concatenate([f[seq_rows:2 * seq_rows] for f in fs], axis=0) for fs in from_state]
    low2 = [gm[0:rows] * m_strict for gm in gram]
    rhs_u = [fs_a[n] + _mm(low2[n][:, rw_pad:2 * rw_pad], jnp.concatenate([v4[n], *pad], axis=0))
             for n in range(len(units))]
    inv = _unit_lower_inverse([lw[:, 0:rows] for lw in low2], c)
    u = [_mm(inv[n], rhs_u[n]) for n in range(len(units))]
    o_unit = {}
    for n, (s0, g) in enumerate(units):
        uv = jnp.concatenate([u[n], *pad, v4[n], *pad], axis=0)
        o4 = fs_r[n] + _mm(gram[n][rows:2 * rows] * m_incl, uv)
        for i in range(ns):
            o_j = o4[i * seq_rows:i * seq_rows + c]
            for hh in range(1, RW_GROUP):
                o_j = o_j + o4[i * seq_rows + hh * c:i * seq_rows + (hh + 1) * c]
            o_unit[(s0 + i, g)] = o_j
        b4 = stack(b_e, (s0, g))
        k4 = stack(k_e, (s0, g))
        for i in range(ns):
            ends = jnp.concatenate([of_seq(b4, i), of_seq(k4, i)], axis=0)
            uv_j = jnp.concatenate([of_seq(u[n], i), of_seq(v4[n], i)], axis=0)
            sbd_scr[s0 + i, g] = sbd_scr[s0 + i, g] * jnp.exp(cum_ends[s0 + i][:, lanes(g)]) + _mm_tn(uv_j, ends)

    @pl.when(ci == n_chunks - 1)
    def _fin():
        for j in seqs:
            for h in range(RW_HEADS):
                g, sl = head_block(h)
                s_ref[j, h] = sbd_scr[j, g, sl, sl]

    o = jnp.concatenate([jnp.concatenate([o_unit[(j, g)] for g in groups], axis=1) for j in seqs], axis=0)
    inv_n = 1.0 / RW_HEAD
    mean = _mm(o, bd) * inv_n
    dev = o - mean
    var = _mm(dev * dev, bd) * inv_n
    normed = dev * lax.rsqrt(var + RW_LN_EPS) * lng_ref[...] + lnb_ref[...]
    bonus = _mm(r * k1 * rk_ref[...], bd) * v
    out = (normed + bonus) * g_a
    o_ref[...] = out.astype(o_ref.dtype)


def _block_masks(n_blocks, c):
    r = np.arange(n_blocks * c)
    same = (r[:, None] // c) == (r[None, :] // c)
    causal = (same & (r[None, :] <= r[:, None])).astype(np.float32)
    strict = (same & (r[None, :] < r[:, None])).astype(np.float32)
    return causal, strict


def _rwkv_masks(c, nb):
    rows = nb * RW_GROUP * c
    rw_pad = -(-rows // LANES) * LANES
    lane_head = np.arange(RW_GW) // RW_HEAD
    hm = (lane_head[None, :] == np.arange(RW_GROUP)[:, None]).astype(np.float32)
    incl, strict = _block_masks(nb * RW_GROUP, c)
    widen = lambda m: np.tile(np.pad(m, ((0, 0), (0, rw_pad - rows))), (1, 2))
    return rw_pad, jnp.asarray(hm), jnp.asarray(widen(strict)), jnp.asarray(widen(incl))


ANY_SPEC = pl.BlockSpec(memory_space=pl.ANY)


def _mixer_io(cols, tails, *, o_prev, batch, c, nb, nc, row_block0, stacked):
    zeros = lambda t: (0,) * len(t)
    if stacked is None:
        prev = []
        st_shapes = [jax.ShapeDtypeStruct((batch,) + t, F32) for t in tails]
        st_specs = [pl.BlockSpec((nb,) + t, functools.partial(lambda b, ci, z: (b,) + z, z=zeros(t))) for t in tails]
    else:
        depth, layer, prev = stacked
        prev = list(prev)
        st_shapes = [jax.ShapeDtypeStruct((depth, batch) + t, F32) for t in tails]
        st_specs = [pl.BlockSpec((None, nb) + t, functools.partial(lambda b, ci, z: (layer, b) + z, z=zeros(t)))
                    for t in tails]
    tok_idx = lambda b, ci: (row_block0 + b * nc + ci, 0)
    aliased = [o_prev, *prev]
    in_specs = [ANY_SPEC] * len(aliased) + [pl.BlockSpec((nb * c, cols), tok_idx)]
    out_shape = (jax.ShapeDtypeStruct(o_prev.shape, o_prev.dtype), *st_shapes)
    out_specs = (pl.BlockSpec((nb * c, BR_W), tok_idx), *st_specs)
    return in_specs, out_shape, out_specs, {i: i for i in range(len(aliased))}, aliased


def _rwkv(pa, shift0, wkv0, layer, prm, *, o_prev, batch, seq, c, nb, ns, row_block0, stacked):
    nc = seq // c
    assert batch % nb == 0 and nb % ns == 0
    rw_pad, hm, m_strict, m_incl = _rwkv_masks(c, ns)
    rows = ns * RW_GROUP * c
    tri = jnp.asarray(_block_masks(nb, c)[0]).astype(BF16)
    io_specs, out_shape, out_specs, aliases, aliased = _mixer_io(
        RW_COLS, [(RW_HEADS, RW_HEAD, RW_HEAD), (1, RW_COLS)], o_prev=o_prev, batch=batch, c=c, nb=nb, nc=nc,
        row_block0=row_block0, stacked=stacked)
    kern = functools.partial(_rwkv_kernel, c=c, nb=nb, ns=ns, n_chunks=nc, rw_pad=rw_pad, n_alias=len(aliased))
    vec = lambda n: pl.BlockSpec((1, n), lambda b, ci: (0, 0))
    mat = lambda r_, n: pl.BlockSpec((r_, n), lambda b, ci: (0, 0))
    prm = tuple(prm) + (hm, m_strict, m_incl, tri)
    return pl.pallas_call(
        kern,
        out_shape=out_shape,
        grid=(batch // nb, nc),
        input_output_aliases=aliases,
        in_specs=[*io_specs,
                  pl.BlockSpec((None, nb, 1, RW_COLS), lambda b, ci: (layer, b, 0, 0)),
                  pl.BlockSpec((None, nb, RW_HEADS, RW_HEAD, RW_HEAD), lambda b, ci: (layer, b, 0, 0, 0)),
                  vec(RW_COLS), vec(512), mat(128, 512), vec(512), mat(128, 512), mat(128, 512),
                  vec(512), vec(512), vec(512), vec(512), vec(512), mat(512, 512),
                  mat(RW_GROUP, RW_GW), mat(rows, 2 * rw_pad), mat(rows, 2 * rw_pad), mat(nb * c, nb * c)],
        out_specs=out_specs,
        scratch_shapes=[pltpu.VMEM((nb, SUBLANES + c, RW_COLS), F32),
                        pltpu.VMEM((nb, RW_HEADS // RW_GROUP, RW_GW, RW_GW), F32)],
        compiler_params=_cparams(2),
        name="rwkv7",
    )(*aliased, pa, shift0, wkv0, *prm)


def _ret_kernel(*refs, c, nb, n_alias):
    refs = refs[n_alias:]
    tok_refs = refs[:1]
    cos_ref, sin_ref, mask_ref, s0_ref, o_ref, s_ref = refs[1:]
    ci = pl.program_id(1)

    @pl.when(ci == 0)
    def _init():
        s_ref[...] = s0_ref[...]

    rows = nb * c
    width = RET_HEADS * RET_DK
    half = RET_DK // 2
    cos = jnp.concatenate([cos_ref[...]] * nb, axis=0)
    sin = jnp.concatenate([sin_ref[...]] * nb, axis=0)
    first_half = (_iota((rows, width), 1) & half) == 0

    def rot(x):
        partner = jnp.where(first_half, pltpu.roll(x, width - half, 1), pltpu.roll(x, half, 1))
        return x * cos + partner * sin

    pb = jnp.concatenate([t[...] for t in tok_refs], axis=0)
    q = rot(pb[:, 0:width])
    k = rot(pb[:, width:2 * width]) * (RET_DK ** -0.5)
    v = pb[:, 2 * width:2 * width + BR_W]
    g_b = pb[:, 2 * width + BR_W:2 * width + 2 * BR_W]

    mask = mask_ref[...]
    causal = mask > 0.5
    dist = jnp.where(causal, (_iota((rows, rows), 0) - _iota((rows, rows), 1)).astype(F32), 0.0)
    t1 = jnp.sum(mask, axis=-1, keepdims=True)
    for h in range(RET_HEADS):
        log_gamma = math.log1p(-(2.0 ** (-5.0 - h)))
        dmat = jnp.where(causal, jnp.exp(dist * log_gamma), 0.0)
        q_h = q[:, h * RET_DK:(h + 1) * RET_DK]
        k_h = k[:, h * RET_DK:(h + 1) * RET_DK]
        v_h = v[:, h * RET_DV:(h + 1) * RET_DV]
        k_dec = k_h * jnp.exp((c - t1) * log_gamma)
        from_state = []
        for j in range(nb):
            js = slice(j * c, (j + 1) * c)
            s_h = s_ref[j, h]
            from_state.append(_mm(q_h[js], s_h))
            s_ref[j, h] = math.exp(c * log_gamma) * s_h + _mm_tn(k_dec[js], v_h[js])
        o_h = _mm(_mm_nt(q_h, k_h) * dmat, v_h) + jnp.exp(t1 * log_gamma) * jnp.concatenate(from_state, axis=0)
        o_n = o_h * lax.rsqrt(jnp.mean(o_h * o_h, axis=-1, keepdims=True) + NORM_EPS)
        o_ref[:, h * RET_DV:(h + 1) * RET_DV] = (o_n * _silu(g_b[:, h * RET_DV:(h + 1) * RET_DV])).astype(o_ref.dtype)


def _ret(pb, cos, sin, ret0, layer, *, o_prev, batch, seq, c, nb, row_block0, stacked):
    nc = seq // c
    assert batch % nb == 0
    width = RET_HEADS * RET_DK
    mask = jnp.asarray(_block_masks(nb, c)[0])
    io_specs, out_shape, out_specs, aliases, aliased = _mixer_io(
        RET_COLS, [(RET_HEADS, RET_DK, RET_DV)], o_prev=o_prev, batch=batch, c=c, nb=nb, nc=nc,
        row_block0=row_block0, stacked=stacked)
    return pl.pallas_call(
        functools.partial(_ret_kernel, c=c, nb=nb, n_alias=len(aliased)),
        out_shape=out_shape,
        grid=(batch // nb, nc),
        input_output_aliases=aliases,
        in_specs=[*io_specs,
                  pl.BlockSpec((c, width), lambda b, ci: (ci, 0)),
                  pl.BlockSpec((c, width), lambda b, ci: (ci, 0)),
                  pl.BlockSpec((nb * c, nb * c), lambda b, ci: (0, 0)),
                  pl.BlockSpec((None, nb, RET_HEADS, RET_DK, RET_DV), lambda b, ci: (layer, b, 0, 0, 0))],
        out_specs=out_specs,
        compiler_params=_cparams(2),
        name="retention",
    )(*aliased, pb, cos, sin, mask, ret0)


def _hgrn_kernel(*refs, c, m, nb, layer, n_chunks, n_alias):
    refs = refs[n_alias:]
    tok_refs = refs[:1]
    lbp_ref, ng_ref, tri_ref, s0_ref, o_ref, s_ref, st_scr = refs[1:]
    ci = pl.program_id(1)

    @pl.when(ci == 0)
    def _init():
        for j in range(nb):
            for h in range(HG_HEADS):
                st_scr[j, h] = s0_ref[j, h].T

    lbp = lbp_ref[...]
    e = jnp.exp(lbp - jnp.max(lbp, axis=0, keepdims=True))
    soft = e / jnp.sum(e, axis=0, keepdims=True)
    lb = jnp.zeros((1, BR_W), F32)
    for j in range(1, layer + 1):
        lb = lb + soft[j:j + 1, :]

    pc = jnp.concatenate([t[...] for t in tok_refs], axis=0)
    q_all = _silu(pc[:, 0:BR_W])
    f_in = pc[:, BR_W:2 * BR_W]
    v_all = pc[:, 2 * BR_W:3 * BR_W]
    g_c = pc[:, 3 * BR_W:4 * BR_W]
    log_f = jnp.log(lb + (1.0 - lb) * _sigmoid(f_in))
    k_all = (1.0 - lb) * _sigmoid(-f_in)
    cum_all = _mm_mask(tri_ref[...], log_f)

    col_m = _iota((m, m), 1)
    row_l = _iota((m, LANES), 0)
    outs = {}
    for j in range(nb):
        js = slice(j * c, (j + 1) * c)
        q, k, v, cum = q_all[js], k_all[js], v_all[js], cum_all[js]
        cum_last = cum[c - 1:c, :]
        q_e = q * jnp.exp(cum)
        k_e = k * jnp.exp(cum_last - cum)
        p_end = jnp.exp(cum_last)
        for h in range(HG_HEADS):
            sl = slice(h * HG_D, (h + 1) * HG_D)
            st = st_scr[j, h]
            o_rows = []
            for i in range(c // m):
                r0 = i * m
                q_i = q[r0:r0 + m, sl]
                c_i = cum[r0:r0 + m, sl]
                k_i = k[r0:r0 + m, sl]
                diag = jnp.zeros((m, m), F32)
                for s in range(m):
                    keep = row_l >= s
                    decay = jnp.exp(jnp.where(keep, c_i - c_i[s:s + 1, :], 0.0))
                    col = jnp.sum(jnp.where(keep, q_i * k_i[s:s + 1, :] * decay, 0.0), axis=-1, keepdims=True)
                    diag = jnp.where(col_m == s, col, diag)
                o_i = _mm(diag, v[r0:r0 + m, sl])
                if i > 0:
                    c_ref = cum[r0 - 1:r0, sl]
                    q_s = q_i * jnp.exp(c_i - c_ref)
                    k_s = k[0:r0, sl] * jnp.exp(c_ref - cum[0:r0, sl])
                    o_i = o_i + _mm(_mm_nt(q_s, k_s), v[0:r0, sl])
                o_rows.append(o_i)
            outs[(j, h)] = _mm_nt(q_e[:, sl], st) + jnp.concatenate(o_rows, axis=0)
            st_scr[j, h] = st * p_end[:, sl] + _mm_tn(v[:, sl], k_e[:, sl])
    for h in range(HG_HEADS):
        sl = slice(h * HG_D, (h + 1) * HG_D)
        o_h = jnp.concatenate([outs[(j, h)] for j in range(nb)], axis=0)
        o_n = o_h * lax.rsqrt(jnp.mean(o_h * o_h, axis=-1, keepdims=True) + NORM_EPS) * ng_ref[:, sl]
        o_ref[:, sl] = (o_n * _silu(g_c[:, sl])).astype(o_ref.dtype)

    @pl.when(ci == n_chunks - 1)
    def _fin():
        for j in range(nb):
            for h in range(HG_HEADS):
                s_ref[j, h] = st_scr[j, h].T


def _hgrn(pc, lbp, ng, hg0, state_layer, layer, *, o_prev, batch, seq, c, nb, row_block0, stacked):
    nc = seq // c
    assert batch % nb == 0
    m = 16 if c % 16 == 0 else SUBLANES
    depth = lbp.shape[0]
    tri = jnp.asarray(_block_masks(nb, c)[0]).astype(BF16)
    io_specs, out_shape, out_specs, aliases, aliased = _mixer_io(
        HG_COLS, [(HG_HEADS, HG_D, HG_D)], o_prev=o_prev, batch=batch, c=c, nb=nb, nc=nc,
        row_block0=row_block0, stacked=stacked)
    return pl.pallas_call(
        functools.partial(_hgrn_kernel, c=c, m=m, nb=nb, layer=layer, n_chunks=nc, n_alias=len(aliased)),
        out_shape=out_shape,
        grid=(batch // nb, nc),
        input_output_aliases=aliases,
        in_specs=[*io_specs,
                  pl.BlockSpec((depth, BR_W), lambda b, ci: (0, 0)),
                  pl.BlockSpec((1, BR_W), lambda b, ci: (0, 0)),
                  pl.BlockSpec((nb * c, nb * c), lambda b, ci: (0, 0)),
                  pl.BlockSpec((None, nb, HG_HEADS, HG_D, HG_D), lambda b, ci: (state_layer, b, 0, 0, 0))],
        out_specs=out_specs,
        scratch_shapes=[pltpu.VMEM((nb, HG_HEADS, HG_D, HG_D), F32)],
        compiler_params=_cparams(2),
        name="hgrn2",
    )(*aliased, pc, lbp, ng, tri, hg0)


def _gdn_kernel(*refs, c, nb, ns, n_alias):
    refs = refs[n_alias:]
    tok_refs = refs[:1]
    (cs0_ref, s0_ref, cw_ref, alog_ref, dtb_ref, ng_ref, mc_ref, mst_ref, tri_ref,
     o_ref, s_ref, cs_ref, ext_scr) = refs[1:]
    ci = pl.program_id(1)
    keep = GD_CONV - 1

    @pl.when(ci == 0)
    def _init():
        for j in range(nb):
            ext_scr[j, SUBLANES - keep:SUBLANES, :] = cs0_ref[j]
        s_ref[...] = s0_ref[...]

    convs = []
    for j in range(nb):
        x = tok_refs[0][j * c:(j + 1) * c, 0:GD_QKV]
        ext_scr[j, SUBLANES:SUBLANES + c, :] = x
        conv = ext_scr[j, SUBLANES - keep:SUBLANES - keep + c, :] * cw_ref[0:1, :]
        for i in range(1, keep):
            conv = conv + ext_scr[j, SUBLANES - keep + i:SUBLANES - keep + i + c, :] * cw_ref[i:i + 1, :]
        convs.append(conv + x * cw_ref[keep:keep + 1, :])
        tail = ext_scr[j, SUBLANES + c - keep:SUBLANES + c, :]
        ext_scr[j, SUBLANES - keep:SUBLANES, :] = tail
        cs_ref[j] = tail

    act = _silu(jnp.concatenate(convs, axis=0))
    q = act[:, 0:BR_W]
    k = act[:, BR_W:2 * BR_W]
    v = act[:, 2 * BR_W:3 * BR_W]
    gate_ba = jnp.concatenate([t[:, GD_QKV:GD_COLS_PAD] for t in tok_refs], axis=0)
    g_d = gate_ba[:, 0:BR_W]
    ba = gate_ba[:, BR_W:BR_W + LANES]
    beta_all = _sigmoid(ba)
    g_all = -jnp.exp(alog_ref[...]) * _softplus(ba + dtb_ref[...])

    cum_all = _mm_mask(tri_ref[...], g_all)

    ur = ns * c
    rows = GD_HEADS * ur
    stack = lambda parts: jnp.concatenate(parts, axis=0)
    l2n = lambda z: z * lax.rsqrt(jnp.sum(z * z, axis=-1, keepdims=True) + 1e-6)
    head = lambda z, h: z[:, h * GD_D:(h + 1) * GD_D]
    qn = jnp.concatenate([l2n(head(q, h)) for h in range(GD_HEADS)], axis=1) * (GD_D ** -0.5)
    kn = jnp.concatenate([l2n(head(k, h)) for h in range(GD_HEADS)], axis=1)
    units = range(nb // ns)
    heads = range(GD_HEADS)
    stack_u = lambda z, un: stack([head(z[un * ur:(un + 1) * ur], h) for h in heads])
    col_u = lambda z, lane0, un: stack([z[un * ur:(un + 1) * ur, lane0 + h:lane0 + h + 1] for h in heads])
    q4 = [stack_u(qn, un) for un in units]
    k4 = [stack_u(kn, un) for un in units]
    v4 = [stack_u(v, un) for un in units]
    beta = [col_u(beta_all, 0, un) for un in units]
    cum = [col_u(cum_all, GD_HEADS, un) for un in units]
    causal = mc_ref[...] > 0.5
    m_strict = mst_ref[...]
    eye = _iota((rows, rows), 0) == _iota((rows, rows), 1)
    ones = jnp.ones((rows, rows), BF16)
    cum_t = [jnp.broadcast_to(cm, (rows, rows)) for cm in cum]
    cum_s = [_mm_mask(ones, jnp.where(eye, ct, 0.0)) for ct in cum_t]
    dmat = [jnp.where(causal, jnp.exp(jnp.where(causal, ct - cs_, 0.0)), 0.0) for ct, cs_ in zip(cum_t, cum_s)]
    a_mat = [beta[un] * _mm_nt(k4[un], k4[un]) * (dmat[un] * m_strict) for un in units]
    inv = _unit_lower_inverse([-a for a in a_mat], c)
    e_cum = [jnp.exp(cm) for cm in cum]
    sol = [_mm(inv[un], jnp.concatenate([(beta[un] * e_cum[un]) * k4[un], beta[un] * v4[un]], axis=1)) for un in units]
    blocks = [(h, i) for h in heads for i in range(ns)]
    rows_of = lambda h, i: slice((h * ns + i) * c, (h * ns + i + 1) * c)
    qw = [[_mm(jnp.concatenate([q4[un][rows_of(h, i)], sol[un][rows_of(h, i), 0:GD_D]], axis=0), s_ref[un * ns + i, h])
           for h, i in blocks] for un in units]
    delta = [sol[un][:, GD_D:2 * GD_D] - stack([z[c:2 * c] for z in qw[un]]) for un in units]
    o4 = [e_cum[un] * stack([z[0:c] for z in qw[un]]) + _mm(_mm_nt(q4[un], k4[un]) * dmat[un], delta[un])
          for un in units]
    for un in units:
        for h, i in blocks:
            hs = rows_of(h, i)
            cum_h = cum[un][hs]
            cum_last = cum_h[c - 1:c, :]
            s_ref[un * ns + i, h] = (jnp.exp(cum_last) * s_ref[un * ns + i, h]
                                     + _mm_tn(k4[un][hs] * jnp.exp(cum_last - cum_h), delta[un][hs]))
    for h in heads:
        sl = slice(h * GD_D, (h + 1) * GD_D)
        o_h = stack([o4[un][h * ur:(h + 1) * ur] for un in units])
        o_n = o_h * lax.rsqrt(jnp.mean(o_h * o_h, axis=-1, keepdims=True) + NORM_EPS) * ng_ref[:, sl]
        o_ref[:, sl] = (o_n * _silu(g_d[:, sl])).astype(o_ref.dtype)


def _gdn(pd, conv0, gd0, layer, cw, alog, dtb, ng, *, o_prev, batch, seq, c, nb, ns, row_block0, stacked):
    nc = seq // c
    assert batch % nb == 0 and nb % ns == 0
    keep = GD_CONV - 1
    rows = GD_HEADS * ns * c
    m_causal, m_strict = (jnp.asarray(m) for m in _block_masks(GD_HEADS * ns, c))
    tri = jnp.asarray(_block_masks(nb, c)[0]).astype(BF16)
    full = lambda *shape: pl.BlockSpec(shape, lambda b, ci: (0,) * len(shape))
    io_specs, out_shape, out_specs, aliases, aliased = _mixer_io(
        GD_COLS_PAD, [(GD_HEADS, GD_D, GD_D), (keep, GD_QKV)], o_prev=o_prev, batch=batch, c=c, nb=nb, nc=nc,
        row_block0=row_block0, stacked=stacked)
    return pl.pallas_call(
        functools.partial(_gdn_kernel, c=c, nb=nb, ns=ns, n_alias=len(aliased)),
        out_shape=out_shape,
        grid=(batch // nb, nc),
        input_output_aliases=aliases,
        in_specs=[*io_specs,
                  pl.BlockSpec((None, nb, keep, GD_QKV), lambda b, ci: (layer, b, 0, 0)),
                  pl.BlockSpec((None, nb, GD_HEADS, GD_D, GD_D), lambda b, ci: (layer, b, 0, 0, 0)),
                  full(GD_CONV, GD_QKV), full(1, LANES), full(1, LANES), full(1, BR_W),
                  full(rows, rows), full(rows, rows), full(nb * c, nb * c)],
        out_specs=out_specs,
        scratch_shapes=[pltpu.VMEM((nb, SUBLANES + c, GD_QKV), F32)],
        compiler_params=_cparams(2),
        name="gdn",
    )(*aliased, pd, conv0, gd0, cw, alog, dtb, ng, m_causal, m_strict, tri)


def _pick_chunk(seq):
    for c in (64, 56, 48, 40, 32, 24, 16, 8):
        if seq % c == 0:
            return c
    raise ValueError(f"sequence length {seq} is not a multiple of {SUBLANES}")


def _pick_tile(n, candidates):
    for t in candidates:
        if n % t == 0:
            return t
    raise ValueError(f"no tile in {candidates} divides {n}")


def _rope_tables(pos):
    half = RET_DK // 2
    inv = ROPE_BASE ** (-jnp.arange(half, dtype=F32) / half)
    ang = pos.astype(F32)[:, None] * inv[None, :]
    cos, sin = jnp.cos(ang), jnp.sin(ang)
    cos_t = jnp.tile(jnp.concatenate([cos, cos], axis=1), (1, RET_HEADS))
    sin_t = jnp.tile(jnp.concatenate([-sin, sin], axis=1), (1, RET_HEADS))
    return cos_t, sin_t


def kernel(x_prompt, x_sample, state_rwkv_wkv, state_rwkv_shift, state_ret, state_hgrn, state_gdn, state_gdn_conv, meta_tokens, norm_mix, w_in, rw_mu, rw_w0, rw_w2, rw_a0, rw_a2, rw_g2, rw_kk, rw_ka, rw_rk, rw_ln_g, rw_ln_b, hg_lb, hg_norm_g, gd_conv, gd_a_log, gd_dt_bias, gd_norm_g, w_branch, w_out, norm_ffn, w_up, w_down, norm_final):
    depth = norm_mix.shape[0]
    bp, tq, d = x_prompt.shape
    tp = tq + N_META
    bs, ts, _ = x_sample.shape
    n_p, n_s = bp * tp, bs * ts
    n_tok = n_p + n_s
    cp, cs = _pick_chunk(tp), _pick_chunk(ts)
    tm = _pick_tile(n_tok, (464, 512, 256, 232, 128, 64, 32, 16, 8))
    tm_in = _pick_tile(n_tok, (928, 464, 512, 256, 232, 128, 64, 32, 16, 8))
    nb_s = _pick_tile(bs, (8, 4, 2, 1))
    nb_p = _pick_tile(bp, (4, 2, 1))
    ncp, gp = tp // cp, bp // nb_p
    assert cs == ts and n_p % (nb_s * ts) == 0

    meta = jnp.broadcast_to(meta_tokens.astype(F32)[None], (bp, N_META, d))
    xp = jnp.concatenate([meta, x_prompt], axis=1).reshape(gp, nb_p, ncp, cp, d)
    x = jnp.concatenate([xp.transpose(0, 2, 1, 3, 4).reshape(n_p, d), x_sample.reshape(n_s, d)], axis=0)

    cos_p, sin_p = _rope_tables(jnp.arange(tp, dtype=jnp.int32))
    cos_s, sin_s = _rope_tables(PAST_LEN + jnp.arange(ts, dtype=jnp.int32))

    head_of = jnp.arange(BR_W) // RW_HEAD
    bd = (head_of[:, None] == head_of[None, :]).astype(BF16)

    zeros = lambda *s: jnp.zeros((1, bp) + s, F32)
    z_wkv, z_shift = zeros(RW_HEADS, RW_HEAD, RW_HEAD), zeros(1, RW_COLS)
    z_ret, z_hg, z_gd = zeros(RET_HEADS, RET_DK, RET_DV), zeros(HG_HEADS, HG_D, HG_D), zeros(GD_HEADS, GD_D, GD_D)
    z_conv = zeros(GD_CONV - 1, GD_QKV)
    shift_s = state_rwkv_shift.reshape(depth, bs, 1, RW_COLS)

    assert w_in.shape[2] == IN_G + GATE_COLS
    w_a, w_b, w_c, w_d, w_g = _w_in_prep(w_in, LANES)
    wb_all, wo_all = w_branch.astype(BF16), w_out.astype(BF16)
    wu_all, wd_all = w_up.astype(BF16), w_down.astype(BF16)

    new_p = [[] for _ in range(6)]
    s_stacks = [jnp.zeros((depth, bs) + tail, F32) for tail in (
        (RW_HEADS, RW_HEAD, RW_HEAD), (1, RW_COLS), (RET_HEADS, RET_DK, RET_DV), (HG_HEADS, HG_D, HG_D),
        (GD_HEADS, GD_D, GD_D), (GD_CONV - 1, GD_QKV))]
    h = _rmsnorm(x, norm_mix[0], tm, BF16)
    for l in range(depth):
        pa = _matmul(h, w_a, l, tm_in, 896, "in_proj_a")
        pb = _matmul(h, w_b, l, tm_in, 768, "in_proj_b")
        pc = _matmul(h, w_c, l, tm_in, 1024, "in_proj_c")
        pd = _matmul(h, w_d, l, tm_in, GD_COLS_PAD, "in_proj_d")
        gates = _matmul(h, w_g, l, tm_in, 1024, "in_proj_g", gate=True)

        row1 = lambda a: a.reshape(1, -1)
        pad_rows = lambda a, top: jnp.concatenate(
            [jnp.zeros((top, BR_W), F32), a, jnp.zeros((LANES - top - a.shape[0], BR_W), F32)], axis=0).astype(BF16)
        rw_prm = (row1(rw_mu[l]), row1(rw_w0[l]), pad_rows(rw_w2[l], 0), row1(rw_a0[l]), pad_rows(rw_a2[l], 64),
                  rw_g2[l].astype(BF16), row1(rw_kk[l]), row1(rw_ka[l]), row1(rw_rk[l]), row1(rw_ln_g[l]),
                  row1(rw_ln_b[l]), bd)
        lane_pad = lambda a: jnp.concatenate(
            [jnp.zeros((GD_HEADS,), F32), a, jnp.zeros((LANES - 2 * GD_HEADS,), F32)]).reshape(1, LANES)
        gd_prm = (gd_conv[l], lane_pad(gd_a_log[l]), lane_pad(gd_dt_bias[l]), row1(gd_norm_g[l]))
        hg_g = row1(hg_norm_g[l])

        oa = ob = oc = od = jnp.zeros((n_tok, BR_W), BF16)
        for grp, (batch, seq, c, nb, ns, rb0, sts, lay, cos, sin) in enumerate((
                (bp, tp, cp, nb_p, 1, 0, (z_wkv, z_shift, z_ret, z_hg, z_gd, z_conv), 0, cos_p, sin_p),
                (bs, ts, cs, nb_s, nb_s, n_p // (nb_s * ts),
                 (state_rwkv_wkv, shift_s, state_ret, state_hgrn, state_gdn, state_gdn_conv), l, cos_s, sin_s))):
            kw = dict(batch=batch, seq=seq, c=c, nb=nb, row_block0=rb0)
            stk = lambda *prev: None if grp == 0 else (depth, l, prev)
            oa, wkv, shift = _rwkv(pa, sts[1], sts[0], lay, rw_prm, o_prev=oa, ns=ns, stacked=stk(*s_stacks[0:2]), **kw)
            ob, ret = _ret(pb, cos, sin, sts[2], lay, o_prev=ob, stacked=stk(*s_stacks[2:3]), **kw)
            oc, hg = _hgrn(pc, hg_lb, hg_g, sts[3], lay, l, o_prev=oc, stacked=stk(*s_stacks[3:4]), **kw)
            od, gd, conv = _gdn(pd, sts[5], sts[4], lay, *gd_prm, o_prev=od, ns=ns, stacked=stk(*s_stacks[4:6]), **kw)
            if grp == 0:
                for i, st in enumerate((wkv, shift.reshape(batch, RW_COLS), ret, hg, gd, conv)):
                    new_p[i].append(st)
            else:
                s_stacks = [wkv, shift, ret, hg, gd, conv]
        x = _merge(x, (oa, ob, oc, od), gates, wb_all, wo_all, l, tm_in, 256)
        last = l == depth - 1
        res = _ffn(x, norm_ffn[l], norm_final if last else norm_mix[l + 1], wu_all, wd_all, l, tm_in, 512, last)
        if last:
            y = res
        else:
            x, h = res

    y_prompt = y[:n_p].reshape(gp, ncp, nb_p, cp, d).transpose(0, 2, 1, 3, 4).reshape(bp, tp, d)[:, N_META:]
    y_sample = y[n_p:].reshape(bs, ts, d)
    p_states = tuple(jnp.stack(z) for z in new_p)
    s_wkv, s_shift, s_ret, s_hg, s_gd, s_conv = s_stacks
    return (y_prompt, y_sample) + p_states + (s_wkv, s_shift.reshape(depth, bs, RW_COLS), s_ret, s_hg, s_gd, s_conv)
```

```python
import functools
import math

import jax
import jax.numpy as jnp
import numpy as np
from jax import lax
from jax.experimental import pallas as pl
from jax.experimental.pallas import tpu as pltpu

F32 = jnp.float32
BF16 = jnp.bfloat16

N_META = 16
PAST_LEN = 16384
NORM_EPS = 1e-6
RW_LN_EPS = 64e-5
ROPE_BASE = 10000.0

D_MODEL = 2048
BR_W = 512
RW_HEADS, RW_HEAD = 8, 64
RW_COLS = 1792
RW_GROUP = 4
RW_GW = RW_GROUP * RW_HEAD
RET_HEADS, RET_DK, RET_DV = 4, 64, 128
RET_COLS = 1536
HG_HEADS, HG_D = 4, 128
HG_COLS = 2048
GD_HEADS, GD_D = 4, 128
GD_CONV = 4
GD_QKV = 1536
GD_COLS_PAD = 2176

LANES = 128
SUBLANES = 8
VMEM_LIMIT = 56 * 1024 * 1024


def _cparams(n_axes):
    return pltpu.CompilerParams(dimension_semantics=("arbitrary",) * n_axes, vmem_limit_bytes=VMEM_LIMIT)


def _dg(a, b, ca, cb):
    return lax.dot_general(a, b, (((ca,), (cb,)), ((), ())), preferred_element_type=F32)


def _mm(a, b):
    return _dg(a.astype(BF16), b.astype(BF16), 1, 0)


def _mm_nt(a, b):
    return _dg(a.astype(BF16), b.astype(BF16), 1, 1)


def _mm_tn(a, b):
    return _dg(a.astype(BF16), b.astype(BF16), 0, 0)


def _split(x, n):
    parts, r = [], x
    for _ in range(n):
        p = r.astype(BF16)
        parts.append(p)
        r = r - p.astype(F32)
    return parts


def _mm_mask(m01, x):
    x0, x1, x2 = _split(x, 3)
    return _dg(m01, x0, 1, 0) + (_dg(m01, x1, 1, 0) + _dg(m01, x2, 1, 0))


def _iota(shape, dim):
    return lax.broadcasted_iota(jnp.int32, shape, dim)


def _softplus(x):
    return jnp.maximum(x, 0.0) + jnp.log(1.0 + jnp.exp(-jnp.abs(x)))


def _sigmoid(x):
    return jax.nn.sigmoid(x)


def _silu(x):
    return x * jax.nn.sigmoid(x)


def _unit_lower_inverse(lows, c):
    shape = lows[0].shape
    eye = (_iota(shape, 0) == _iota(shape, 1)).astype(F32)
    invs = [eye + low for low in lows]
    powers = list(lows)
    covered = 1
    while covered < c - 1:
        powers = [_mm(p, p) for p in powers]
        invs = [inv + _mm(inv, p) for inv, p in zip(invs, powers)]
        covered = 2 * covered + 1
    return invs


def _rmsnorm_kernel(x_ref, g_ref, o_ref):
    x = x_ref[...]
    y = x * lax.rsqrt(jnp.mean(x * x, axis=-1, keepdims=True) + NORM_EPS)
    o_ref[...] = (y * g_ref[...]).astype(o_ref.dtype)


def _rmsnorm(x, g, tm, out_dtype):
    n, d = x.shape
    return pl.pallas_call(
        _rmsnorm_kernel,
        out_shape=jax.ShapeDtypeStruct((n, d), out_dtype),
        grid=(n // tm,),
        in_specs=[pl.BlockSpec((tm, d), lambda i: (i, 0)), pl.BlockSpec((1, d), lambda i: (0, 0))],
        out_specs=pl.BlockSpec((tm, d), lambda i: (i, 0)),
        compiler_params=_cparams(1),
        name="rmsnorm",
    )(x, g.reshape(1, d))


def _matmul_kernel(x_ref, w_ref, o_ref):
    o_ref[...] = _dg(x_ref[...], w_ref[...], 1, 1)


def _matmul_sigmoid_kernel(x_ref, w_ref, o_ref):
    o_ref[...] = _sigmoid(_dg(x_ref[...], w_ref[...], 1, 1)).astype(o_ref.dtype)


def _matmul(x, w, layer, tm, tn, name, gate=False):
    n, k = x.shape
    m = w.shape[1]
    return pl.pallas_call(
        _matmul_sigmoid_kernel if gate else _matmul_kernel,
        out_shape=jax.ShapeDtypeStruct((n, m), BF16 if gate else F32),
        grid=(n // tm, m // tn),
        in_specs=[pl.BlockSpec((tm, k), lambda i, j: (i, 0)), pl.BlockSpec((None, tn, k), lambda i, j: (layer, j, 0))],
        out_specs=pl.BlockSpec((tm, tn), lambda i, j: (i, j)),
        compiler_params=_cparams(2),
        name=name,
    )(x, w)


IN_A, IN_B, IN_C = RW_COLS, RW_COLS + RET_COLS, RW_COLS + RET_COLS + HG_COLS
IN_BA = IN_C + GD_QKV
IN_G = IN_BA + 2 * GD_HEADS + BR_W
GATE_COLS = 4 * D_MODEL


def _w_in_prep_kernel(w_ref, a_ref, b_ref, c_ref, d_ref, g_ref):
    cast = lambda lo, hi: w_ref[lo:hi, :].astype(BF16)
    a_ref[...] = cast(0, IN_A)
    b_ref[...] = cast(IN_A, IN_B)
    c_ref[...] = cast(IN_B, IN_C)
    d_ref[0:GD_QKV, :] = cast(IN_C, IN_BA)
    d_ref[GD_QKV:GD_QKV + BR_W, :] = cast(IN_BA + 2 * GD_HEADS, IN_G)
    ba = jnp.concatenate([w_ref[IN_BA:IN_BA + 2 * GD_HEADS, :],
                          jnp.zeros((LANES - 2 * GD_HEADS, w_ref.shape[1]), F32)], axis=0)
    d_ref[GD_QKV + BR_W:GD_COLS_PAD, :] = ba.astype(BF16)
    step = 1024
    for lo in range(0, GATE_COLS, step):
        g_ref[lo:lo + step, :] = cast(IN_G + lo, IN_G + lo + step)


def _w_in_prep(w_in_t, tk):
    depth, n, k = w_in_t.shape
    out = lambda cols: (jax.ShapeDtypeStruct((depth, cols, k), BF16), pl.BlockSpec((None, cols, tk), lambda l, i: (l, 0, i)))
    shapes, specs = zip(*(out(cols) for cols in (RW_COLS, RET_COLS, HG_COLS, GD_COLS_PAD, GATE_COLS)))
    return pl.pallas_call(
        _w_in_prep_kernel,
        out_shape=shapes,
        grid=(depth, k // tk),
        in_specs=[pl.BlockSpec((None, n, tk), lambda l, i: (l, 0, i))],
        out_specs=specs,
        compiler_params=_cparams(2),
        name="w_in_prep",
    )(w_in_t)


def _merge_kernel(x_ref, oa_ref, ob_ref, oc_ref, od_ref, g0_ref, g1_ref, g2_ref, g3_ref, wb_ref, wo_ref, y_ref):
    j = pl.program_id(1)

    @pl.when(j == 0)
    def _init():
        y_ref[...] = x_ref[...]

    merged = None
    for n, (o_ref, g_ref) in enumerate(((oa_ref, g0_ref), (ob_ref, g1_ref), (oc_ref, g2_ref), (od_ref, g3_ref))):
        term = g_ref[...].astype(F32) * _dg(o_ref[...], wb_ref[n], 1, 0)
        merged = term if merged is None else merged + term
    y_ref[...] += _dg(merged.astype(BF16), wo_ref[...], 1, 0)


def _merge(x, outs, pg, wb, wo, layer, tm, tj):
    n, d = x.shape
    nj = d // tj
    o_spec = pl.BlockSpec((tm, BR_W), lambda i, j: (i, 0))
    g_specs = [pl.BlockSpec((tm, tj), functools.partial(lambda i, j, nb: (i, nb * nj + j), nb=nb)) for nb in range(4)]
    return pl.pallas_call(
        _merge_kernel,
        out_shape=jax.ShapeDtypeStruct((n, d), F32),
        grid=(n // tm, nj),
        in_specs=[pl.BlockSpec((tm, d), lambda i, j: (i, 0)), o_spec, o_spec, o_spec, o_spec, *g_specs,
                  pl.BlockSpec((None, 4, BR_W, tj), lambda i, j: (layer, 0, 0, j)),
                  pl.BlockSpec((None, tj, d), lambda i, j: (layer, j, 0))],
        out_specs=pl.BlockSpec((tm, d), lambda i, j: (i, 0)),
        compiler_params=_cparams(2),
        name="merge",
    )(x, *outs, pg, pg, pg, pg, wb, wo)


def _rms(x, g):
    return x * lax.rsqrt(jnp.mean(x * x, axis=-1, keepdims=True) + NORM_EPS) * g


def _ffn_kernel(x_ref, g_ref, gn_ref, wu_ref, wg_ref, wd_ref, *out_and_scratch, last):
    *outs, h_ref = out_and_scratch
    y_ref = outs[0]
    j = pl.program_id(1)

    @pl.when(j == 0)
    def _init():
        x = x_ref[...]
        h_ref[...] = _rms(x, g_ref[...]).astype(BF16)
        y_ref[...] = x

    h = h_ref[...]
    up = _dg(h, wu_ref[...], 1, 0)
    gate = _dg(h, wg_ref[...], 1, 0)
    y_ref[...] += _dg((_silu(gate) * up).astype(BF16), wd_ref[...], 1, 0)

    @pl.when(j == pl.num_programs(1) - 1)
    def _fin():
        if last:
            y_ref[...] = _rms(y_ref[...], gn_ref[...])
        else:
            outs[1][...] = _rms(y_ref[...], gn_ref[...]).astype(BF16)


def _ffn(x, g, g_next, w_up, w_down, layer, tm, tf, last):
    n, d = x.shape
    ff = w_down.shape[1]
    nf = ff // tf
    row_spec = pl.BlockSpec((tm, d), lambda i, j: (i, 0))
    vec_spec = pl.BlockSpec((1, d), lambda i, j: (0, 0))
    if last:
        out_shape, out_specs = jax.ShapeDtypeStruct((n, d), F32), row_spec
    else:
        out_shape = (jax.ShapeDtypeStruct((n, d), F32), jax.ShapeDtypeStruct((n, d), BF16))
        out_specs = (row_spec, row_spec)
    return pl.pallas_call(
        functools.partial(_ffn_kernel, last=last),
        out_shape=out_shape,
        grid=(n // tm, nf),
        in_specs=[pl.BlockSpec((tm, d), lambda i, j: (i, 0), pipeline_mode=pl.Buffered(1)), vec_spec, vec_spec,
                  pl.BlockSpec((None, d, tf), lambda i, j: (layer, 0, j)),
                  pl.BlockSpec((None, d, tf), lambda i, j: (layer, 0, nf + j)),
                  pl.BlockSpec((None, tf, d), lambda i, j: (layer, j, 0))],
        out_specs=out_specs,
        scratch_shapes=[pltpu.VMEM((tm, d), BF16)],
        compiler_params=_cparams(2),
        name="ffn",
    )(x, g.reshape(1, d), g_next.reshape(1, d), w_up, w_up, w_down)


def _rwkv_kernel(*refs, c, nb, ns, n_chunks, rw_pad, n_alias):
    refs = refs[n_alias:]
    tok_refs = refs[:1]
    (sh0_ref, s0_ref, mu_ref, w0_ref, w2_ref, a0_ref, a2_ref, g2_ref, kkw_ref, ka_ref, rk_ref, lng_ref, lnb_ref,
     bd_ref, hm_ref, ms_ref, mi_ref, tri_ref, o_ref, s_ref, sh_ref, xs_scr, sbd_scr) = refs[1:]
    ci = pl.program_id(1)
    seqs = range(nb)
    head_block = lambda h: (h // RW_GROUP, slice((h % RW_GROUP) * RW_HEAD, (h % RW_GROUP + 1) * RW_HEAD))

    @pl.when(ci == 0)
    def _init():
        sbd_scr[...] = jnp.zeros_like(sbd_scr)
        for j in seqs:
            xs_scr[j, SUBLANES - 1:SUBLANES, :] = sh0_ref[j]
            for h in range(RW_HEADS):
                g, sl = head_block(h)
                sbd_scr[j, g, sl, sl] = s0_ref[j, h]

    pa = jnp.concatenate([t[...] for t in tok_refs], axis=0)
    prevs = []
    for j in seqs:
        pa_j = pa[j * c:(j + 1) * c, :]
        xs_scr[j, SUBLANES:SUBLANES + c, :] = pa_j
        prevs.append(xs_scr[j, SUBLANES - 1:SUBLANES - 1 + c, :])
        last = pa_j[c - 1:c, :]
        xs_scr[j, SUBLANES - 1:SUBLANES, :] = last
        sh_ref[j] = last
    prev = jnp.concatenate(prevs, axis=0)

    xm = pa + (prev - pa) * mu_ref[...]
    r = xm[:, 0:512]
    k0 = xm[:, 512:1024]
    v = xm[:, 1024:1536]
    wa = xm[:, 1536:1664]
    gl = xm[:, 1664:1792]
    w_log = -_softplus(-(w0_ref[...] + _mm(jnp.tanh(wa), w2_ref[...]))) - 0.5
    ld = -jnp.exp(w_log)
    asig = _sigmoid(a0_ref[...] + _mm(wa, a2_ref[...]))
    g_a = _mm(_sigmoid(gl), g2_ref[...])
    bd = bd_ref[...]
    kkr = k0 * kkw_ref[...]
    kkn = kkr * lax.rsqrt(_mm(kkr * kkr, bd) + 1e-6)
    k1 = k0 * (1.0 + (asig - 1.0) * ka_ref[...])
    a_vec = -kkn
    b_vec = kkn * asig

    cum = _mm_mask(tri_ref[...], ld)
    cum_ends = [cum[(j + 1) * c - 1:(j + 1) * c, :] for j in seqs]
    cum_last = jnp.concatenate([jnp.broadcast_to(e, (c, BR_W)) for e in cum_ends], axis=0)
    a_t = a_vec * jnp.exp(cum - ld)
    r_t = r * jnp.exp(cum)
    inv_p = jnp.exp(-cum)
    b_t = b_vec * inv_p
    k_t = k1 * inv_p
    to_end = jnp.exp(cum_last - cum)
    b_e = b_vec * to_end
    k_e = k1 * to_end

    seq_rows = RW_GROUP * c
    rows = ns * seq_rows
    pad = [jnp.zeros((rw_pad - rows, RW_GW), F32)] if rw_pad > rows else []
    m_strict = ms_ref[...]
    m_incl = mi_ref[...]
    groups = range(RW_HEADS // RW_GROUP)
    units = [(s0, g) for s0 in range(0, nb, ns) for g in groups]
    lanes = lambda g: slice(g * RW_GW, (g + 1) * RW_GW)
    of_seq = lambda x, i: x[i * seq_rows:(i + 1) * seq_rows]

    def stack(x, unit):
        s0, g = unit
        return jnp.concatenate([x[j * c:(j + 1) * c, lanes(g)] * hm_ref[hh:hh + 1, :]
                                for j in range(s0, s0 + ns) for hh in range(RW_GROUP)], axis=0)

    a4 = [stack(a_t, un) for un in units]
    r4 = [stack(r_t, un) for un in units]
    v4 = [stack(v, un) for un in units]
    gram = [_mm_nt(jnp.concatenate([a4[n], r4[n]], axis=0),
                   jnp.concatenate([stack(b_t, un), *pad, stack(k_t, un), *pad], axis=0))
            for n, un in enumerate(units)]
    from_state = [[_mm_nt(jnp.concatenate([of_seq(a4[n], i), of_seq(r4[n], i)], axis=0), sbd_scr[s0 + i, g])
                   for i in range(ns)] for n, (s0, g) in enumerate(units)]
    fs_a = [jnp.concatenate([f[0:seq_rows] for f in fs], axis=0) for fs in from_state]
    fs_r = [jnp.concatenate([f[seq_rows:2 * seq_rows] for f in fs], axis=0) for fs in from_state]
    low2 = [gm[0:rows] * m_strict for gm in gram]
    rhs_u = [fs_a[n] + _mm(low2[n][:, rw_pad:2 * rw_pad], jnp.concatenate([v4[n], *pad], axis=0))
             for n in range(len(units))]
    inv = _unit_lower_inverse([lw[:, 0:rows] for lw in low2], c)
    u = [_mm(inv[n], rhs_u[n]) for n in range(len(units))]
    o_unit = {}
    for n, (s0, g) in enumerate(units):
        uv = jnp.concatenate([u[n], *pad, v4[n], *pad], axis=0)
        o4 = fs_r[n] + _mm(gram[n][rows:2 * rows] * m_incl, uv)
        for i in range(ns):
            o_j = o4[i * seq_rows:i * seq_rows + c]
            for hh in range(1, RW_GROUP):
                o_j = o_j + o4[i * seq_rows + hh * c:i * seq_rows + (hh + 1) * c]
            o_unit[(s0 + i, g)] = o_j
        b4 = stack(b_e, (s0, g))
        k4 = stack(k_e, (s0, g))
        for i in range(ns):
            ends = jnp.concatenate([of_seq(b4, i), of_seq(k4, i)], axis=0)
            uv_j = jnp.concatenate([of_seq(u[n], i), of_seq(v4[n], i)], axis=0)
            sbd_scr[s0 + i, g] = sbd_scr[s0 + i, g] * jnp.exp(cum_ends[s0 + i][:, lanes(g)]) + _mm_tn(uv_j, ends)

    @pl.when(ci == n_chunks - 1)
    def _fin():
        for j in seqs:
            for h in range(RW_HEADS):
                g, sl = head_block(h)
                s_ref[j, h] = sbd_scr[j, g, sl, sl]

    o = jnp.concatenate([jnp.concatenate([o_unit[(j, g)] for g in groups], axis=1) for j in seqs], axis=0)
    inv_n = 1.0 / RW_HEAD
    mean = _mm(o, bd) * inv_n
    dev = o - mean
    var = _mm(dev * dev, bd) * inv_n
    normed = dev * lax.rsqrt(var + RW_LN_EPS) * lng_ref[...] + lnb_ref[...]
    bonus = _mm(r * k1 * rk_ref[...], bd) * v
    out = (normed + bonus) * g_a
    o_ref[...] = out.astype(o_ref.dtype)


def _block_masks(n_blocks, c):
    r = np.arange(n_blocks * c)
    same = (r[:, None] // c) == (r[None, :] // c)
    causal = (same & (r[None, :] <= r[:, None])).astype(np.float32)
    strict = (same & (r[None, :] < r[:, None])).astype(np.float32)
    return causal, strict


def _rwkv_masks(c, nb):
    rows = nb * RW_GROUP * c
    rw_pad = -(-rows // LANES) * LANES
    lane_head = np.arange(RW_GW) // RW_HEAD
    hm = (lane_head[None, :] == np.arange(RW_GROUP)[:, None]).astype(np.float32)
    incl, strict = _block_masks(nb * RW_GROUP, c)
    widen = lambda m: np.tile(np.pad(m, ((0, 0), (0, rw_pad - rows))), (1, 2))
    return rw_pad, jnp.asarray(hm), jnp.asarray(widen(strict)), jnp.asarray(widen(incl))


ANY_SPEC = pl.BlockSpec(memory_space=pl.ANY)


def _mixer_io(cols, tails, *, o_prev, batch, c, nb, nc, row_block0, stacked):
    zeros = lambda t: (0,) * len(t)
    if stacked is None:
        prev = []
        st_shapes = [jax.ShapeDtypeStruct((batch,) + t, F32) for t in tails]
        st_specs = [pl.BlockSpec((nb,) + t, functools.partial(lambda b, ci, z: (b,) + z, z=zeros(t))) for t in tails]
    else:
        depth, layer, prev = stacked
        prev = list(prev)
        st_shapes = [jax.ShapeDtypeStruct((depth, batch) + t, F32) for t in tails]
        st_specs = [pl.BlockSpec((None, nb) + t, functools.partial(lambda b, ci, z: (layer, b) + z, z=zeros(t)))
                    for t in tails]
    tok_idx = lambda b, ci: (row_block0 + b * nc + ci, 0)
    aliased = [o_prev, *prev]
    in_specs = [ANY_SPEC] * len(aliased) + [pl.BlockSpec((nb * c, cols), tok_idx)]
    out_shape = (jax.ShapeDtypeStruct(o_prev.shape, o_prev.dtype), *st_shapes)
    out_specs = (pl.BlockSpec((nb * c, BR_W), tok_idx), *st_specs)
    return in_specs, out_shape, out_specs, {i: i for i in range(len(aliased))}, aliased


def _rwkv(pa, shift0, wkv0, layer, prm, *, o_prev, batch, seq, c, nb, ns, row_block0, stacked):
    nc = seq // c
    assert batch % nb == 0 and nb % ns == 0
    rw_pad, hm, m_strict, m_incl = _rwkv_masks(c, ns)
    rows = ns * RW_GROUP * c
    tri = jnp.asarray(_block_masks(nb, c)[0]).astype(BF16)
    io_specs, out_shape, out_specs, aliases, aliased = _mixer_io(
        RW_COLS, [(RW_HEADS, RW_HEAD, RW_HEAD), (1, RW_COLS)], o_prev=o_prev, batch=batch, c=c, nb=nb, nc=nc,
        row_block0=row_block0, stacked=stacked)
    kern = functools.partial(_rwkv_kernel, c=c, nb=nb, ns=ns, n_chunks=nc, rw_pad=rw_pad, n_alias=len(aliased))
    vec = lambda n: pl.BlockSpec((1, n), lambda b, ci: (0, 0))
    mat = lambda r_, n: pl.BlockSpec((r_, n), lambda b, ci: (0, 0))
    prm = tuple(prm) + (hm, m_strict, m_incl, tri)
    return pl.pallas_call(
        kern,
        out_shape=out_shape,
        grid=(batch // nb, nc),
        input_output_aliases=aliases,
        in_specs=[*io_specs,
                  pl.BlockSpec((None, nb, 1, RW_COLS), lambda b, ci: (layer, b, 0, 0)),
                  pl.BlockSpec((None, nb, RW_HEADS, RW_HEAD, RW_HEAD), lambda b, ci: (layer, b, 0, 0, 0)),
                  vec(RW_COLS), vec(512), mat(128, 512), vec(512), mat(128, 512), mat(128, 512),
                  vec(512), vec(512), vec(512), vec(512), vec(512), mat(512, 512),
                  mat(RW_GROUP, RW_GW), mat(rows, 2 * rw_pad), mat(rows, 2 * rw_pad), mat(nb * c, nb * c)],
        out_specs=out_specs,
        scratch_shapes=[pltpu.VMEM((nb, SUBLANES + c, RW_COLS), F32),
                        pltpu.VMEM((nb, RW_HEADS // RW_GROUP, RW_GW, RW_GW), F32)],
        compiler_params=_cparams(2),
        name="rwkv7",
    )(*aliased, pa, shift0, wkv0, *prm)


def _ret_kernel(*refs, c, nb, n_alias):
    refs = refs[n_alias:]
    tok_refs = refs[:1]
    cos_ref, sin_ref, mask_ref, s0_ref, o_ref, s_ref = refs[1:]
    ci = pl.program_id(1)

    @pl.when(ci == 0)
    def _init():
        s_ref[...] = s0_ref[...]

    rows = nb * c
    width = RET_HEADS * RET_DK
    half = RET_DK // 2
    cos = jnp.concatenate([cos_ref[...]] * nb, axis=0)
    sin = jnp.concatenate([sin_ref[...]] * nb, axis=0)
    first_half = (_iota((rows, width), 1) & half) == 0

    def rot(x):
        partner = jnp.where(first_half, pltpu.roll(x, width - half, 1), pltpu.roll(x, half, 1))
        return x * cos + partner * sin

    pb = jnp.concatenate([t[...] for t in tok_refs], axis=0)
    q = rot(pb[:, 0:width])
    k = rot(pb[:, width:2 * width]) * (RET_DK ** -0.5)
    v = pb[:, 2 * width:2 * width + BR_W]
    g_b = pb[:, 2 * width + BR_W:2 * width + 2 * BR_W]

    mask = mask_ref[...]
    causal = mask > 0.5
    dist = jnp.where(causal, (_iota((rows, rows), 0) - _iota((rows, rows), 1)).astype(F32), 0.0)
    t1 = jnp.sum(mask, axis=-1, keepdims=True)
    for h in range(RET_HEADS):
        log_gamma = math.log1p(-(2.0 ** (-5.0 - h)))
        dmat = jnp.where(causal, jnp.exp(dist * log_gamma), 0.0)
        q_h = q[:, h * RET_DK:(h + 1) * RET_DK]
        k_h = k[:, h * RET_DK:(h + 1) * RET_DK]
        v_h = v[:, h * RET_DV:(h + 1) * RET_DV]
        k_dec = k_h * jnp.exp((c - t1) * log_gamma)
        from_state = []
        for j in range(nb):
            js = slice(j * c, (j + 1) * c)
            s_h = s_ref[j, h]
            from_state.append(_mm(q_h[js], s_h))
            s_ref[j, h] = math.exp(c * log_gamma) * s_h + _mm_tn(k_dec[js], v_h[js])
        o_h = _mm(_mm_nt(q_h, k_h) * dmat, v_h) + jnp.exp(t1 * log_gamma) * jnp.concatenate(from_state, axis=0)
        o_n = o_h * lax.rsqrt(jnp.mean(o_h * o_h, axis=-1, keepdims=True) + NORM_EPS)
        o_ref[:, h * RET_DV:(h + 1) * RET_DV] = (o_n * _silu(g_b[:, h * RET_DV:(h + 1) * RET_DV])).astype(o_ref.dtype)


def _ret(pb, cos, sin, ret0, layer, *, o_prev, batch, seq, c, nb, row_block0, stacked):
    nc = seq // c
    assert batch % nb == 0
    width = RET_HEADS * RET_DK
    mask = jnp.asarray(_block_masks(nb, c)[0])
    io_specs, out_shape, out_specs, aliases, aliased = _mixer_io(
        RET_COLS, [(RET_HEADS, RET_DK, RET_DV)], o_prev=o_prev, batch=batch, c=c, nb=nb, nc=nc,
        row_block0=row_block0, stacked=stacked)
    return pl.pallas_call(
        functools.partial(_ret_kernel, c=c, nb=nb, n_alias=len(aliased)),
        out_shape=out_shape,
        grid=(batch // nb, nc),
        input_output_aliases=aliases,
        in_specs=[*io_specs,
                  pl.BlockSpec((c, width), lambda b, ci: (ci, 0)),
                  pl.BlockSpec((c, width), lambda b, ci: (ci, 0)),
                  pl.BlockSpec((nb * c, nb * c), lambda b, ci: (0, 0)),
                  pl.BlockSpec((None, nb, RET_HEADS, RET_DK, RET_DV), lambda b, ci: (layer, b, 0, 0, 0))],
        out_specs=out_specs,
        compiler_params=_cparams(2),
        name="retention",
    )(*aliased, pb, cos, sin, mask, ret0)


def _hgrn_kernel(*refs, c, m, nb, layer, n_chunks, n_alias):
    refs = refs[n_alias:]
    tok_refs = refs[:1]
    lbp_ref, ng_ref, tri_ref, s0_ref, o_ref, s_ref, st_scr = refs[1:]
    ci = pl.program_id(1)

    @pl.when(ci == 0)
    def _init():
        for j in range(nb):
            for h in range(HG_HEADS):
                st_scr[j, h] = s0_ref[j, h].T

    lbp = lbp_ref[...]
    e = jnp.exp(lbp - jnp.max(lbp, axis=0, keepdims=True))
    soft = e / jnp.sum(e, axis=0, keepdims=True)
    lb = jnp.zeros((1, BR_W), F32)
    for j in range(1, layer + 1):
        lb = lb + soft[j:j + 1, :]

    pc = jnp.concatenate([t[...] for t in tok_refs], axis=0)
    q_all = _silu(pc[:, 0:BR_W])
    f_in = pc[:, BR_W:2 * BR_W]
    v_all = pc[:, 2 * BR_W:3 * BR_W]
    g_c = pc[:, 3 * BR_W:4 * BR_W]
    log_f = jnp.log(lb + (1.0 - lb) * _sigmoid(f_in))
    k_all = (1.0 - lb) * _sigmoid(-f_in)
    cum_all = _mm_mask(tri_ref[...], log_f)

    col_m = _iota((m, m), 1)
    row_l = _iota((m, LANES), 0)
    outs = {}
    for j in range(nb):
        js = slice(j * c, (j + 1) * c)
        q, k, v, cum = q_all[js], k_all[js], v_all[js], cum_all[js]
        cum_last = cum[c - 1:c, :]
        q_e = q * jnp.exp(cum)
        k_e = k * jnp.exp(cum_last - cum)
        p_end = jnp.exp(cum_last)
        for h in range(HG_HEADS):
            sl = slice(h * HG_D, (h + 1) * HG_D)
            st = st_scr[j, h]
            o_rows = []
            for i in range(c // m):
                r0 = i * m
                q_i = q[r0:r0 + m, sl]
                c_i = cum[r0:r0 + m, sl]
                k_i = k[r0:r0 + m, sl]
                diag = jnp.zeros((m, m), F32)
                for s in range(m):
                    keep = row_l >= s
                    decay = jnp.exp(jnp.where(keep, c_i - c_i[s:s + 1, :], 0.0))
                    col = jnp.sum(jnp.where(keep, q_i * k_i[s:s + 1, :] * decay, 0.0), axis=-1, keepdims=True)
                    diag = jnp.where(col_m == s, col, diag)
                o_i = _mm(diag, v[r0:r0 + m, sl])
                if i > 0:
                    c_ref = cum[r0 - 1:r0, sl]
                    q_s = q_i * jnp.exp(c_i - c_ref)
                    k_s = k[0:r0, sl] * jnp.exp(c_ref - cum[0:r0, sl])
                    o_i = o_i + _mm(_mm_nt(q_s, k_s), v[0:r0, sl])
                o_rows.append(o_i)
            outs[(j, h)] = _mm_nt(q_e[:, sl], st) + jnp.concatenate(o_rows, axis=0)
            st_scr[j, h] = st * p_end[:, sl] + _mm_tn(v[:, sl], k_e[:, sl])
    for h in range(HG_HEADS):
        sl = slice(h * HG_D, (h + 1) * HG_D)
        o_h = jnp.concatenate([outs[(j, h)] for j in range(nb)], axis=0)
        o_n = o_h * lax.rsqrt(jnp.mean(o_h * o_h, axis=-1, keepdims=True) + NORM_EPS) * ng_ref[:, sl]
        o_ref[:, sl] = (o_n * _silu(g_c[:, sl])).astype(o_ref.dtype)

    @pl.when(ci == n_chunks - 1)
    def _fin():
        for j in range(nb):
            for h in range(HG_HEADS):
                s_ref[j, h] = st_scr[j, h].T


def _hgrn(pc, lbp, ng, hg0, state_layer, layer, *, o_prev, batch, seq, c, nb, row_block0, stacked):
    nc = seq // c
    assert batch % nb == 0
    m = 16 if c % 16 == 0 else SUBLANES
    depth = lbp.shape[0]
    tri = jnp.asarray(_block_masks(nb, c)[0]).astype(BF16)
    io_specs, out_shape, out_specs, aliases, aliased = _mixer_io(
        HG_COLS, [(HG_HEADS, HG_D, HG_D)], o_prev=o_prev, batch=batch, c=c, nb=nb, nc=nc,
        row_block0=row_block0, stacked=stacked)
    return pl.pallas_call(
        functools.partial(_hgrn_kernel, c=c, m=m, nb=nb, layer=layer, n_chunks=nc, n_alias=len(aliased)),
        out_shape=out_shape,
        grid=(batch // nb, nc),
        input_output_aliases=aliases,
        in_specs=[*io_specs,
                  pl.BlockSpec((depth, BR_W), lambda b, ci: (0, 0)),
                  pl.BlockSpec((1, BR_W), lambda b, ci: (0, 0)),
                  pl.BlockSpec((nb * c, nb * c), lambda b, ci: (0, 0)),
                  pl.BlockSpec((None, nb, HG_HEADS, HG_D, HG_D), lambda b, ci: (state_layer, b, 0, 0, 0))],
        out_specs=out_specs,
        scratch_shapes=[pltpu.VMEM((nb, HG_HEADS, HG_D, HG_D), F32)],
        compiler_params=_cparams(2),
        name="hgrn2",
    )(*aliased, pc, lbp, ng, tri, hg0)


def _gdn_kernel(*refs, c, nb, ns, n_alias):
    refs = refs[n_alias:]
    tok_refs = refs[:1]
    (cs0_ref, s0_ref, cw_ref, alog_ref, dtb_ref, ng_ref, mc_ref, mst_ref, tri_ref,
     o_ref, s_ref, cs_ref, ext_scr) = refs[1:]
    ci = pl.program_id(1)
    keep = GD_CONV - 1

    @pl.when(ci == 0)
    def _init():
        for j in range(nb):
            ext_scr[j, SUBLANES - keep:SUBLANES, :] = cs0_ref[j]
        s_ref[...] = s0_ref[...]

    convs = []
    for j in range(nb):
        x = tok_refs[0][j * c:(j + 1) * c, 0:GD_QKV]
        ext_scr[j, SUBLANES:SUBLANES + c, :] = x
        conv = ext_scr[j, SUBLANES - keep:SUBLANES - keep + c, :] * cw_ref[0:1, :]
        for i in range(1, keep):
            conv = conv + ext_scr[j, SUBLANES - keep + i:SUBLANES - keep + i + c, :] * cw_ref[i:i + 1, :]
        convs.append(conv + x * cw_ref[keep:keep + 1, :])
        tail = ext_scr[j, SUBLANES + c - keep:SUBLANES + c, :]
        ext_scr[j, SUBLANES - keep:SUBLANES, :] = tail
        cs_ref[j] = tail

    act = _silu(jnp.concatenate(convs, axis=0))
    q = act[:, 0:BR_W]
    k = act[:, BR_W:2 * BR_W]
    v = act[:, 2 * BR_W:3 * BR_W]
    gate_ba = jnp.concatenate([t[:, GD_QKV:GD_COLS_PAD] for t in tok_refs], axis=0)
    g_d = gate_ba[:, 0:BR_W]
    ba = gate_ba[:, BR_W:BR_W + LANES]
    beta_all = _sigmoid(ba)
    g_all = -jnp.exp(alog_ref[...]) * _softplus(ba + dtb_ref[...])

    cum_all = _mm_mask(tri_ref[...], g_all)

    ur = ns * c
    rows = GD_HEADS * ur
    stack = lambda parts: jnp.concatenate(parts, axis=0)
    l2n = lambda z: z * lax.rsqrt(jnp.sum(z * z, axis=-1, keepdims=True) + 1e-6)
    head = lambda z, h: z[:, h * GD_D:(h + 1) * GD_D]
    qn = jnp.concatenate([l2n(head(q, h)) for h in range(GD_HEADS)], axis=1) * (GD_D ** -0.5)
    kn = jnp.concatenate([l2n(head(k, h)) for h in range(GD_HEADS)], axis=1)
    units = range(nb // ns)
    heads = range(GD_HEADS)
    stack_u = lambda z, un: stack([head(z[un * ur:(un + 1) * ur], h) for h in heads])
    col_u = lambda z, lane0, un: stack([z[un * ur:(un + 1) * ur, lane0 + h:lane0 + h + 1] for h in heads])
    q4 = [stack_u(qn, un) for un in units]
    k4 = [stack_u(kn, un) for un in units]
    v4 = [stack_u(v, un) for un in units]
    beta = [col_u(beta_all, 0, un) for un in units]
    cum = [col_u(cum_all, GD_HEADS, un) for un in units]
    causal = mc_ref[...] > 0.5
    m_strict = mst_ref[...]
    eye = _iota((rows, rows), 0) == _iota((rows, rows), 1)
    ones = jnp.ones((rows, rows), BF16)
    cum_t = [jnp.broadcast_to(cm, (rows, rows)) for cm in cum]
    cum_s = [_mm_mask(ones, jnp.where(eye, ct, 0.0)) for ct in cum_t]
    dmat = [jnp.where(causal, jnp.exp(jnp.where(causal, ct - cs_, 0.0)), 0.0) for ct, cs_ in zip(cum_t, cum_s)]
    a_mat = [beta[un] * _mm_nt(k4[un], k4[un]) * (dmat[un] * m_strict) for un in units]
    inv = _unit_lower_inverse([-a for a in a_mat], c)
    e_cum = [jnp.exp(cm) for cm in cum]
    sol = [_mm(inv[un], jnp.concatenate([(beta[un] * e_cum[un]) * k4[un], beta[un] * v4[un]], axis=1)) for un in units]
    blocks = [(h, i) for h in heads for i in range(ns)]
    rows_of = lambda h, i: slice((h * ns + i) * c, (h * ns + i + 1) * c)
    qw = [[_mm(jnp.concatenate([q4[un][rows_of(h, i)], sol[un][rows_of(h, i), 0:GD_D]], axis=0), s_ref[un * ns + i, h])
           for h, i in blocks] for un in units]
    delta = [sol[un][:, GD_D:2 * GD_D] - stack([z[c:2 * c] for z in qw[un]]) for un in units]
    o4 = [e_cum[un] * stack([z[0:c] for z in qw[un]]) + _mm(_mm_nt(q4[un], k4[un]) * dmat[un], delta[un])
          for un in units]
    for un in units:
        for h, i in blocks:
            hs = rows_of(h, i)
            cum_h = cum[un][hs]
            cum_last = cum_h[c - 1:c, :]
            s_ref[un * ns + i, h] = (jnp.exp(cum_last) * s_ref[un * ns + i, h]
                                     + _mm_tn(k4[un][hs] * jnp.exp(cum_last - cum_h), delta[un][hs]))
    for h in heads:
        sl = slice(h * GD_D, (h + 1) * GD_D)
        o_h = stack([o4[un][h * ur:(h + 1) * ur] for un in units])
        o_n = o_h * lax.rsqrt(jnp.mean(o_h * o_h, axis=-1, keepdims=True) + NORM_EPS) * ng_ref[:, sl]
        o_ref[:, sl] = (o_n * _silu(g_d[:, sl])).astype(o_ref.dtype)


def _gdn(pd, conv0, gd0, layer, cw, alog, dtb, ng, *, o_prev, batch, seq, c, nb, ns, row_block0, stacked):
    nc = seq // c
    assert batch % nb == 0 and nb % ns == 0
    keep = GD_CONV - 1
    rows = GD_HEADS * ns * c
    m_causal, m_strict = (jnp.asarray(m) for m in _block_masks(GD_HEADS * ns, c))
    tri = jnp.asarray(_block_masks(nb, c)[0]).astype(BF16)
    full = lambda *shape: pl.BlockSpec(shape, lambda b, ci: (0,) * len(shape))
    io_specs, out_shape, out_specs, aliases, aliased = _mixer_io(
        GD_COLS_PAD, [(GD_HEADS, GD_D, GD_D), (keep, GD_QKV)], o_prev=o_prev, batch=batch, c=c, nb=nb, nc=nc,
        row_block0=row_block0, stacked=stacked)
    return pl.pallas_call(
        functools.partial(_gdn_kernel, c=c, nb=nb, ns=ns, n_alias=len(aliased)),
        out_shape=out_shape,
        grid=(batch // nb, nc),
        input_output_aliases=aliases,
        in_specs=[*io_specs,
                  pl.BlockSpec((None, nb, keep, GD_QKV), lambda b, ci: (layer, b, 0, 0)),
                  pl.BlockSpec((None, nb, GD_HEADS, GD_D, GD_D), lambda b, ci: (layer, b, 0, 0, 0)),
                  full(GD_CONV, GD_QKV), full(1, LANES), full(1, LANES), full(1, BR_W),
                  full(rows, rows), full(rows, rows), full(nb * c, nb * c)],
        out_specs=out_specs,
        scratch_shapes=[pltpu.VMEM((nb, SUBLANES + c, GD_QKV), F32)],
        compiler_params=_cparams(2),
        name="gdn",
    )(*aliased, pd, conv0, gd0, cw, alog, dtb, ng, m_causal, m_strict, tri)


def _pick_chunk(seq):
    for c in (64, 56, 48, 40, 32, 24, 16, 8):
        if seq % c == 0:
            return c
    raise ValueError(f"sequence length {seq} is not a multiple of {SUBLANES}")


def _pick_tile(n, candidates):
    for t in candidates:
        if n % t == 0:
            return t
    raise ValueError(f"no tile in {candidates} divides {n}")


def _rope_tables(pos):
    half = RET_DK // 2
    inv = ROPE_BASE ** (-jnp.arange(half, dtype=F32) / half)
    ang = pos.astype(F32)[:, None] * inv[None, :]
    cos, sin = jnp.cos(ang), jnp.sin(ang)
    cos_t = jnp.tile(jnp.concatenate([cos, cos], axis=1), (1, RET_HEADS))
    sin_t = jnp.tile(jnp.concatenate([-sin, sin], axis=1), (1, RET_HEADS))
    return cos_t, sin_t


def kernel(x_prompt, x_sample, state_rwkv_wkv, state_rwkv_shift, state_ret, state_hgrn, state_gdn, state_gdn_conv, meta_tokens, norm_mix, w_in, rw_mu, rw_w0, rw_w2, rw_a0, rw_a2, rw_g2, rw_kk, rw_ka, rw_rk, rw_ln_g, rw_ln_b, hg_lb, hg_norm_g, gd_conv, gd_a_log, gd_dt_bias, gd_norm_g, w_branch, w_out, norm_ffn, w_up, w_down, norm_final):
    depth = norm_mix.shape[0]
    bp, tq, d = x_prompt.shape
    tp = tq + N_META
    bs, ts, _ = x_sample.shape
    n_p, n_s = bp * tp, bs * ts
    n_tok = n_p + n_s
    cp, cs = _pick_chunk(tp), _pick_chunk(ts)
    tm = _pick_tile(n_tok, (464, 512, 256, 232, 128, 64, 32, 16, 8))
    tm_in = _pick_tile(n_tok, (928, 464, 512, 256, 232, 128, 64, 32, 16, 8))
    nb_s = _pick_tile(bs, (8, 4, 2, 1))
    nb_p = _pick_tile(bp, (4, 2, 1))
    ncp, gp = tp // cp, bp // nb_p
    assert cs == ts and n_p % (nb_s * ts) == 0

    meta = jnp.broadcast_to(meta_tokens.astype(F32)[None], (bp, N_META, d))
    xp = jnp.concatenate([meta, x_prompt], axis=1).reshape(gp, nb_p, ncp, cp, d)
    x = jnp.concatenate([xp.transpose(0, 2, 1, 3, 4).reshape(n_p, d), x_sample.reshape(n_s, d)], axis=0)

    cos_p, sin_p = _rope_tables(jnp.arange(tp, dtype=jnp.int32))
    cos_s, sin_s = _rope_tables(PAST_LEN + jnp.arange(ts, dtype=jnp.int32))

    head_of = jnp.arange(BR_W) // RW_HEAD
    bd = (head_of[:, None] == head_of[None, :]).astype(BF16)

    zeros = lambda *s: jnp.zeros((1, bp) + s, F32)
    z_wkv, z_shift = zeros(RW_HEADS, RW_HEAD, RW_HEAD), zeros(1, RW_COLS)
    z_ret, z_hg, z_gd = zeros(RET_HEADS, RET_DK, RET_DV), zeros(HG_HEADS, HG_D, HG_D), zeros(GD_HEADS, GD_D, GD_D)
    z_conv = zeros(GD_CONV - 1, GD_QKV)
    shift_s = state_rwkv_shift.reshape(depth, bs, 1, RW_COLS)

    assert w_in.shape[2] == IN_G + GATE_COLS
    w_a, w_b, w_c, w_d, w_g = _w_in_prep(jnp.swapaxes(w_in, 1, 2), LANES)
    wb_all, wo_all = w_branch.astype(BF16), w_out.astype(BF16)
    wu_all, wd_all = w_up.astype(BF16), w_down.astype(BF16)

    new_p = [[] for _ in range(6)]
    s_stacks = [jnp.zeros((depth, bs) + tail, F32) for tail in (
        (RW_HEADS, RW_HEAD, RW_HEAD), (1, RW_COLS), (RET_HEADS, RET_DK, RET_DV), (HG_HEADS, HG_D, HG_D),
        (GD_HEADS, GD_D, GD_D), (GD_CONV - 1, GD_QKV))]
    h = _rmsnorm(x, norm_mix[0], tm, BF16)
    for l in range(depth):
        pa = _matmul(h, w_a, l, tm_in, 896, "in_proj_a")
        pb = _matmul(h, w_b, l, tm_in, 768, "in_proj_b")
        pc = _matmul(h, w_c, l, tm_in, 1024, "in_proj_c")
        pd = _matmul(h, w_d, l, tm_in, GD_COLS_PAD, "in_proj_d")
        gates = _matmul(h, w_g, l, tm_in, 1024, "in_proj_g", gate=True)

        row1 = lambda a: a.reshape(1, -1)
        pad_rows = lambda a, top: jnp.concatenate(
            [jnp.zeros((top, BR_W), F32), a, jnp.zeros((LANES - top - a.shape[0], BR_W), F32)], axis=0).astype(BF16)
        rw_prm = (row1(rw_mu[l]), row1(rw_w0[l]), pad_rows(rw_w2[l], 0), row1(rw_a0[l]), pad_rows(rw_a2[l], 64),
                  rw_g2[l].astype(BF16), row1(rw_kk[l]), row1(rw_ka[l]), row1(rw_rk[l]), row1(rw_ln_g[l]),
                  row1(rw_ln_b[l]), bd)
        lane_pad = lambda a: jnp.concatenate(
            [jnp.zeros((GD_HEADS,), F32), a, jnp.zeros((LANES - 2 * GD_HEADS,), F32)]).reshape(1, LANES)
        gd_prm = (gd_conv[l], lane_pad(gd_a_log[l]), lane_pad(gd_dt_bias[l]), row1(gd_norm_g[l]))
        hg_g = row1(hg_norm_g[l])

        oa = ob = oc = od = jnp.zeros((n_tok, BR_W), BF16)
        for grp, (batch, seq, c, nb, ns, rb0, sts, lay, cos, sin) in enumerate((
                (bp, tp, cp, nb_p, 1, 0, (z_wkv, z_shift, z_ret, z_hg, z_gd, z_conv), 0, cos_p, sin_p),
                (bs, ts, cs, nb_s, nb_s, n_p // (nb_s * ts),
                 (state_rwkv_wkv, shift_s, state_ret, state_hgrn, state_gdn, state_gdn_conv), l, cos_s, sin_s))):
            kw = dict(batch=batch, seq=seq, c=c, nb=nb, row_block0=rb0)
            stk = lambda *prev: None if grp == 0 else (depth, l, prev)
            oa, wkv, shift = _rwkv(pa, sts[1], sts[0], lay, rw_prm, o_prev=oa, ns=ns, stacked=stk(*s_stacks[0:2]), **kw)
            ob, ret = _ret(pb, cos, sin, sts[2], lay, o_prev=ob, stacked=stk(*s_stacks[2:3]), **kw)
            oc, hg = _hgrn(pc, hg_lb, hg_g, sts[3], lay, l, o_prev=oc, stacked=stk(*s_stacks[3:4]), **kw)
            od, gd, conv = _gdn(pd, sts[5], sts[4], lay, *gd_prm, o_prev=od, ns=ns, stacked=stk(*s_stacks[4:6]), **kw)
            if grp == 0:
                for i, st in enumerate((wkv, shift.reshape(batch, RW_COLS), ret, hg, gd, conv)):
                    new_p[i].append(st)
            else:
                s_stacks = [wkv, shift, ret, hg, gd, conv]
        x = _merge(x, (oa, ob, oc, od), gates, wb_all, wo_all, l, tm_in, 256)
        last = l == depth - 1
        res = _ffn(x, norm_ffn[l], norm_final if last else norm_mix[l + 1], wu_all, wd_all, l, tm_in, 512, last)
        if last:
            y = res
        else:
            x, h = res

    y_prompt = y[:n_p].reshape(gp, ncp, nb_p, cp, d).transpose(0, 2, 1, 3, 4).reshape(bp, tp, d)[:, N_META:]
    y_sample = y[n_p:].reshape(bs, ts, d)
    p_states = tuple(jnp.stack(z) for z in new_p)
    s_wkv, s_shift, s_ret, s_hg, s_gd, s_conv = s_stacks
    return (y_prompt, y_sample) + p_states + (s_wkv, s_shift.reshape(depth, bs, RW_COLS), s_ret, s_hg, s_gd, s_conv)
```

```python
import functools
import math

import jax
import jax.numpy as jnp
import numpy as np
from jax import lax
from jax.experimental import pallas as pl
from jax.experimental.pallas import tpu as pltpu

F32 = jnp.float32
BF16 = jnp.bfloat16

N_META = 16
PAST_LEN = 16384
NORM_EPS = 1e-6
RW_LN_EPS = 64e-5
ROPE_BASE = 10000.0

D_MODEL = 2048
BR_W = 512
RW_HEADS, RW_HEAD = 8, 64
RW_COLS = 1792
RW_GROUP = 4
RW_GW = RW_GROUP * RW_HEAD
RET_HEADS, RET_DK, RET_DV = 4, 64, 128
RET_COLS = 1536
HG_HEADS, HG_D = 4, 128
HG_COLS = 2048
GD_HEADS, GD_D = 4, 128
GD_CONV = 4
GD_QKV = 1536
GD_COLS_PAD = 2176

LANES = 128
SUBLANES = 8
VMEM_LIMIT = 56 * 1024 * 1024


def _cparams(n_axes):
    return pltpu.CompilerParams(dimension_semantics=("arbitrary",) * n_axes, vmem_limit_bytes=VMEM_LIMIT)


def _dg(a, b, ca, cb):
    return lax.dot_general(a, b, (((ca,), (cb,)), ((), ())), preferred_element_type=F32)


def _mm(a, b):
    return _dg(a.astype(BF16), b.astype(BF16), 1, 0)


def _mm_nt(a, b):
    return _dg(a.astype(BF16), b.astype(BF16), 1, 1)


def _mm_tn(a, b):
    return _dg(a.astype(BF16), b.astype(BF16), 0, 0)


def _split(x, n):
    parts, r = [], x
    for _ in range(n):
        p = r.astype(BF16)
        parts.append(p)
        r = r - p.astype(F32)
    return parts


def _mm_mask(m01, x):
    x0, x1, x2 = _split(x, 3)
    return _dg(m01, x0, 1, 0) + (_dg(m01, x1, 1, 0) + _dg(m01, x2, 1, 0))


def _iota(shape, dim):
    return lax.broadcasted_iota(jnp.int32, shape, dim)


def _softplus(x):
    return jnp.maximum(x, 0.0) + jnp.log(1.0 + jnp.exp(-jnp.abs(x)))


def _sigmoid(x):
    return jax.nn.sigmoid(x)


def _silu(x):
    return x * jax.nn.sigmoid(x)


def _unit_lower_inverse(lows, c):
    shape = lows[0].shape
    eye = (_iota(shape, 0) == _iota(shape, 1)).astype(F32)
    invs = [eye + low for low in lows]
    powers = list(lows)
    covered = 1
    while covered < c - 1:
        powers = [_mm(p, p) for p in powers]
        invs = [inv + _mm(inv, p) for inv, p in zip(invs, powers)]
        covered = 2 * covered + 1
    return invs


def _rmsnorm_kernel(x_ref, g_ref, o_ref):
    x = x_ref[...]
    y = x * lax.rsqrt(jnp.mean(x * x, axis=-1, keepdims=True) + NORM_EPS)
    o_ref[...] = (y * g_ref[...]).astype(o_ref.dtype)


def _rmsnorm(x, g, tm, out_dtype):
    n, d = x.shape
    return pl.pallas_call(
        _rmsnorm_kernel,
        out_shape=jax.ShapeDtypeStruct((n, d), out_dtype),
        grid=(n // tm,),
        in_specs=[pl.BlockSpec((tm, d), lambda i: (i, 0)), pl.BlockSpec((1, d), lambda i: (0, 0))],
        out_specs=pl.BlockSpec((tm, d), lambda i: (i, 0)),
        compiler_params=_cparams(1),
        name="rmsnorm",
    )(x, g.reshape(1, d))


def _matmul_kernel(x_ref, w_ref, o_ref):
    o_ref[...] = _dg(x_ref[...], w_ref[...], 1, 1)


def _matmul_sigmoid_kernel(x_ref, w_ref, o_ref):
    o_ref[...] = _sigmoid(_dg(x_ref[...], w_ref[...], 1, 1)).astype(o_ref.dtype)


def _matmul(x, w, layer, tm, tn, name, gate=False):
    n, k = x.shape
    m = w.shape[1]
    return pl.pallas_call(
        _matmul_sigmoid_kernel if gate else _matmul_kernel,
        out_shape=jax.ShapeDtypeStruct((n, m), BF16 if gate else F32),
        grid=(n // tm, m // tn),
        in_specs=[pl.BlockSpec((tm, k), lambda i, j: (i, 0)), pl.BlockSpec((None, tn, k), lambda i, j: (layer, j, 0))],
        out_specs=pl.BlockSpec((tm, tn), lambda i, j: (i, j)),
        compiler_params=_cparams(2),
        name=name,
    )(x, w)


IN_A, IN_B, IN_C = RW_COLS, RW_COLS + RET_COLS, RW_COLS + RET_COLS + HG_COLS
IN_BA = IN_C + GD_QKV
IN_G = IN_BA + 2 * GD_HEADS + BR_W
GATE_COLS = 4 * D_MODEL


def _w_in_prep_kernel(w_ref, a_ref, b_ref, c_ref, d_ref, g_ref):
    cast = lambda lo, hi: w_ref[lo:hi, :].astype(BF16)
    a_ref[...] = cast(0, IN_A)
    b_ref[...] = cast(IN_A, IN_B)
    c_ref[...] = cast(IN_B, IN_C)
    d_ref[0:GD_QKV, :] = cast(IN_C, IN_BA)
    d_ref[GD_QKV:GD_QKV + BR_W, :] = cast(IN_BA + 2 * GD_HEADS, IN_G)
    ba = jnp.concatenate([w_ref[IN_BA:IN_BA + 2 * GD_HEADS, :],
                          jnp.zeros((LANES - 2 * GD_HEADS, w_ref.shape[1]), F32)], axis=0)
    d_ref[GD_QKV + BR_W:GD_COLS_PAD, :] = ba.astype(BF16)
    step = 1024
    for lo in range(0, GATE_COLS, step):
        g_ref[lo:lo + step, :] = cast(IN_G + lo, IN_G + lo + step)


def _w_in_prep(w_in_t, tk):
    depth, n, k = w_in_t.shape
    out = lambda cols: (jax.ShapeDtypeStruct((depth, cols, k), BF16), pl.BlockSpec((None, cols, tk), lambda l, i: (l, 0, i)))
    shapes, specs = zip(*(out(cols) for cols in (RW_COLS, RET_COLS, HG_COLS, GD_COLS_PAD, GATE_COLS)))
    return pl.pallas_call(
        _w_in_prep_kernel,
        out_shape=shapes,
        grid=(depth, k // tk),
        in_specs=[pl.BlockSpec((None, n, tk), lambda l, i: (l, 0, i))],
        out_specs=specs,
        compiler_params=_cparams(2),
        name="w_in_prep",
    )(w_in_t)


def _merge_kernel(x_ref, oa_ref, ob_ref, oc_ref, od_ref, g0_ref, g1_ref, g2_ref, g3_ref, wb_ref, wo_ref, y_ref):
    j = pl.program_id(1)

    @pl.when(j == 0)
    def _init():
        y_ref[...] = x_ref[...]

    merged = None
    for n, (o_ref, g_ref) in enumerate(((oa_ref, g0_ref), (ob_ref, g1_ref), (oc_ref, g2_ref), (od_ref, g3_ref))):
        term = g_ref[...].astype(F32) * _dg(o_ref[...], wb_ref[n], 1, 0)
        merged = term if merged is None else merged + term
    y_ref[...] += _dg(merged.astype(BF16), wo_ref[...], 1, 0)


def _merge(x, outs, pg, wb, wo, layer, tm, tj):
    n, d = x.shape
    nj = d // tj
    o_spec = pl.BlockSpec((tm, BR_W), lambda i, j: (i, 0))
    g_specs = [pl.BlockSpec((tm, tj), functools.partial(lambda i, j, nb: (i, nb * nj + j), nb=nb)) for nb in range(4)]
    return pl.pallas_call(
        _merge_kernel,
        out_shape=jax.ShapeDtypeStruct((n, d), F32),
        grid=(n // tm, nj),
        in_specs=[pl.BlockSpec((tm, d), lambda i, j: (i, 0), pipeline_mode=pl.Buffered(1)),
                  o_spec, o_spec, o_spec, o_spec, *g_specs,
                  pl.BlockSpec((None, 4, BR_W, tj), lambda i, j: (layer, 0, 0, j)),
                  pl.BlockSpec((None, tj, d), lambda i, j: (layer, j, 0))],
        out_specs=pl.BlockSpec((tm, d), lambda i, j: (i, 0)),
        compiler_params=_cparams(2),
        name="merge",
    )(x, *outs, pg, pg, pg, pg, wb, wo)


def _rms(x, g):
    return x * lax.rsqrt(jnp.mean(x * x, axis=-1, keepdims=True) + NORM_EPS) * g


def _ffn_kernel(x_ref, g_ref, gn_ref, wu_ref, wg_ref, wd_ref, *out_and_scratch, last):
    *outs, h_ref = out_and_scratch
    y_ref = outs[0]
    j = pl.program_id(1)

    @pl.when(j == 0)
    def _init():
        x = x_ref[...]
        h_ref[...] = _rms(x, g_ref[...]).astype(BF16)
        y_ref[...] = x

    h = h_ref[...]
    up = _dg(h, wu_ref[...], 1, 0)
    gate = _dg(h, wg_ref[...], 1, 0)
    y_ref[...] += _dg((_silu(gate) * up).astype(BF16), wd_ref[...], 1, 0)

    @pl.when(j == pl.num_programs(1) - 1)
    def _fin():
        if last:
            y_ref[...] = _rms(y_ref[...], gn_ref[...])
        else:
            outs[1][...] = _rms(y_ref[...], gn_ref[...]).astype(BF16)


def _ffn(x, g, g_next, w_up, w_down, layer, tm, tf, last):
    n, d = x.shape
    ff = w_down.shape[1]
    nf = ff // tf
    row_spec = pl.BlockSpec((tm, d), lambda i, j: (i, 0))
    vec_spec = pl.BlockSpec((1, d), lambda i, j: (0, 0))
    if last:
        out_shape, out_specs = jax.ShapeDtypeStruct((n, d), F32), row_spec
    else:
        out_shape = (jax.ShapeDtypeStruct((n, d), F32), jax.ShapeDtypeStruct((n, d), BF16))
        out_specs = (row_spec, row_spec)
    return pl.pallas_call(
        functools.partial(_ffn_kernel, last=last),
        out_shape=out_shape,
        grid=(n // tm, nf),
        in_specs=[pl.BlockSpec((tm, d), lambda i, j: (i, 0), pipeline_mode=pl.Buffered(1)), vec_spec, vec_spec,
                  pl.BlockSpec((None, d, tf), lambda i, j: (layer, 0, j)),
                  pl.BlockSpec((None, d, tf), lambda i, j: (layer, 0, nf + j)),
                  pl.BlockSpec((None, tf, d), lambda i, j: (layer, j, 0))],
        out_specs=out_specs,
        scratch_shapes=[pltpu.VMEM((tm, d), BF16)],
        compiler_params=_cparams(2),
        name="ffn",
    )(x, g.reshape(1, d), g_next.reshape(1, d), w_up, w_up, w_down)


def _rwkv_kernel(*refs, c, nb, ns, n_chunks, rw_pad, n_alias):
    refs = refs[n_alias:]
    tok_refs = refs[:1]
    (sh0_ref, s0_ref, mu_ref, w0_ref, w2_ref, a0_ref, a2_ref, g2_ref, kkw_ref, ka_ref, rk_ref, lng_ref, lnb_ref,
     bd_ref, hm_ref, ms_ref, mi_ref, tri_ref, o_ref, s_ref, sh_ref, xs_scr, sbd_scr) = refs[1:]
    ci = pl.program_id(1)
    seqs = range(nb)
    head_block = lambda h: (h // RW_GROUP, slice((h % RW_GROUP) * RW_HEAD, (h % RW_GROUP + 1) * RW_HEAD))

    @pl.when(ci == 0)
    def _init():
        sbd_scr[...] = jnp.zeros_like(sbd_scr)
        for j in seqs:
            xs_scr[j, SUBLANES - 1:SUBLANES, :] = sh0_ref[j]
            for h in range(RW_HEADS):
                g, sl = head_block(h)
                sbd_scr[j, g, sl, sl] = s0_ref[j, h]

    pa = jnp.concatenate([t[...] for t in tok_refs], axis=0)
    prevs = []
    for j in seqs:
        pa_j = pa[j * c:(j + 1) * c, :]
        xs_scr[j, SUBLANES:SUBLANES + c, :] = pa_j
        prevs.append(xs_scr[j, SUBLANES - 1:SUBLANES - 1 + c, :])
        last = pa_j[c - 1:c, :]
        xs_scr[j, SUBLANES - 1:SUBLANES, :] = last
        sh_ref[j] = last
    prev = jnp.concatenate(prevs, axis=0)

    xm = pa + (prev - pa) * mu_ref[...]
    r = xm[:, 0:512]
    k0 = xm[:, 512:1024]
    v = xm[:, 1024:1536]
    wa = xm[:, 1536:1664]
    gl = xm[:, 1664:1792]
    w_log = -_softplus(-(w0_ref[...] + _mm(jnp.tanh(wa), w2_ref[...]))) - 0.5
    ld = -jnp.exp(w_log)
    asig = _sigmoid(a0_ref[...] + _mm(wa, a2_ref[...]))
    g_a = _mm(_sigmoid(gl), g2_ref[...])
    bd = bd_ref[...]
    kkr = k0 * kkw_ref[...]
    kkn = kkr * lax.rsqrt(_mm(kkr * kkr, bd) + 1e-6)
    k1 = k0 * (1.0 + (asig - 1.0) * ka_ref[...])
    a_vec = -kkn
    b_vec = kkn * asig

    cum = _mm_mask(tri_ref[...], ld)
    cum_ends = [cum[(j + 1) * c - 1:(j + 1) * c, :] for j in seqs]
    cum_last = jnp.concatenate([jnp.broadcast_to(e, (c, BR_W)) for e in cum_ends], axis=0)
    a_t = a_vec * jnp.exp(cum - ld)
    r_t = r * jnp.exp(cum)
    inv_p = jnp.exp(-cum)
    b_t = b_vec * inv_p
    k_t = k1 * inv_p
    to_end = jnp.exp(cum_last - cum)
    b_e = b_vec * to_end
    k_e = k1 * to_end

    seq_rows = RW_GROUP * c
    rows = ns * seq_rows
    pad = [jnp.zeros((rw_pad - rows, RW_GW), F32)] if rw_pad > rows else []
    m_strict = ms_ref[...]
    m_incl = mi_ref[...]
    groups = range(RW_HEADS // RW_GROUP)
    units = [(s0, g) for s0 in range(0, nb, ns) for g in groups]
    lanes = lambda g: slice(g * RW_GW, (g + 1) * RW_GW)
    of_seq = lambda x, i: x[i * seq_rows:(i + 1) * seq_rows]

    def stack(x, unit):
        s0, g = unit
        return jnp.concatenate([x[j * c:(j + 1) * c, lanes(g)] * hm_ref[hh:hh + 1, :]
                                for j in range(s0, s0 + ns) for hh in range(RW_GROUP)], axis=0)

    a4 = [stack(a_t, un) for un in units]
    r4 = [stack(r_t, un) for un in units]
    v4 = [stack(v, un) for un in units]
    gram = [_mm_nt(jnp.concatenate([a4[n], r4[n]], axis=0),
                   jnp.concatenate([stack(b_t, un), *pad, stack(k_t, un), *pad], axis=0))
            for n, un in enumerate(units)]
    from_state = [[_mm_nt(jnp.concatenate([of_seq(a4[n], i), of_seq(r4[n], i)], axis=0), sbd_scr[s0 + i, g])
                   for i in range(ns)] for n, (s0, g) in enumerate(units)]
    fs_a = [jnp.concatenate([f[0:seq_rows] for f in fs], axis=0) for fs in from_state]
    fs_r = [jnp.concatenate([f[seq_rows:2 * seq_rows] for f in fs], axis=0) for fs in from_state]
    low2 = [gm[0:rows] * m_strict for gm in gram]
    rhs_u = [fs_a[n] + _mm(low2[n][:, rw_pad:2 * rw_pad], jnp.concatenate([v4[n], *pad], axis=0))
             for n in range(len(units))]
    inv = _unit_lower_inverse([lw[:, 0:rows] for lw in low2], c)
    u = [_mm(inv[n], rhs_u[n]) for n in range(len(units))]
    o_unit = {}
    for n, (s0, g) in enumerate(units):
        uv = jnp.concatenate([u[n], *pad, v4[n], *pad], axis=0)
        o4 = fs_r[n] + _mm(gram[n][rows:2 * rows] * m_incl, uv)
        for i in range(ns):
            o_j = o4[i * seq_rows:i * seq_rows + c]
            for hh in range(1, RW_GROUP):
                o_j = o_j + o4[i * seq_rows + hh * c:i * seq_rows + (hh + 1) * c]
            o_unit[(s0 + i, g)] = o_j
        b4 = stack(b_e, (s0, g))
        k4 = stack(k_e, (s0, g))
        for i in range(ns):
            ends = jnp.concatenate([of_seq(b4, i), of_seq(k4, i)], axis=0)
            uv_j = jnp.concatenate([of_seq(u[n], i), of_seq(v4[n], i)], axis=0)
            sbd_scr[s0 + i, g] = sbd_scr[s0 + i, g] * jnp.exp(cum_ends[s0 + i][:, lanes(g)]) + _mm_tn(uv_j, ends)

    @pl.when(ci == n_chunks - 1)
    def _fin():
        for j in seqs:
            for h in range(RW_HEADS):
                g, sl = head_block(h)
                s_ref[j, h] = sbd_scr[j, g, sl, sl]

    o = jnp.concatenate([jnp.concatenate([o_unit[(j, g)] for g in groups], axis=1) for j in seqs], axis=0)
    inv_n = 1.0 / RW_HEAD
    mean = _mm(o, bd) * inv_n
    dev = o - mean
    var = _mm(dev * dev, bd) * inv_n
    normed = dev * lax.rsqrt(var + RW_LN_EPS) * lng_ref[...] + lnb_ref[...]
    bonus = _mm(r * k1 * rk_ref[...], bd) * v
    out = (normed + bonus) * g_a
    o_ref[...] = out.astype(o_ref.dtype)


def _block_masks(n_blocks, c):
    r = np.arange(n_blocks * c)
    same = (r[:, None] // c) == (r[None, :] // c)
    causal = (same & (r[None, :] <= r[:, None])).astype(np.float32)
    strict = (same & (r[None, :] < r[:, None])).astype(np.float32)
    return causal, strict


def _rwkv_masks(c, nb):
    rows = nb * RW_GROUP * c
    rw_pad = -(-rows // LANES) * LANES
    lane_head = np.arange(RW_GW) // RW_HEAD
    hm = (lane_head[None, :] == np.arange(RW_GROUP)[:, None]).astype(np.float32)
    incl, strict = _block_masks(nb * RW_GROUP, c)
    widen = lambda m: np.tile(np.pad(m, ((0, 0), (0, rw_pad - rows))), (1, 2))
    return rw_pad, jnp.asarray(hm), jnp.asarray(widen(strict)), jnp.asarray(widen(incl))


ANY_SPEC = pl.BlockSpec(memory_space=pl.ANY)


def _mixer_io(cols, tails, *, o_prev, batch, c, nb, nc, row_block0, stacked):
    zeros = lambda t: (0,) * len(t)
    if stacked is None:
        prev = []
        st_shapes = [jax.ShapeDtypeStruct((batch,) + t, F32) for t in tails]
        st_specs = [pl.BlockSpec((nb,) + t, functools.partial(lambda b, ci, z: (b,) + z, z=zeros(t))) for t in tails]
    else:
        depth, layer, prev = stacked
        prev = list(prev)
        st_shapes = [jax.ShapeDtypeStruct((depth, batch) + t, F32) for t in tails]
        st_specs = [pl.BlockSpec((None, nb) + t, functools.partial(lambda b, ci, z: (layer, b) + z, z=zeros(t)))
                    for t in tails]
    tok_idx = lambda b, ci: (row_block0 + b * nc + ci, 0)
    aliased = [o_prev, *prev]
    in_specs = [ANY_SPEC] * len(aliased) + [pl.BlockSpec((nb * c, cols), tok_idx)]
    out_shape = (jax.ShapeDtypeStruct(o_prev.shape, o_prev.dtype), *st_shapes)
    out_specs = (pl.BlockSpec((nb * c, BR_W), tok_idx), *st_specs)
    return in_specs, out_shape, out_specs, {i: i for i in range(len(aliased))}, aliased


def _rwkv(pa, shift0, wkv0, layer, prm, *, o_prev, batch, seq, c, nb, ns, row_block0, stacked):
    nc = seq // c
    assert batch % nb == 0 and nb % ns == 0
    rw_pad, hm, m_strict, m_incl = _rwkv_masks(c, ns)
    rows = ns * RW_GROUP * c
    tri = jnp.asarray(_block_masks(nb, c)[0]).astype(BF16)
    io_specs, out_shape, out_specs, aliases, aliased = _mixer_io(
        RW_COLS, [(RW_HEADS, RW_HEAD, RW_HEAD), (1, RW_COLS)], o_prev=o_prev, batch=batch, c=c, nb=nb, nc=nc,
        row_block0=row_block0, stacked=stacked)
    kern = functools.partial(_rwkv_kernel, c=c, nb=nb, ns=ns, n_chunks=nc, rw_pad=rw_pad, n_alias=len(aliased))
    vec = lambda n: pl.BlockSpec((1, n), lambda b, ci: (0, 0))
    mat = lambda r_, n: pl.BlockSpec((r_, n), lambda b, ci: (0, 0))
    prm = tuple(prm) + (hm, m_strict, m_incl, tri)
    return pl.pallas_call(
        kern,
        out_shape=out_shape,
        grid=(batch // nb, nc),
        input_output_aliases=aliases,
        in_specs=[*io_specs,
                  pl.BlockSpec((None, nb, 1, RW_COLS), lambda b, ci: (layer, b, 0, 0)),
                  pl.BlockSpec((None, nb, RW_HEADS, RW_HEAD, RW_HEAD), lambda b, ci: (layer, b, 0, 0, 0)),
                  vec(RW_COLS), vec(512), mat(128, 512), vec(512), mat(128, 512), mat(128, 512),
                  vec(512), vec(512), vec(512), vec(512), vec(512), mat(512, 512),
                  mat(RW_GROUP, RW_GW), mat(rows, 2 * rw_pad), mat(rows, 2 * rw_pad), mat(nb * c, nb * c)],
        out_specs=out_specs,
        scratch_shapes=[pltpu.VMEM((nb, SUBLANES + c, RW_COLS), F32),
                        pltpu.VMEM((nb, RW_HEADS // RW_GROUP, RW_GW, RW_GW), F32)],
        compiler_params=_cparams(2),
        name="rwkv7",
    )(*aliased, pa, shift0, wkv0, *prm)


def _ret_kernel(*refs, c, nb, n_alias):
    refs = refs[n_alias:]
    tok_refs = refs[:1]
    cos_ref, sin_ref, mask_ref, s0_ref, o_ref, s_ref = refs[1:]
    ci = pl.program_id(1)

    @pl.when(ci == 0)
    def _init():
        s_ref[...] = s0_ref[...]

    rows = nb * c
    width = RET_HEADS * RET_DK
    half = RET_DK // 2
    cos = jnp.concatenate([cos_ref[...]] * nb, axis=0)
    sin = jnp.concatenate([sin_ref[...]] * nb, axis=0)
    first_half = (_iota((rows, width), 1) & half) == 0

    def rot(x):
        partner = jnp.where(first_half, pltpu.roll(x, width - half, 1), pltpu.roll(x, half, 1))
        return x * cos + partner * sin

    pb = jnp.concatenate([t[...] for t in tok_refs], axis=0)
    q = rot(pb[:, 0:width])
    k = rot(pb[:, width:2 * width]) * (RET_DK ** -0.5)
    v = pb[:, 2 * width:2 * width + BR_W]
    g_b = pb[:, 2 * width + BR_W:2 * width + 2 * BR_W]

    mask = mask_ref[...]
    causal = mask > 0.5
    dist = jnp.where(causal, (_iota((rows, rows), 0) - _iota((rows, rows), 1)).astype(F32), 0.0)
    t1 = jnp.sum(mask, axis=-1, keepdims=True)
    for h in range(RET_HEADS):
        log_gamma = math.log1p(-(2.0 ** (-5.0 - h)))
        dmat = jnp.where(causal, jnp.exp(dist * log_gamma), 0.0)
        q_h = q[:, h * RET_DK:(h + 1) * RET_DK]
        k_h = k[:, h * RET_DK:(h + 1) * RET_DK]
        v_h = v[:, h * RET_DV:(h + 1) * RET_DV]
        k_dec = k_h * jnp.exp((c - t1) * log_gamma)
        from_state = []
        for j in range(nb):
            js = slice(j * c, (j + 1) * c)
            s_h = s_ref[j, h]
            from_state.append(_mm(q_h[js], s_h))
            s_ref[j, h] = math.exp(c * log_gamma) * s_h + _mm_tn(k_dec[js], v_h[js])
        o_h = _mm(_mm_nt(q_h, k_h) * dmat, v_h) + jnp.exp(t1 * log_gamma) * jnp.concatenate(from_state, axis=0)
        o_n = o_h * lax.rsqrt(jnp.mean(o_h * o_h, axis=-1, keepdims=True) + NORM_EPS)
        o_ref[:, h * RET_DV:(h + 1) * RET_DV] = (o_n * _silu(g_b[:, h * RET_DV:(h + 1) * RET_DV])).astype(o_ref.dtype)


def _ret(pb, cos, sin, ret0, layer, *, o_prev, batch, seq, c, nb, row_block0, stacked):
    nc = seq // c
    assert batch % nb == 0
    width = RET_HEADS * RET_DK
    mask = jnp.asarray(_block_masks(nb, c)[0])
    io_specs, out_shape, out_specs, aliases, aliased = _mixer_io(
        RET_COLS, [(RET_HEADS, RET_DK, RET_DV)], o_prev=o_prev, batch=batch, c=c, nb=nb, nc=nc,
        row_block0=row_block0, stacked=stacked)
    return pl.pallas_call(
        functools.partial(_ret_kernel, c=c, nb=nb, n_alias=len(aliased)),
        out_shape=out_shape,
        grid=(batch // nb, nc),
        input_output_aliases=aliases,
        in_specs=[*io_specs,
                  pl.BlockSpec((c, width), lambda b, ci: (ci, 0)),
                  pl.BlockSpec((c, width), lambda b, ci: (ci, 0)),
                  pl.BlockSpec((nb * c, nb * c), lambda b, ci: (0, 0)),
                  pl.BlockSpec((None, nb, RET_HEADS, RET_DK, RET_DV), lambda b, ci: (layer, b, 0, 0, 0))],
        out_specs=out_specs,
        compiler_params=_cparams(2),
        name="retention",
    )(*aliased, pb, cos, sin, mask, ret0)


def _hgrn_kernel(*refs, c, m, nb, layer, n_chunks, n_alias):
    refs = refs[n_alias:]
    tok_refs = refs[:1]
    lbp_ref, ng_ref, tri_ref, s0_ref, o_ref, s_ref, st_scr = refs[1:]
    ci = pl.program_id(1)

    @pl.when(ci == 0)
    def _init():
        for j in range(nb):
            for h in range(HG_HEADS):
                st_scr[j, h] = s0_ref[j, h].T

    lbp = lbp_ref[...]
    e = jnp.exp(lbp - jnp.max(lbp, axis=0, keepdims=True))
    soft = e / jnp.sum(e, axis=0, keepdims=True)
    lb = jnp.zeros((1, BR_W), F32)
    for j in range(1, layer + 1):
        lb = lb + soft[j:j + 1, :]

    pc = jnp.concatenate([t[...] for t in tok_refs], axis=0)
    q_all = _silu(pc[:, 0:BR_W])
    f_in = pc[:, BR_W:2 * BR_W]
    v_all = pc[:, 2 * BR_W:3 * BR_W]
    g_c = pc[:, 3 * BR_W:4 * BR_W]
    log_f = jnp.log(lb + (1.0 - lb) * _sigmoid(f_in))
    k_all = (1.0 - lb) * _sigmoid(-f_in)
    cum_all = _mm_mask(tri_ref[...], log_f)

    col_m = _iota((m, m), 1)
    row_l = _iota((m, LANES), 0)
    outs = {}
    for j in range(nb):
        js = slice(j * c, (j + 1) * c)
        q, k, v, cum = q_all[js], k_all[js], v_all[js], cum_all[js]
        cum_last = cum[c - 1:c, :]
        q_e = q * jnp.exp(cum)
        k_e = k * jnp.exp(cum_last - cum)
        p_end = jnp.exp(cum_last)
        for h in range(HG_HEADS):
            sl = slice(h * HG_D, (h + 1) * HG_D)
            st = st_scr[j, h]
            o_rows = []
            for i in range(c // m):
                r0 = i * m
                q_i = q[r0:r0 + m, sl]
                c_i = cum[r0:r0 + m, sl]
                k_i = k[r0:r0 + m, sl]
                diag = jnp.zeros((m, m), F32)
                for s in range(m):
                    keep = row_l >= s
                    decay = jnp.exp(jnp.where(keep, c_i - c_i[s:s + 1, :], 0.0))
                    col = jnp.sum(jnp.where(keep, q_i * k_i[s:s + 1, :] * decay, 0.0), axis=-1, keepdims=True)
                    diag = jnp.where(col_m == s, col, diag)
                o_i = _mm(diag, v[r0:r0 + m, sl])
                if i > 0:
                    c_ref = cum[r0 - 1:r0, sl]
                    q_s = q_i * jnp.exp(c_i - c_ref)
                    k_s = k[0:r0, sl] * jnp.exp(c_ref - cum[0:r0, sl])
                    o_i = o_i + _mm(_mm_nt(q_s, k_s), v[0:r0, sl])
                o_rows.append(o_i)
            outs[(j, h)] = _mm_nt(q_e[:, sl], st) + jnp.concatenate(o_rows, axis=0)
            st_scr[j, h] = st * p_end[:, sl] + _mm_tn(v[:, sl], k_e[:, sl])
    for h in range(HG_HEADS):
        sl = slice(h * HG_D, (h + 1) * HG_D)
        o_h = jnp.concatenate([outs[(j, h)] for j in range(nb)], axis=0)
        o_n = o_h * lax.rsqrt(jnp.mean(o_h * o_h, axis=-1, keepdims=True) + NORM_EPS) * ng_ref[:, sl]
        o_ref[:, sl] = (o_n * _silu(g_c[:, sl])).astype(o_ref.dtype)

    @pl.when(ci == n_chunks - 1)
    def _fin():
        for j in range(nb):
            for h in range(HG_HEADS):
                s_ref[j, h] = st_scr[j, h].T


def _hgrn(pc, lbp, ng, hg0, state_layer, layer, *, o_prev, batch, seq, c, nb, row_block0, stacked):
    nc = seq // c
    assert batch % nb == 0
    m = 16 if c % 16 == 0 else SUBLANES
    depth = lbp.shape[0]
    tri = jnp.asarray(_block_masks(nb, c)[0]).astype(BF16)
    io_specs, out_shape, out_specs, aliases, aliased = _mixer_io(
        HG_COLS, [(HG_HEADS, HG_D, HG_D)], o_prev=o_prev, batch=batch, c=c, nb=nb, nc=nc,
        row_block0=row_block0, stacked=stacked)
    return pl.pallas_call(
        functools.partial(_hgrn_kernel, c=c, m=m, nb=nb, layer=layer, n_chunks=nc, n_alias=len(aliased)),
        out_shape=out_shape,
        grid=(batch // nb, nc),
        input_output_aliases=aliases,
        in_specs=[*io_specs,
                  pl.BlockSpec((depth, BR_W), lambda b, ci: (0, 0)),
                  pl.BlockSpec((1, BR_W), lambda b, ci: (0, 0)),
                  pl.BlockSpec((nb * c, nb * c), lambda b, ci: (0, 0)),
                  pl.BlockSpec((None, nb, HG_HEADS, HG_D, HG_D), lambda b, ci: (state_layer, b, 0, 0, 0))],
        out_specs=out_specs,
        scratch_shapes=[pltpu.VMEM((nb, HG_HEADS, HG_D, HG_D), F32)],
        compiler_params=_cparams(2),
        name="hgrn2",
    )(*aliased, pc, lbp, ng, tri, hg0)


def _gdn_kernel(*refs, c, nb, ns, n_alias):
    refs = refs[n_alias:]
    tok_refs = refs[:1]
    (cs0_ref, s0_ref, cw_ref, alog_ref, dtb_ref, ng_ref, mc_ref, mst_ref, tri_ref,
     o_ref, s_ref, cs_ref, ext_scr) = refs[1:]
    ci = pl.program_id(1)
    keep = GD_CONV - 1

    @pl.when(ci == 0)
    def _init():
        for j in range(nb):
            ext_scr[j, SUBLANES - keep:SUBLANES, :] = cs0_ref[j]
        s_ref[...] = s0_ref[...]

    convs = []
    for j in range(nb):
        x = tok_refs[0][j * c:(j + 1) * c, 0:GD_QKV]
        ext_scr[j, SUBLANES:SUBLANES + c, :] = x
        conv = ext_scr[j, SUBLANES - keep:SUBLANES - keep + c, :] * cw_ref[0:1, :]
        for i in range(1, keep):
            conv = conv + ext_scr[j, SUBLANES - keep + i:SUBLANES - keep + i + c, :] * cw_ref[i:i + 1, :]
        convs.append(conv + x * cw_ref[keep:keep + 1, :])
        tail = ext_scr[j, SUBLANES + c - keep:SUBLANES + c, :]
        ext_scr[j, SUBLANES - keep:SUBLANES, :] = tail
        cs_ref[j] = tail

    act = _silu(jnp.concatenate(convs, axis=0))
    q = act[:, 0:BR_W]
    k = act[:, BR_W:2 * BR_W]
    v = act[:, 2 * BR_W:3 * BR_W]
    gate_ba = jnp.concatenate([t[:, GD_QKV:GD_COLS_PAD] for t in tok_refs], axis=0)
    g_d = gate_ba[:, 0:BR_W]
    ba = gate_ba[:, BR_W:BR_W + LANES]
    beta_all = _sigmoid(ba)
    g_all = -jnp.exp(alog_ref[...]) * _softplus(ba + dtb_ref[...])

    cum_all = _mm_mask(tri_ref[...], g_all)

    ur = ns * c
    rows = GD_HEADS * ur
    stack = lambda parts: jnp.concatenate(parts, axis=0)
    l2n = lambda z: z * lax.rsqrt(jnp.sum(z * z, axis=-1, keepdims=True) + 1e-6)
    head = lambda z, h: z[:, h * GD_D:(h + 1) * GD_D]
    qn = jnp.concatenate([l2n(head(q, h)) for h in range(GD_HEADS)], axis=1) * (GD_D ** -0.5)
    kn = jnp.concatenate([l2n(head(k, h)) for h in range(GD_HEADS)], axis=1)
    units = range(nb // ns)
    heads = range(GD_HEADS)
    stack_u = lambda z, un: stack([head(z[un * ur:(un + 1) * ur], h) for h in heads])
    col_u = lambda z, lane0, un: stack([z[un * ur:(un + 1) * ur, lane0 + h:lane0 + h + 1] for h in heads])
    q4 = [stack_u(qn, un) for un in units]
    k4 = [stack_u(kn, un) for un in units]
    v4 = [stack_u(v, un) for un in units]
    beta = [col_u(beta_all, 0, un) for un in units]
    cum = [col_u(cum_all, GD_HEADS, un) for un in units]
    causal = mc_ref[...] > 0.5
    m_strict = mst_ref[...]
    eye = _iota((rows, rows), 0) == _iota((rows, rows), 1)
    ones = jnp.ones((rows, rows), BF16)
    cum_t = [jnp.broadcast_to(cm, (rows, rows)) for cm in cum]
    cum_s = [_mm_mask(ones, jnp.where(eye, ct, 0.0)) for ct in cum_t]
    dmat = [jnp.where(causal, jnp.exp(jnp.where(causal, ct - cs_, 0.0)), 0.0) for ct, cs_ in zip(cum_t, cum_s)]
    a_mat = [beta[un] * _mm_nt(k4[un], k4[un]) * (dmat[un] * m_strict) for un in units]
    inv = _unit_lower_inverse([-a for a in a_mat], c)
    e_cum = [jnp.exp(cm) for cm in cum]
    sol = [_mm(inv[un], jnp.concatenate([(beta[un] * e_cum[un]) * k4[un], beta[un] * v4[un]], axis=1)) for un in units]
    blocks = [(h, i) for h in heads for i in range(ns)]
    rows_of = lambda h, i: slice((h * ns + i) * c, (h * ns + i + 1) * c)
    qw = [[_mm(jnp.concatenate([q4[un][rows_of(h, i)], sol[un][rows_of(h, i), 0:GD_D]], axis=0), s_ref[un * ns + i, h])
           for h, i in blocks] for un in units]
    delta = [sol[un][:, GD_D:2 * GD_D] - stack([z[c:2 * c] for z in qw[un]]) for un in units]
    o4 = [e_cum[un] * stack([z[0:c] for z in qw[un]]) + _mm(_mm_nt(q4[un], k4[un]) * dmat[un], delta[un])
          for un in units]
    for un in units:
        for h, i in blocks:
            hs = rows_of(h, i)
            cum_h = cum[un][hs]
            cum_last = cum_h[c - 1:c, :]
            s_ref[un * ns + i, h] = (jnp.exp(cum_last) * s_ref[un * ns + i, h]
                                     + _mm_tn(k4[un][hs] * jnp.exp(cum_last - cum_h), delta[un][hs]))
    for h in heads:
        sl = slice(h * GD_D, (h + 1) * GD_D)
        o_h = stack([o4[un][h * ur:(h + 1) * ur] for un in units])
        o_n = o_h * lax.rsqrt(jnp.mean(o_h * o_h, axis=-1, keepdims=True) + NORM_EPS) * ng_ref[:, sl]
        o_ref[:, sl] = (o_n * _silu(g_d[:, sl])).astype(o_ref.dtype)


def _gdn(pd, conv0, gd0, layer, cw, alog, dtb, ng, *, o_prev, batch, seq, c, nb, ns, row_block0, stacked):
    nc = seq // c
    assert batch % nb == 0 and nb % ns == 0
    keep = GD_CONV - 1
    rows = GD_HEADS * ns * c
    m_causal, m_strict = (jnp.asarray(m) for m in _block_masks(GD_HEADS * ns, c))
    tri = jnp.asarray(_block_masks(nb, c)[0]).astype(BF16)
    full = lambda *shape: pl.BlockSpec(shape, lambda b, ci: (0,) * len(shape))
    io_specs, out_shape, out_specs, aliases, aliased = _mixer_io(
        GD_COLS_PAD, [(GD_HEADS, GD_D, GD_D), (keep, GD_QKV)], o_prev=o_prev, batch=batch, c=c, nb=nb, nc=nc,
        row_block0=row_block0, stacked=stacked)
    return pl.pallas_call(
        functools.partial(_gdn_kernel, c=c, nb=nb, ns=ns, n_alias=len(aliased)),
        out_shape=out_shape,
        grid=(batch // nb, nc),
        input_output_aliases=aliases,
        in_specs=[*io_specs,
                  pl.BlockSpec((None, nb, keep, GD_QKV), lambda b, ci: (layer, b, 0, 0)),
                  pl.BlockSpec((None, nb, GD_HEADS, GD_D, GD_D), lambda b, ci: (layer, b, 0, 0, 0)),
                  full(GD_CONV, GD_QKV), full(1, LANES), full(1, LANES), full(1, BR_W),
                  full(rows, rows), full(rows, rows), full(nb * c, nb * c)],
        out_specs=out_specs,
        scratch_shapes=[pltpu.VMEM((nb, SUBLANES + c, GD_QKV), F32)],
        compiler_params=_cparams(2),
        name="gdn",
    )(*aliased, pd, conv0, gd0, cw, alog, dtb, ng, m_causal, m_strict, tri)


def _pick_chunk(seq):
    for c in (64, 56, 48, 40, 32, 24, 16, 8):
        if seq % c == 0:
            return c
    raise ValueError(f"sequence length {seq} is not a multiple of {SUBLANES}")


def _pick_tile(n, candidates):
    for t in candidates:
        if n % t == 0:
            return t
    raise ValueError(f"no tile in {candidates} divides {n}")


def _rope_tables(pos):
    half = RET_DK // 2
    inv = ROPE_BASE ** (-jnp.arange(half, dtype=F32) / half)
    ang = pos.astype(F32)[:, None] * inv[None, :]
    cos, sin = jnp.cos(ang), jnp.sin(ang)
    cos_t = jnp.tile(jnp.concatenate([cos, cos], axis=1), (1, RET_HEADS))
    sin_t = jnp.tile(jnp.concatenate([-sin, sin], axis=1), (1, RET_HEADS))
    return cos_t, sin_t


def kernel(x_prompt, x_sample, state_rwkv_wkv, state_rwkv_shift, state_ret, state_hgrn, state_gdn, state_gdn_conv, meta_tokens, norm_mix, w_in, rw_mu, rw_w0, rw_w2, rw_a0, rw_a2, rw_g2, rw_kk, rw_ka, rw_rk, rw_ln_g, rw_ln_b, hg_lb, hg_norm_g, gd_conv, gd_a_log, gd_dt_bias, gd_norm_g, w_branch, w_out, norm_ffn, w_up, w_down, norm_final):
    depth = norm_mix.shape[0]
    bp, tq, d = x_prompt.shape
    tp = tq + N_META
    bs, ts, _ = x_sample.shape
    n_p, n_s = bp * tp, bs * ts
    n_tok = n_p + n_s
    cp, cs = _pick_chunk(tp), _pick_chunk(ts)
    tm = _pick_tile(n_tok, (464, 512, 256, 232, 128, 64, 32, 16, 8))
    tm_in = _pick_tile(n_tok, (928, 464, 512, 256, 232, 128, 64, 32, 16, 8))
    nb_s = _pick_tile(bs, (8, 4, 2, 1))
    nb_p = _pick_tile(bp, (4, 2, 1))
    ncp, gp = tp // cp, bp // nb_p
    assert cs == ts and n_p % (nb_s * ts) == 0

    meta = jnp.broadcast_to(meta_tokens.astype(F32)[None], (bp, N_META, d))
    xp = jnp.concatenate([meta, x_prompt], axis=1).reshape(gp, nb_p, ncp, cp, d)
    x = jnp.concatenate([xp.transpose(0, 2, 1, 3, 4).reshape(n_p, d), x_sample.reshape(n_s, d)], axis=0)

    cos_p, sin_p = _rope_tables(jnp.arange(tp, dtype=jnp.int32))
    cos_s, sin_s = _rope_tables(PAST_LEN + jnp.arange(ts, dtype=jnp.int32))

    head_of = jnp.arange(BR_W) // RW_HEAD
    bd = (head_of[:, None] == head_of[None, :]).astype(BF16)

    zeros = lambda *s: jnp.zeros((1, bp) + s, F32)
    z_wkv, z_shift = zeros(RW_HEADS, RW_HEAD, RW_HEAD), zeros(1, RW_COLS)
    z_ret, z_hg, z_gd = zeros(RET_HEADS, RET_DK, RET_DV), zeros(HG_HEADS, HG_D, HG_D), zeros(GD_HEADS, GD_D, GD_D)
    z_conv = zeros(GD_CONV - 1, GD_QKV)
    shift_s = state_rwkv_shift.reshape(depth, bs, 1, RW_COLS)

    assert w_in.shape[2] == IN_G + GATE_COLS
    w_a, w_b, w_c, w_d, w_g = _w_in_prep(jnp.swapaxes(w_in, 1, 2), LANES)
    wb_all, wo_all = w_branch.astype(BF16), w_out.astype(BF16)
    wu_all, wd_all = w_up.astype(BF16), w_down.astype(BF16)

    new_p = [[] for _ in range(6)]
    s_stacks = [jnp.zeros((depth, bs) + tail, F32) for tail in (
        (RW_HEADS, RW_HEAD, RW_HEAD), (1, RW_COLS), (RET_HEADS, RET_DK, RET_DV), (HG_HEADS, HG_D, HG_D),
        (GD_HEADS, GD_D, GD_D), (GD_CONV - 1, GD_QKV))]
    h = _rmsnorm(x, norm_mix[0], tm, BF16)
    for l in range(depth):
        pa = _matmul(h, w_a, l, tm_in, 896, "in_proj_a")
        pb = _matmul(h, w_b, l, tm_in, 768, "in_proj_b")
        pc = _matmul(h, w_c, l, tm_in, 1024, "in_proj_c")
        pd = _matmul(h, w_d, l, tm_in, GD_COLS_PAD, "in_proj_d")
        gates = _matmul(h, w_g, l, tm_in, 2048, "in_proj_g", gate=True)

        row1 = lambda a: a.reshape(1, -1)
        pad_rows = lambda a, top: jnp.concatenate(
            [jnp.zeros((top, BR_W), F32), a, jnp.zeros((LANES - top - a.shape[0], BR_W), F32)], axis=0).astype(BF16)
        rw_prm = (row1(rw_mu[l]), row1(rw_w0[l]), pad_rows(rw_w2[l], 0), row1(rw_a0[l]), pad_rows(rw_a2[l], 64),
                  rw_g2[l].astype(BF16), row1(rw_kk[l]), row1(rw_ka[l]), row1(rw_rk[l]), row1(rw_ln_g[l]),
                  row1(rw_ln_b[l]), bd)
        lane_pad = lambda a: jnp.concatenate(
            [jnp.zeros((GD_HEADS,), F32), a, jnp.zeros((LANES - 2 * GD_HEADS,), F32)]).reshape(1, LANES)
        gd_prm = (gd_conv[l], lane_pad(gd_a_log[l]), lane_pad(gd_dt_bias[l]), row1(gd_norm_g[l]))
        hg_g = row1(hg_norm_g[l])

        if l == 0:
            oa = ob = oc = od = jnp.zeros((n_tok, BR_W), BF16)
        for grp, (batch, seq, c, nb, ns, rb0, sts, lay, cos, sin) in enumerate((
                (bp, tp, cp, nb_p, 1, 0, (z_wkv, z_shift, z_ret, z_hg, z_gd, z_conv), 0, cos_p, sin_p),
                (bs, ts, cs, nb_s, nb_s, n_p // (nb_s * ts),
                 (state_rwkv_wkv, shift_s, state_ret, state_hgrn, state_gdn, state_gdn_conv), l, cos_s, sin_s))):
            kw = dict(batch=batch, seq=seq, c=c, nb=nb, row_block0=rb0)
            stk = lambda *prev: None if grp == 0 else (depth, l, prev)
            oa, wkv, shift = _rwkv(pa, sts[1], sts[0], lay, rw_prm, o_prev=oa, ns=ns, stacked=stk(*s_stacks[0:2]), **kw)
            ob, ret = _ret(pb, cos, sin, sts[2], lay, o_prev=ob, stacked=stk(*s_stacks[2:3]), **kw)
            oc, hg = _hgrn(pc, hg_lb, hg_g, sts[3], lay, l, o_prev=oc, stacked=stk(*s_stacks[3:4]), **kw)
            od, gd, conv = _gdn(pd, sts[5], sts[4], lay, *gd_prm, o_prev=od, ns=ns, stacked=stk(*s_stacks[4:6]), **kw)
            if grp == 0:
                for i, st in enumerate((wkv, shift.reshape(batch, RW_COLS), ret, hg, gd, conv)):
                    new_p[i].append(st)
            else:
                s_stacks = [wkv, shift, ret, hg, gd, conv]
        x = _merge(x, (oa, ob, oc, od), gates, wb_all, wo_all, l, tm_in, 512)
        last = l == depth - 1
        res = _ffn(x, norm_ffn[l], norm_final if last else norm_mix[l + 1], wu_all, wd_all, l, tm_in, 512, last)
        if last:
            y = res
        else:
            x, h = res

    y_prompt = y[:n_p].reshape(gp, ncp, nb_p, cp, d).transpose(0, 2, 1, 3, 4).reshape(bp, tp, d)[:, N_META:]
    y_sample = y[n_p:].reshape(bs, ts, d)
    p_states = tuple(jnp.stack(z) for z in new_p)
    s_wkv, s_shift, s_ret, s_hg, s_gd, s_conv = s_stacks
    return (y_prompt, y_sample) + p_states + (s_wkv, s_shift.reshape(depth, bs, RW_COLS), s_ret, s_hg, s_gd, s_conv)
```

```python
import functools
import math

import jax
import jax.numpy as jnp
import numpy as np
from jax import lax
from jax.experimental import pallas as pl
from jax.experimental.pallas import tpu as pltpu

F32 = jnp.float32
BF16 = jnp.bfloat16

N_META = 16
PAST_LEN = 16384
NORM_EPS = 1e-6
RW_LN_EPS = 64e-5
ROPE_BASE = 10000.0

D_MODEL = 2048
BR_W = 512
RW_HEADS, RW_HEAD = 8, 64
RW_COLS = 1792
RW_GROUP = 4
RW_GW = RW_GROUP * RW_HEAD
RET_HEADS, RET_DK, RET_DV = 4, 64, 128
RET_COLS = 1536
HG_HEADS, HG_D = 4, 128
HG_COLS = 2048
GD_HEADS, GD_D = 4, 128
GD_CONV = 4
GD_QKV = 1536
GD_COLS_PAD = 2176

LANES = 128
SUBLANES = 8
VMEM_LIMIT = 56 * 1024 * 1024


def _cparams(n_axes):
    return pltpu.CompilerParams(dimension_semantics=("arbitrary",) * n_axes, vmem_limit_bytes=VMEM_LIMIT)


def _dg(a, b, ca, cb):
    return lax.dot_general(a, b, (((ca,), (cb,)), ((), ())), preferred_element_type=F32)


def _mm(a, b):
    return _dg(a.astype(BF16), b.astype(BF16), 1, 0)


def _mm_nt(a, b):
    return _dg(a.astype(BF16), b.astype(BF16), 1, 1)


def _mm_tn(a, b):
    return _dg(a.astype(BF16), b.astype(BF16), 0, 0)


def _split(x, n):
    parts, r = [], x
    for _ in range(n):
        p = r.astype(BF16)
        parts.append(p)
        r = r - p.astype(F32)
    return parts


def _mm_mask(m01, x):
    x0, x1, x2 = _split(x, 3)
    return _dg(m01, x0, 1, 0) + (_dg(m01, x1, 1, 0) + _dg(m01, x2, 1, 0))


def _iota(shape, dim):
    return lax.broadcasted_iota(jnp.int32, shape, dim)


def _softplus(x):
    return jnp.maximum(x, 0.0) + jnp.log(1.0 + jnp.exp(-jnp.abs(x)))


def _sigmoid(x):
    return jax.nn.sigmoid(x)


def _silu(x):
    return x * jax.nn.sigmoid(x)


def _unit_lower_inverse(lows, c):
    shape = lows[0].shape
    eye = (_iota(shape, 0) == _iota(shape, 1)).astype(F32)
    invs = [eye + low for low in lows]
    powers = list(lows)
    covered = 1
    while covered < c - 1:
        powers = [_mm(p, p) for p in powers]
        invs = [inv + _mm(inv, p) for inv, p in zip(invs, powers)]
        covered = 2 * covered + 1
    return invs


def _rmsnorm_kernel(x_ref, g_ref, o_ref):
    x = x_ref[...]
    y = x * lax.rsqrt(jnp.mean(x * x, axis=-1, keepdims=True) + NORM_EPS)
    o_ref[...] = (y * g_ref[...]).astype(o_ref.dtype)


def _rmsnorm(x, g, tm, out_dtype):
    n, d = x.shape
    return pl.pallas_call(
        _rmsnorm_kernel,
        out_shape=jax.ShapeDtypeStruct((n, d), out_dtype),
        grid=(n // tm,),
        in_specs=[pl.BlockSpec((tm, d), lambda i: (i, 0)), pl.BlockSpec((1, d), lambda i: (0, 0))],
        out_specs=pl.BlockSpec((tm, d), lambda i: (i, 0)),
        compiler_params=_cparams(1),
        name="rmsnorm",
    )(x, g.reshape(1, d))


def _matmul_kernel(x_ref, w_ref, o_ref):
    o_ref[...] = _dg(x_ref[...], w_ref[...], 1, 1)


def _matmul_sigmoid_kernel(x_ref, w_ref, o_ref):
    o_ref[...] = _sigmoid(_dg(x_ref[...], w_ref[...], 1, 1)).astype(o_ref.dtype)


def _matmul(x, w, layer, tm, tn, name, gate=False):
    n, k = x.shape
    m = w.shape[1]
    return pl.pallas_call(
        _matmul_sigmoid_kernel if gate else _matmul_kernel,
        out_shape=jax.ShapeDtypeStruct((n, m), BF16 if gate else F32),
        grid=(n // tm, m // tn),
        in_specs=[pl.BlockSpec((tm, k), lambda i, j: (i, 0)), pl.BlockSpec((None, tn, k), lambda i, j: (layer, j, 0))],
        out_specs=pl.BlockSpec((tm, tn), lambda i, j: (i, j)),
        compiler_params=_cparams(2),
        name=name,
    )(x, w)


IN_A, IN_B, IN_C = RW_COLS, RW_COLS + RET_COLS, RW_COLS + RET_COLS + HG_COLS
IN_BA = IN_C + GD_QKV
IN_G = IN_BA + 2 * GD_HEADS + BR_W
GATE_COLS = 4 * D_MODEL


def _w_in_prep_kernel(w_ref, a_ref, b_ref, c_ref, d_ref, g_ref):
    cast = lambda lo, hi: w_ref[lo:hi, :].astype(BF16)
    a_ref[...] = cast(0, IN_A)
    b_ref[...] = cast(IN_A, IN_B)
    c_ref[...] = cast(IN_B, IN_C)
    d_ref[0:GD_QKV, :] = cast(IN_C, IN_BA)
    d_ref[GD_QKV:GD_QKV + BR_W, :] = cast(IN_BA + 2 * GD_HEADS, IN_G)
    ba = jnp.concatenate([w_ref[IN_BA:IN_BA + 2 * GD_HEADS, :],
                          jnp.zeros((LANES - 2 * GD_HEADS, w_ref.shape[1]), F32)], axis=0)
    d_ref[GD_QKV + BR_W:GD_COLS_PAD, :] = ba.astype(BF16)
    step = 1024
    for lo in range(0, GATE_COLS, step):
        g_ref[lo:lo + step, :] = cast(IN_G + lo, IN_G + lo + step)


def _w_in_prep(w_in_t, tk):
    depth, n, k = w_in_t.shape
    out = lambda cols: (jax.ShapeDtypeStruct((depth, cols, k), BF16), pl.BlockSpec((None, cols, tk), lambda l, i: (l, 0, i)))
    shapes, specs = zip(*(out(cols) for cols in (RW_COLS, RET_COLS, HG_COLS, GD_COLS_PAD, GATE_COLS)))
    return pl.pallas_call(
        _w_in_prep_kernel,
        out_shape=shapes,
        grid=(depth, k // tk),
        in_specs=[pl.BlockSpec((None, n, tk), lambda l, i: (l, 0, i))],
        out_specs=specs,
        compiler_params=_cparams(2),
        name="w_in_prep",
    )(w_in_t)


def _merge_kernel(x_ref, oa_ref, ob_ref, oc_ref, od_ref, g0_ref, g1_ref, g2_ref, g3_ref, wb_ref, wo_ref, y_ref):
    j = pl.program_id(1)

    @pl.when(j == 0)
    def _init():
        y_ref[...] = x_ref[...]

    merged = None
    for n, (o_ref, g_ref) in enumerate(((oa_ref, g0_ref), (ob_ref, g1_ref), (oc_ref, g2_ref), (od_ref, g3_ref))):
        term = g_ref[...].astype(F32) * _dg(o_ref[...], wb_ref[n], 1, 0)
        merged = term if merged is None else merged + term
    y_ref[...] += _dg(merged.astype(BF16), wo_ref[...], 1, 0)


def _merge(x, outs, pg, wb, wo, layer, tm, tj):
    n, d = x.shape
    nj = d // tj
    o_spec = pl.BlockSpec((tm, BR_W), lambda i, j: (i, 0))
    g_specs = [pl.BlockSpec((tm, tj), functools.partial(lambda i, j, nb: (i, nb * nj + j), nb=nb)) for nb in range(4)]
    return pl.pallas_call(
        _merge_kernel,
        out_shape=jax.ShapeDtypeStruct((n, d), F32),
        grid=(n // tm, nj),
        in_specs=[pl.BlockSpec((tm, d), lambda i, j: (i, 0)), o_spec, o_spec, o_spec, o_spec, *g_specs,
                  pl.BlockSpec((None, 4, BR_W, tj), lambda i, j: (layer, 0, 0, j)),
                  pl.BlockSpec((None, tj, d), lambda i, j: (layer, j, 0))],
        out_specs=pl.BlockSpec((tm, d), lambda i, j: (i, 0)),
        compiler_params=_cparams(2),
        name="merge",
    )(x, *outs, pg, pg, pg, pg, wb, wo)


def _rms(x, g):
    return x * lax.rsqrt(jnp.mean(x * x, axis=-1, keepdims=True) + NORM_EPS) * g


def _ffn_kernel(x_ref, g_ref, gn_ref, wu_ref, wg_ref, wd_ref, *out_and_scratch, last):
    *outs, h_ref = out_and_scratch
    y_ref = outs[0]
    j = pl.program_id(1)

    @pl.when(j == 0)
    def _init():
        x = x_ref[...]
        h_ref[...] = _rms(x, g_ref[...]).astype(BF16)
        y_ref[...] = x

    h = h_ref[...]
    up = _dg(h, wu_ref[...], 1, 0)
    gate = _dg(h, wg_ref[...], 1, 0)
    y_ref[...] += _dg((_silu(gate) * up).astype(BF16), wd_ref[...], 1, 0)

    @pl.when(j == pl.num_programs(1) - 1)
    def _fin():
        if last:
            y_ref[...] = _rms(y_ref[...], gn_ref[...])
        else:
            outs[1][...] = _rms(y_ref[...], gn_ref[...]).astype(BF16)


def _ffn(x, g, g_next, w_up, w_down, layer, tm, tf, last):
    n, d = x.shape
    ff = w_down.shape[1]
    nf = ff // tf
    row_spec = pl.BlockSpec((tm, d), lambda i, j: (i, 0))
    vec_spec = pl.BlockSpec((1, d), lambda i, j: (0, 0))
    if last:
        out_shape, out_specs = jax.ShapeDtypeStruct((n, d), F32), row_spec
    else:
        out_shape = (jax.ShapeDtypeStruct((n, d), F32), jax.ShapeDtypeStruct((n, d), BF16))
        out_specs = (row_spec, row_spec)
    return pl.pallas_call(
        functools.partial(_ffn_kernel, last=last),
        out_shape=out_shape,
        grid=(n // tm, nf),
        in_specs=[row_spec, vec_spec, vec_spec,
                  pl.BlockSpec((None, d, tf), lambda i, j: (layer, 0, j)),
                  pl.BlockSpec((None, d, tf), lambda i, j: (layer, 0, nf + j)),
                  pl.BlockSpec((None, tf, d), lambda i, j: (layer, j, 0))],
        out_specs=out_specs,
        scratch_shapes=[pltpu.VMEM((tm, d), BF16)],
        compiler_params=_cparams(2),
        name="ffn",
    )(x, g.reshape(1, d), g_next.reshape(1, d), w_up, w_up, w_down)


def _rwkv_kernel(*refs, c, nb, ns, n_chunks, rw_pad, n_alias):
    refs = refs[n_alias:]
    tok_refs = refs[:1]
    (sh0_ref, s0_ref, mu_ref, w0_ref, w2_ref, a0_ref, a2_ref, g2_ref, kkw_ref, ka_ref, rk_ref, lng_ref, lnb_ref,
     bd_ref, hm_ref, ms_ref, mi_ref, tri_ref, o_ref, s_ref, sh_ref, xs_scr, sbd_scr) = refs[1:]
    ci = pl.program_id(1)
    seqs = range(nb)
    head_block = lambda h: (h // RW_GROUP, slice((h % RW_GROUP) * RW_HEAD, (h % RW_GROUP + 1) * RW_HEAD))

    @pl.when(ci == 0)
    def _init():
        sbd_scr[...] = jnp.zeros_like(sbd_scr)
        for j in seqs:
            xs_scr[j, SUBLANES - 1:SUBLANES, :] = sh0_ref[j]
            for h in range(RW_HEADS):
                g, sl = head_block(h)
                sbd_scr[j, g, sl, sl] = s0_ref[j, h]

    pa = jnp.concatenate([t[...] for t in tok_refs], axis=0)
    prevs = []
    for j in seqs:
        pa_j = pa[j * c:(j + 1) * c, :]
        xs_scr[j, SUBLANES:SUBLANES + c, :] = pa_j
        prevs.append(xs_scr[j, SUBLANES - 1:SUBLANES - 1 + c, :])
        last = pa_j[c - 1:c, :]
        xs_scr[j, SUBLANES - 1:SUBLANES, :] = last
        sh_ref[j] = last
    prev = jnp.concatenate(prevs, axis=0)

    xm = pa + (prev - pa) * mu_ref[...]
    r = xm[:, 0:512]
    k0 = xm[:, 512:1024]
    v = xm[:, 1024:1536]
    wa = xm[:, 1536:1664]
    gl = xm[:, 1664:1792]
    w_log = -_softplus(-(w0_ref[...] + _mm(jnp.tanh(wa), w2_ref[...]))) - 0.5
    ld = -jnp.exp(w_log)
    asig = _sigmoid(a0_ref[...] + _mm(wa, a2_ref[...]))
    g_a = _mm(_sigmoid(gl), g2_ref[...])
    bd = bd_ref[...]
    kkr = k0 * kkw_ref[...]
    kkn = kkr * lax.rsqrt(_mm(kkr * kkr, bd) + 1e-6)
    k1 = k0 * (1.0 + (asig - 1.0) * ka_ref[...])
    a_vec = -kkn
    b_vec = kkn * asig

    cum = _mm_mask(tri_ref[...], ld)
    cum_ends = [cum[(j + 1) * c - 1:(j + 1) * c, :] for j in seqs]
    cum_last = jnp.concatenate([jnp.broadcast_to(e, (c, BR_W)) for e in cum_ends], axis=0)
    a_t = a_vec * jnp.exp(cum - ld)
    r_t = r * jnp.exp(cum)
    inv_p = jnp.exp(-cum)
    b_t = b_vec * inv_p
    k_t = k1 * inv_p
    to_end = jnp.exp(cum_last - cum)
    b_e = b_vec * to_end
    k_e = k1 * to_end

    seq_rows = RW_GROUP * c
    rows = ns * seq_rows
    pad = [jnp.zeros((rw_pad - rows, RW_GW), F32)] if rw_pad > rows else []
    m_strict = ms_ref[...] > 0.5
    m_incl = mi_ref[...] > 0.5
    groups = range(RW_HEADS // RW_GROUP)
    units = [(s0, g) for s0 in range(0, nb, ns) for g in groups]
    lanes = lambda g: slice(g * RW_GW, (g + 1) * RW_GW)
    of_seq = lambda x, i: x[i * seq_rows:(i + 1) * seq_rows]

    def stack(x, unit):
        s0, g = unit
        return jnp.concatenate([x[j * c:(j + 1) * c, lanes(g)] * hm_ref[hh:hh + 1, :]
                                for j in range(s0, s0 + ns) for hh in range(RW_GROUP)], axis=0)

    a4 = [stack(a_t, un) for un in units]
    r4 = [stack(r_t, un) for un in units]
    v4 = [stack(v, un) for un in units]
    gram = [_mm_nt(jnp.concatenate([a4[n], r4[n]], axis=0),
                   jnp.concatenate([stack(b_t, un), *pad, stack(k_t, un), *pad], axis=0))
            for n, un in enumerate(units)]
    from_state = [[_mm_nt(jnp.concatenate([of_seq(a4[n], i), of_seq(r4[n], i)], axis=0), sbd_scr[s0 + i, g])
                   for i in range(ns)] for n, (s0, g) in enumerate(units)]
    fs_a = [jnp.concatenate([f[0:seq_rows] for f in fs], axis=0) for fs in from_state]
    fs_r = [jnp.concatenate([f[seq_rows:2 * seq_rows] for f in fs], axis=0) for fs in from_state]
    low2 = [jnp.where(m_strict, gm[0:rows], 0.0) for gm in gram]
    rhs_u = [fs_a[n] + _mm(low2[n][:, rw_pad:2 * rw_pad], jnp.concatenate([v4[n], *pad], axis=0))
             for n in range(len(units))]
    inv = _unit_lower_inverse([lw[:, 0:rows] for lw in low2], c)
    u = [_mm(inv[n], rhs_u[n]) for n in range(len(units))]
    o_unit = {}
    for n, (s0, g) in enumerate(units):
        uv = jnp.concatenate([u[n], *pad, v4[n], *pad], axis=0)
        o4 = fs_r[n] + _mm(jnp.where(m_incl, gram[n][rows:2 * rows], 0.0), uv)
        for i in range(ns):
            o_j = o4[i * seq_rows:i * seq_rows + c]
            for hh in range(1, RW_GROUP):
                o_j = o_j + o4[i * seq_rows + hh * c:i * seq_rows + (hh + 1) * c]
            o_unit[(s0 + i, g)] = o_j
        b4 = stack(b_e, (s0, g))
        k4 = stack(k_e, (s0, g))
        for i in range(ns):
            ends = jnp.concatenate([of_seq(b4, i), of_seq(k4, i)], axis=0)
            uv_j = jnp.concatenate([of_seq(u[n], i), of_seq(v4[n], i)], axis=0)
            sbd_scr[s0 + i, g] = sbd_scr[s0 + i, g] * jnp.exp(cum_ends[s0 + i][:, lanes(g)]) + _mm_tn(uv_j, ends)

    @pl.when(ci == n_chunks - 1)
    def _fin():
        for j in seqs:
            for h in range(RW_HEADS):
                g, sl = head_block(h)
                s_ref[j, h] = sbd_scr[j, g, sl, sl]

    o = jnp.concatenate([jnp.concatenate([o_unit[(j, g)] for g in groups], axis=1) for j in seqs], axis=0)
    inv_n = 1.0 / RW_HEAD
    mean = _mm(o, bd) * inv_n
    dev = o - mean
    var = _mm(dev * dev, bd) * inv_n
    normed = dev * lax.rsqrt(var + RW_LN_EPS) * lng_ref[...] + lnb_ref[...]
    bonus = _mm(r * k1 * rk_ref[...], bd) * v
    out = (normed + bonus) * g_a
    o_ref[...] = out.astype(o_ref.dtype)


def _block_masks(n_blocks, c):
    r = np.arange(n_blocks * c)
    same = (r[:, None] // c) == (r[None, :] // c)
    causal = (same & (r[None, :] <= r[:, None])).astype(np.float32)
    strict = (same & (r[None, :] < r[:, None])).astype(np.float32)
    return causal, strict


def _rwkv_masks(c, nb):
    rows = nb * RW_GROUP * c
    rw_pad = -(-rows // LANES) * LANES
    lane_head = np.arange(RW_GW) // RW_HEAD
    hm = (lane_head[None, :] == np.arange(RW_GROUP)[:, None]).astype(np.float32)
    incl, strict = _block_masks(nb * RW_GROUP, c)
    widen = lambda m: np.tile(np.pad(m, ((0, 0), (0, rw_pad - rows))), (1, 2))
    return rw_pad, jnp.asarray(hm), jnp.asarray(widen(strict)), jnp.asarray(widen(incl))


ANY_SPEC = pl.BlockSpec(memory_space=pl.ANY)


def _mixer_io(cols, tails, *, o_prev, batch, c, nb, nc, row_block0, stacked):
    zeros = lambda t: (0,) * len(t)
    if stacked is None:
        prev = []
        st_shapes = [jax.ShapeDtypeStruct((batch,) + t, F32) for t in tails]
        st_specs = [pl.BlockSpec((nb,) + t, functools.partial(lambda b, ci, z: (b,) + z, z=zeros(t))) for t in tails]
    else:
        depth, layer, prev = stacked
        prev = list(prev)
        st_shapes = [jax.ShapeDtypeStruct((depth, batch) + t, F32) for t in tails]
        st_specs = [pl.BlockSpec((None, nb) + t, functools.partial(lambda b, ci, z: (layer, b) + z, z=zeros(t)))
                    for t in tails]
    tok_idx = lambda b, ci: (row_block0 + b * nc + ci, 0)
    aliased = [o_prev, *prev]
    in_specs = [ANY_SPEC] * len(aliased) + [pl.BlockSpec((nb * c, cols), tok_idx)]
    out_shape = (jax.ShapeDtypeStruct(o_prev.shape, o_prev.dtype), *st_shapes)
    out_specs = (pl.BlockSpec((nb * c, BR_W), tok_idx), *st_specs)
    return in_specs, out_shape, out_specs, {i: i for i in range(len(aliased))}, aliased


def _rwkv(pa, shift0, wkv0, layer, prm, *, o_prev, batch, seq, c, nb, ns, row_block0, stacked):
    nc = seq // c
    assert batch % nb == 0 and nb % ns == 0
    rw_pad, hm, m_strict, m_incl = _rwkv_masks(c, ns)
    rows = ns * RW_GROUP * c
    tri = jnp.asarray(_block_masks(nb, c)[0]).astype(BF16)
    io_specs, out_shape, out_specs, aliases, aliased = _mixer_io(
        RW_COLS, [(RW_HEADS, RW_HEAD, RW_HEAD), (1, RW_COLS)], o_prev=o_prev, batch=batch, c=c, nb=nb, nc=nc,
        row_block0=row_block0, stacked=stacked)
    kern = functools.partial(_rwkv_kernel, c=c, nb=nb, ns=ns, n_chunks=nc, rw_pad=rw_pad, n_alias=len(aliased))
    vec = lambda n: pl.BlockSpec((1, n), lambda b, ci: (0, 0))
    mat = lambda r_, n: pl.BlockSpec((r_, n), lambda b, ci: (0, 0))
    prm = tuple(prm) + (hm, m_strict, m_incl, tri)
    return pl.pallas_call(
        kern,
        out_shape=out_shape,
        grid=(batch // nb, nc),
        input_output_aliases=aliases,
        in_specs=[*io_specs,
                  pl.BlockSpec((None, nb, 1, RW_COLS), lambda b, ci: (layer, b, 0, 0)),
                  pl.BlockSpec((None, nb, RW_HEADS, RW_HEAD, RW_HEAD), lambda b, ci: (layer, b, 0, 0, 0)),
                  vec(RW_COLS), vec(512), mat(128, 512), vec(512), mat(128, 512), mat(128, 512),
                  vec(512), vec(512), vec(512), vec(512), vec(512), mat(512, 512),
                  mat(RW_GROUP, RW_GW), mat(rows, 2 * rw_pad), mat(rows, 2 * rw_pad), mat(nb * c, nb * c)],
        out_specs=out_specs,
        scratch_shapes=[pltpu.VMEM((nb, SUBLANES + c, RW_COLS), F32),
                        pltpu.VMEM((nb, RW_HEADS // RW_GROUP, RW_GW, RW_GW), F32)],
        compiler_params=_cparams(2),
        name="rwkv7",
    )(*aliased, pa, shift0, wkv0, *prm)


def _ret_kernel(*refs, c, nb, n_alias):
    refs = refs[n_alias:]
    tok_refs = refs[:1]
    cos_ref, sin_ref, mask_ref, s0_ref, o_ref, s_ref = refs[1:]
    ci = pl.program_id(1)

    @pl.when(ci == 0)
    def _init():
        s_ref[...] = s0_ref[...]

    rows = nb * c
    width = RET_HEADS * RET_DK
    half = RET_DK // 2
    cos = jnp.concatenate([cos_ref[...]] * nb, axis=0)
    sin = jnp.concatenate([sin_ref[...]] * nb, axis=0)
    first_half = (_iota((rows, width), 1) & half) == 0

    def rot(x):
        partner = jnp.where(first_half, pltpu.roll(x, width - half, 1), pltpu.roll(x, half, 1))
        return x * cos + partner * sin

    pb = jnp.concatenate([t[...] for t in tok_refs], axis=0)
    q = rot(pb[:, 0:width])
    k = rot(pb[:, width:2 * width]) * (RET_DK ** -0.5)
    v = pb[:, 2 * width:2 * width + BR_W]
    g_b = pb[:, 2 * width + BR_W:2 * width + 2 * BR_W]

    mask = mask_ref[...]
    causal = mask > 0.5
    dist = jnp.where(causal, (_iota((rows, rows), 0) - _iota((rows, rows), 1)).astype(F32), 0.0)
    t1 = jnp.sum(mask, axis=-1, keepdims=True)
    for h in range(RET_HEADS):
        log_gamma = math.log1p(-(2.0 ** (-5.0 - h)))
        dmat = jnp.where(causal, jnp.exp(dist * log_gamma), 0.0)
        q_h = q[:, h * RET_DK:(h + 1) * RET_DK]
        k_h = k[:, h * RET_DK:(h + 1) * RET_DK]
        v_h = v[:, h * RET_DV:(h + 1) * RET_DV]
        k_dec = k_h * jnp.exp((c - t1) * log_gamma)
        from_state = []
        for j in range(nb):
            js = slice(j * c, (j + 1) * c)
            s_h = s_ref[j, h]
            from_state.append(_mm(q_h[js], s_h))
            s_ref[j, h] = math.exp(c * log_gamma) * s_h + _mm_tn(k_dec[js], v_h[js])
        o_h = _mm(_mm_nt(q_h, k_h) * dmat, v_h) + jnp.exp(t1 * log_gamma) * jnp.concatenate(from_state, axis=0)
        o_n = o_h * lax.rsqrt(jnp.mean(o_h * o_h, axis=-1, keepdims=True) + NORM_EPS)
        o_ref[:, h * RET_DV:(h + 1) * RET_DV] = (o_n * _silu(g_b[:, h * RET_DV:(h + 1) * RET_DV])).astype(o_ref.dtype)


def _ret(pb, cos, sin, ret0, layer, *, o_prev, batch, seq, c, nb, row_block0, stacked):
    nc = seq // c
    assert batch % nb == 0
    width = RET_HEADS * RET_DK
    mask = jnp.asarray(_block_masks(nb, c)[0])
    io_specs, out_shape, out_specs, aliases, aliased = _mixer_io(
        RET_COLS, [(RET_HEADS, RET_DK, RET_DV)], o_prev=o_prev, batch=batch, c=c, nb=nb, nc=nc,
        row_block0=row_block0, stacked=stacked)
    return pl.pallas_call(
        functools.partial(_ret_kernel, c=c, nb=nb, n_alias=len(aliased)),
        out_shape=out_shape,
        grid=(batch // nb, nc),
        input_output_aliases=aliases,
        in_specs=[*io_specs,
                  pl.BlockSpec((c, width), lambda b, ci: (ci, 0)),
                  pl.BlockSpec((c, width), lambda b, ci: (ci, 0)),
                  pl.BlockSpec((nb * c, nb * c), lambda b, ci: (0, 0)),
                  pl.BlockSpec((None, nb, RET_HEADS, RET_DK, RET_DV), lambda b, ci: (layer, b, 0, 0, 0))],
        out_specs=out_specs,
        compiler_params=_cparams(2),
        name="retention",
    )(*aliased, pb, cos, sin, mask, ret0)


def _hgrn_kernel(*refs, c, m, nb, layer, n_chunks, n_alias):
    refs = refs[n_alias:]
    tok_refs = refs[:1]
    lbp_ref, ng_ref, tri_ref, s0_ref, o_ref, s_ref, st_scr = refs[1:]
    ci = pl.program_id(1)

    @pl.when(ci == 0)
    def _init():
        for j in range(nb):
            for h in range(HG_HEADS):
                st_scr[j, h] = s0_ref[j, h].T

    lbp = lbp_ref[...]
    e = jnp.exp(lbp - jnp.max(lbp, axis=0, keepdims=True))
    soft = e / jnp.sum(e, axis=0, keepdims=True)
    lb = jnp.zeros((1, BR_W), F32)
    for j in range(1, layer + 1):
        lb = lb + soft[j:j + 1, :]

    pc = jnp.concatenate([t[...] for t in tok_refs], axis=0)
    q_all = _silu(pc[:, 0:BR_W])
    f_in = pc[:, BR_W:2 * BR_W]
    v_all = pc[:, 2 * BR_W:3 * BR_W]
    g_c = pc[:, 3 * BR_W:4 * BR_W]
    log_f = jnp.log(lb + (1.0 - lb) * _sigmoid(f_in))
    k_all = (1.0 - lb) * _sigmoid(-f_in)
    cum_all = _mm_mask(tri_ref[...], log_f)

    col_m = _iota((m, m), 1)
    row_l = _iota((m, LANES), 0)
    outs = {}
    for j in range(nb):
        js = slice(j * c, (j + 1) * c)
        q, k, v, cum = q_all[js], k_all[js], v_all[js], cum_all[js]
        cum_last = cum[c - 1:c, :]
        q_e = q * jnp.exp(cum)
        k_e = k * jnp.exp(cum_last - cum)
        p_end = jnp.exp(cum_last)
        for h in range(HG_HEADS):
            sl = slice(h * HG_D, (h + 1) * HG_D)
            st = st_scr[j, h]
            o_rows = []
            for i in range(c // m):
                r0 = i * m
                q_i = q[r0:r0 + m, sl]
                c_i = cum[r0:r0 + m, sl]
                k_i = k[r0:r0 + m, sl]
                diag = jnp.zeros((m, m), F32)
                for s in range(m):
                    keep = row_l >= s
                    decay = jnp.exp(jnp.where(keep, c_i - c_i[s:s + 1, :], 0.0))
                    col = jnp.sum(jnp.where(keep, q_i * k_i[s:s + 1, :] * decay, 0.0), axis=-1, keepdims=True)
                    diag = jnp.where(col_m == s, col, diag)
                o_i = _mm(diag, v[r0:r0 + m, sl])
                if i > 0:
                    c_ref = cum[r0 - 1:r0, sl]
                    q_s = q_i * jnp.exp(c_i - c_ref)
                    k_s = k[0:r0, sl] * jnp.exp(c_ref - cum[0:r0, sl])
                    o_i = o_i + _mm(_mm_nt(q_s, k_s), v[0:r0, sl])
                o_rows.append(o_i)
            outs[(j, h)] = _mm_nt(q_e[:, sl], st) + jnp.concatenate(o_rows, axis=0)
            st_scr[j, h] = st * p_end[:, sl] + _mm_tn(v[:, sl], k_e[:, sl])
    for h in range(HG_HEADS):
        sl = slice(h * HG_D, (h + 1) * HG_D)
        o_h = jnp.concatenate([outs[(j, h)] for j in range(nb)], axis=0)
        o_n = o_h * lax.rsqrt(jnp.mean(o_h * o_h, axis=-1, keepdims=True) + NORM_EPS) * ng_ref[:, sl]
        o_ref[:, sl] = (o_n * _silu(g_c[:, sl])).astype(o_ref.dtype)

    @pl.when(ci == n_chunks - 1)
    def _fin():
        for j in range(nb):
            for h in range(HG_HEADS):
                s_ref[j, h] = st_scr[j, h].T


def _hgrn(pc, lbp, ng, hg0, state_layer, layer, *, o_prev, batch, seq, c, nb, row_block0, stacked):
    nc = seq // c
    assert batch % nb == 0
    m = 16 if c % 16 == 0 else SUBLANES
    depth = lbp.shape[0]
    tri = jnp.asarray(_block_masks(nb, c)[0]).astype(BF16)
    io_specs, out_shape, out_specs, aliases, aliased = _mixer_io(
        HG_COLS, [(HG_HEADS, HG_D, HG_D)], o_prev=o_prev, batch=batch, c=c, nb=nb, nc=nc,
        row_block0=row_block0, stacked=stacked)
    return pl.pallas_call(
        functools.partial(_hgrn_kernel, c=c, m=m, nb=nb, layer=layer, n_chunks=nc, n_alias=len(aliased)),
        out_shape=out_shape,
        grid=(batch // nb, nc),
        input_output_aliases=aliases,
        in_specs=[*io_specs,
                  pl.BlockSpec((depth, BR_W), lambda b, ci: (0, 0)),
                  pl.BlockSpec((1, BR_W), lambda b, ci: (0, 0)),
                  pl.BlockSpec((nb * c, nb * c), lambda b, ci: (0, 0)),
                  pl.BlockSpec((None, nb, HG_HEADS, HG_D, HG_D), lambda b, ci: (state_layer, b, 0, 0, 0))],
        out_specs=out_specs,
        scratch_shapes=[pltpu.VMEM((nb, HG_HEADS, HG_D, HG_D), F32)],
        compiler_params=_cparams(2),
        name="hgrn2",
    )(*aliased, pc, lbp, ng, tri, hg0)


def _gdn_kernel(*refs, c, nb, ns, n_alias):
    refs = refs[n_alias:]
    tok_refs = refs[:1]
    (cs0_ref, s0_ref, cw_ref, alog_ref, dtb_ref, ng_ref, mc_ref, mst_ref, tri_ref,
     o_ref, s_ref, cs_ref, ext_scr) = refs[1:]
    ci = pl.program_id(1)
    keep = GD_CONV - 1

    @pl.when(ci == 0)
    def _init():
        for j in range(nb):
            ext_scr[j, SUBLANES - keep:SUBLANES, :] = cs0_ref[j]
        s_ref[...] = s0_ref[...]

    convs = []
    for j in range(nb):
        x = tok_refs[0][j * c:(j + 1) * c, 0:GD_QKV]
        ext_scr[j, SUBLANES:SUBLANES + c, :] = x
        conv = ext_scr[j, SUBLANES - keep:SUBLANES - keep + c, :] * cw_ref[0:1, :]
        for i in range(1, keep):
            conv = conv + ext_scr[j, SUBLANES - keep + i:SUBLANES - keep + i + c, :] * cw_ref[i:i + 1, :]
        convs.append(conv + x * cw_ref[keep:keep + 1, :])
        tail = ext_scr[j, SUBLANES + c - keep:SUBLANES + c, :]
        ext_scr[j, SUBLANES - keep:SUBLANES, :] = tail
        cs_ref[j] = tail

    act = _silu(jnp.concatenate(convs, axis=0))
    q = act[:, 0:BR_W]
    k = act[:, BR_W:2 * BR_W]
    v = act[:, 2 * BR_W:3 * BR_W]
    gate_ba = jnp.concatenate([t[:, GD_QKV:GD_COLS_PAD] for t in tok_refs], axis=0)
    g_d = gate_ba[:, 0:BR_W]
    ba = gate_ba[:, BR_W:BR_W + LANES]
    beta_all = _sigmoid(ba)
    g_all = -jnp.exp(alog_ref[...]) * _softplus(ba + dtb_ref[...])

    cum_all = _mm_mask(tri_ref[...], g_all)

    ur = ns * c
    rows = GD_HEADS * ur
    stack = lambda parts: jnp.concatenate(parts, axis=0)
    l2n = lambda z: z * lax.rsqrt(jnp.sum(z * z, axis=-1, keepdims=True) + 1e-6)
    head = lambda z, h: z[:, h * GD_D:(h + 1) * GD_D]
    qn = jnp.concatenate([l2n(head(q, h)) for h in range(GD_HEADS)], axis=1) * (GD_D ** -0.5)
    kn = jnp.concatenate([l2n(head(k, h)) for h in range(GD_HEADS)], axis=1)
    units = range(nb // ns)
    heads = range(GD_HEADS)
    stack_u = lambda z, un: stack([head(z[un * ur:(un + 1) * ur], h) for h in heads])
    col_u = lambda z, lane0, un: stack([z[un * ur:(un + 1) * ur, lane0 + h:lane0 + h + 1] for h in heads])
    q4 = [stack_u(qn, un) for un in units]
    k4 = [stack_u(kn, un) for un in units]
    v4 = [stack_u(v, un) for un in units]
    beta = [col_u(beta_all, 0, un) for un in units]
    cum = [col_u(cum_all, GD_HEADS, un) for un in units]
    causal = mc_ref[...] > 0.5
    m_strict = mst_ref[...]
    eye = _iota((rows, rows), 0) == _iota((rows, rows), 1)
    ones = jnp.ones((rows, rows), BF16)
    cum_t = [jnp.broadcast_to(cm, (rows, rows)) for cm in cum]
    cum_s = [_mm_mask(ones, jnp.where(eye, ct, 0.0)) for ct in cum_t]
    dmat = [jnp.where(causal, jnp.exp(jnp.where(causal, ct - cs_, 0.0)), 0.0) for ct, cs_ in zip(cum_t, cum_s)]
    a_mat = [beta[un] * _mm_nt(k4[un], k4[un]) * (dmat[un] * m_strict) for un in units]
    inv = _unit_lower_inverse([-a for a in a_mat], c)
    e_cum = [jnp.exp(cm) for cm in cum]
    sol = [_mm(inv[un], jnp.concatenate([(beta[un] * e_cum[un]) * k4[un], beta[un] * v4[un]], axis=1)) for un in units]
    blocks = [(h, i) for h in heads for i in range(ns)]
    rows_of = lambda h, i: slice((h * ns + i) * c, (h * ns + i + 1) * c)
    qw = [[_mm(jnp.concatenate([q4[un][rows_of(h, i)], sol[un][rows_of(h, i), 0:GD_D]], axis=0), s_ref[un * ns + i, h])
           for h, i in blocks] for un in units]
    delta = [sol[un][:, GD_D:2 * GD_D] - stack([z[c:2 * c] for z in qw[un]]) for un in units]
    o4 = [e_cum[un] * stack([z[0:c] for z in qw[un]]) + _mm(_mm_nt(q4[un], k4[un]) * dmat[un], delta[un])
          for un in units]
    for un in units:
        for h, i in blocks:
            hs = rows_of(h, i)
            cum_h = cum[un][hs]
            cum_last = cum_h[c - 1:c, :]
            s_ref[un * ns + i, h] = (jnp.exp(cum_last) * s_ref[un * ns + i, h]
                                     + _mm_tn(k4[un][hs] * jnp.exp(cum_last - cum_h), delta[un][hs]))
    for h in heads:
        sl = slice(h * GD_D, (h + 1) * GD_D)
        o_h = stack([o4[un][h * ur:(h + 1) * ur] for un in units])
        o_n = o_h * lax.rsqrt(jnp.mean(o_h * o_h, axis=-1, keepdims=True) + NORM_EPS) * ng_ref[:, sl]
        o_ref[:, sl] = (o_n * _silu(g_d[:, sl])).astype(o_ref.dtype)


def _gdn(pd, conv0, gd0, layer, cw, alog, dtb, ng, *, o_prev, batch, seq, c, nb, ns, row_block0, stacked):
    nc = seq // c
    assert batch % nb == 0 and nb % ns == 0
    keep = GD_CONV - 1
    rows = GD_HEADS * ns * c
    m_causal, m_strict = (jnp.asarray(m) for m in _block_masks(GD_HEADS * ns, c))
    tri = jnp.asarray(_block_masks(nb, c)[0]).astype(BF16)
    full = lambda *shape: pl.BlockSpec(shape, lambda b, ci: (0,) * len(shape))
    io_specs, out_shape, out_specs, aliases, aliased = _mixer_io(
        GD_COLS_PAD, [(GD_HEADS, GD_D, GD_D), (keep, GD_QKV)], o_prev=o_prev, batch=batch, c=c, nb=nb, nc=nc,
        row_block0=row_block0, stacked=stacked)
    return pl.pallas_call(
        functools.partial(_gdn_kernel, c=c, nb=nb, ns=ns, n_alias=len(aliased)),
        out_shape=out_shape,
        grid=(batch // nb, nc),
        input_output_aliases=aliases,
        in_specs=[*io_specs,
                  pl.BlockSpec((None, nb, keep, GD_QKV), lambda b, ci: (layer, b, 0, 0)),
                  pl.BlockSpec((None, nb, GD_HEADS, GD_D, GD_D), lambda b, ci: (layer, b, 0, 0, 0)),
                  full(GD_CONV, GD_QKV), full(1, LANES), full(1, LANES), full(1, BR_W),
                  full(rows, rows), full(rows, rows), full(nb * c, nb * c)],
        out_specs=out_specs,
        scratch_shapes=[pltpu.VMEM((nb, SUBLANES + c, GD_QKV), F32)],
        compiler_params=_cparams(2),
        name="gdn",
    )(*aliased, pd, conv0, gd0, cw, alog, dtb, ng, m_causal, m_strict, tri)


def _pick_chunk(seq):
    for c in (64, 56, 48, 40, 32, 24, 16, 8):
        if seq % c == 0:
            return c
    raise ValueError(f"sequence length {seq} is not a multiple of {SUBLANES}")


def _pick_tile(n, candidates):
    for t in candidates:
        if n % t == 0:
            return t
    raise ValueError(f"no tile in {candidates} divides {n}")


def _rope_tables(pos):
    half = RET_DK // 2
    inv = ROPE_BASE ** (-jnp.arange(half, dtype=F32) / half)
    ang = pos.astype(F32)[:, None] * inv[None, :]
    cos, sin = jnp.cos(ang), jnp.sin(ang)
    cos_t = jnp.tile(jnp.concatenate([cos, cos], axis=1), (1, RET_HEADS))
    sin_t = jnp.tile(jnp.concatenate([-sin, sin], axis=1), (1, RET_HEADS))
    return cos_t, sin_t


def kernel(x_prompt, x_sample, state_rwkv_wkv, state_rwkv_shift, state_ret, state_hgrn, state_gdn, state_gdn_conv, meta_tokens, norm_mix, w_in, rw_mu, rw_w0, rw_w2, rw_a0, rw_a2, rw_g2, rw_kk, rw_ka, rw_rk, rw_ln_g, rw_ln_b, hg_lb, hg_norm_g, gd_conv, gd_a_log, gd_dt_bias, gd_norm_g, w_branch, w_out, norm_ffn, w_up, w_down, norm_final):
    depth = norm_mix.shape[0]
    bp, tq, d = x_prompt.shape
    tp = tq + N_META
    bs, ts, _ = x_sample.shape
    n_p, n_s = bp * tp, bs * ts
    n_tok = n_p + n_s
    cp, cs = _pick_chunk(tp), _pick_chunk(ts)
    tm = _pick_tile(n_tok, (464, 512, 256, 232, 128, 64, 32, 16, 8))
    tm_in = _pick_tile(n_tok, (928, 464, 512, 256, 232, 128, 64, 32, 16, 8))
    nb_s = _pick_tile(bs, (8, 4, 2, 1))
    nb_p = _pick_tile(bp, (4, 2, 1))
    ncp, gp = tp // cp, bp // nb_p
    assert cs == ts and n_p % (nb_s * ts) == 0

    meta = jnp.broadcast_to(meta_tokens.astype(F32)[None], (bp, N_META, d))
    xp = jnp.concatenate([meta, x_prompt], axis=1).reshape(gp, nb_p, ncp, cp, d)
    x = jnp.concatenate([xp.transpose(0, 2, 1, 3, 4).reshape(n_p, d), x_sample.reshape(n_s, d)], axis=0)

    cos_p, sin_p = _rope_tables(jnp.arange(tp, dtype=jnp.int32))
    cos_s, sin_s = _rope_tables(PAST_LEN + jnp.arange(ts, dtype=jnp.int32))

    head_of = jnp.arange(BR_W) // RW_HEAD
    bd = (head_of[:, None] == head_of[None, :]).astype(BF16)

    zeros = lambda *s: jnp.zeros((1, bp) + s, F32)
    z_wkv, z_shift = zeros(RW_HEADS, RW_HEAD, RW_HEAD), zeros(1, RW_COLS)
    z_ret, z_hg, z_gd = zeros(RET_HEADS, RET_DK, RET_DV), zeros(HG_HEADS, HG_D, HG_D), zeros(GD_HEADS, GD_D, GD_D)
    z_conv = zeros(GD_CONV - 1, GD_QKV)
    shift_s = state_rwkv_shift.reshape(depth, bs, 1, RW_COLS)

    assert w_in.shape[2] == IN_G + GATE_COLS
    w_a, w_b, w_c, w_d, w_g = _w_in_prep(jnp.swapaxes(w_in, 1, 2), LANES)
    wb_all, wo_all = w_branch.astype(BF16), w_out.astype(BF16)
    wu_all, wd_all = w_up.astype(BF16), w_down.astype(BF16)

    new_p = [[] for _ in range(6)]
    s_stacks = [jnp.zeros((depth, bs) + tail, F32) for tail in (
        (RW_HEADS, RW_HEAD, RW_HEAD), (1, RW_COLS), (RET_HEADS, RET_DK, RET_DV), (HG_HEADS, HG_D, HG_D),
        (GD_HEADS, GD_D, GD_D), (GD_CONV - 1, GD_QKV))]
    h = _rmsnorm(x, norm_mix[0], tm, BF16)
    for l in range(depth):
        pa = _matmul(h, w_a, l, tm_in, 896, "in_proj_a")
        pb = _matmul(h, w_b, l, tm_in, 768, "in_proj_b")
        pc = _matmul(h, w_c, l, tm_in, 1024, "in_proj_c")
        pd = _matmul(h, w_d, l, tm_in, GD_COLS_PAD, "in_proj_d")
        gates = _matmul(h, w_g, l, tm_in, 2048, "in_proj_g", gate=True)

        row1 = lambda a: a.reshape(1, -1)
        pad_rows = lambda a, top: jnp.concatenate(
            [jnp.zeros((top, BR_W), F32), a, jnp.zeros((LANES - top - a.shape[0], BR_W), F32)], axis=0).astype(BF16)
        rw_prm = (row1(rw_mu[l]), row1(rw_w0[l]), pad_rows(rw_w2[l], 0), row1(rw_a0[l]), pad_rows(rw_a2[l], 64),
                  rw_g2[l].astype(BF16), row1(rw_kk[l]), row1(rw_ka[l]), row1(rw_rk[l]), row1(rw_ln_g[l]),
                  row1(rw_ln_b[l]), bd)
        lane_pad = lambda a: jnp.concatenate(
            [jnp.zeros((GD_HEADS,), F32), a, jnp.zeros((LANES - 2 * GD_HEADS,), F32)]).reshape(1, LANES)
        gd_prm = (gd_conv[l], lane_pad(gd_a_log[l]), lane_pad(gd_dt_bias[l]), row1(gd_norm_g[l]))
        hg_g = row1(hg_norm_g[l])

        if l == 0:
            oa = ob = oc = od = jnp.zeros((n_tok, BR_W), BF16)
        for grp, (batch, seq, c, nb, ns, rb0, sts, lay, cos, sin) in enumerate((
                (bp, tp, cp, nb_p, 1, 0, (z_wkv, z_shift, z_ret, z_hg, z_gd, z_conv), 0, cos_p, sin_p),
                (bs, ts, cs, nb_s, nb_s, n_p // (nb_s * ts),
                 (state_rwkv_wkv, shift_s, state_ret, state_hgrn, state_gdn, state_gdn_conv), l, cos_s, sin_s))):
            kw = dict(batch=batch, seq=seq, c=c, nb=nb, row_block0=rb0)
            stk = lambda *prev: None if grp == 0 else (depth, l, prev)
            oa, wkv, shift = _rwkv(pa, sts[1], sts[0], lay, rw_prm, o_prev=oa, ns=ns, stacked=stk(*s_stacks[0:2]), **kw)
            ob, ret = _ret(pb, cos, sin, sts[2], lay, o_prev=ob, stacked=stk(*s_stacks[2:3]), **kw)
            oc, hg = _hgrn(pc, hg_lb, hg_g, sts[3], lay, l, o_prev=oc, stacked=stk(*s_stacks[3:4]), **kw)
            od, gd, conv = _gdn(pd, sts[5], sts[4], lay, *gd_prm, o_prev=od, ns=ns, stacked=stk(*s_stacks[4:6]), **kw)
            if grp == 0:
                for i, st in enumerate((wkv, shift.reshape(batch, RW_COLS), ret, hg, gd, conv)):
                    new_p[i].append(st)
            else:
                s_stacks = [wkv, shift, ret, hg, gd, conv]
        x = _merge(x, (oa, ob, oc, od), gates, wb_all, wo_all, l, tm_in, 256)
        last = l == depth - 1
        res = _ffn(x, norm_ffn[l], norm_final if last else norm_mix[l + 1], wu_all, wd_all, l, tm, 512, last)
        if last:
            y = res
        else:
            x, h = res

    y_prompt = y[:n_p].reshape(gp, ncp, nb_p, cp, d).transpose(0, 2, 1, 3, 4).reshape(bp, tp, d)[:, N_META:]
    y_sample = y[n_p:].reshape(bs, ts, d)
    p_states = tuple(jnp.stack(z) for z in new_p)
    s_wkv, s_shift, s_ret, s_hg, s_gd, s_conv = s_stacks
    return (y_prompt, y_sample) + p_states + (s_wkv, s_shift.reshape(depth, bs, RW_COLS), s_ret, s_hg, s_gd, s_conv)
```

```python
import functools
import math

import jax
import jax.numpy as jnp
import numpy as np
from jax import lax
from jax.experimental import pallas as pl
from jax.experimental.pallas import tpu as pltpu

F32 = jnp.float32
BF16 = jnp.bfloat16

N_META = 16
PAST_LEN = 16384
NORM_EPS = 1e-6
RW_LN_EPS = 64e-5
ROPE_BASE = 10000.0

D_MODEL = 2048
BR_W = 512
RW_HEADS, RW_HEAD = 8, 64
RW_COLS = 1792
RW_GROUP = 4
RW_GW = RW_GROUP * RW_HEAD
RET_HEADS, RET_DK, RET_DV = 4, 64, 128
RET_COLS = 1536
HG_HEADS, HG_D = 4, 128
HG_COLS = 2048
GD_HEADS, GD_D = 4, 128
GD_CONV = 4
GD_QKV = 1536
GD_COLS_PAD = 2176

LANES = 128
SUBLANES = 8
VMEM_LIMIT = 56 * 1024 * 1024


def _cparams(n_axes):
    return pltpu.CompilerParams(dimension_semantics=("arbitrary",) * n_axes, vmem_limit_bytes=VMEM_LIMIT)


def _dg(a, b, ca, cb):
    return lax.dot_general(a, b, (((ca,), (cb,)), ((), ())), preferred_element_type=F32)


def _mm(a, b):
    return _dg(a.astype(BF16), b.astype(BF16), 1, 0)


def _mm_nt(a, b):
    return _dg(a.astype(BF16), b.astype(BF16), 1, 1)


def _mm_tn(a, b):
    return _dg(a.astype(BF16), b.astype(BF16), 0, 0)


def _split(x, n):
    parts, r = [], x
    for _ in range(n):
        p = r.astype(BF16)
        parts.append(p)
        r = r - p.astype(F32)
    return parts


def _mm_mask(m01, x):
    x0, x1, x2 = _split(x, 3)
    return _dg(m01, x0, 1, 0) + (_dg(m01, x1, 1, 0) + _dg(m01, x2, 1, 0))


def _iota(shape, dim):
    return lax.broadcasted_iota(jnp.int32, shape, dim)


def _softplus(x):
    return jnp.maximum(x, 0.0) + jnp.log(1.0 + jnp.exp(-jnp.abs(x)))


def _sigmoid(x):
    return jax.nn.sigmoid(x)


def _silu(x):
    return x * jax.nn.sigmoid(x)


def _unit_lower_inverse(lows, c):
    shape = lows[0].shape
    eye = (_iota(shape, 0) == _iota(shape, 1)).astype(F32)
    invs = [eye + low for low in lows]
    powers = list(lows)
    covered = 1
    while covered < c - 1:
        powers = [_mm(p, p) for p in powers]
        invs = [inv + _mm(inv, p) for inv, p in zip(invs, powers)]
        covered = 2 * covered + 1
    return invs


def _rmsnorm_kernel(x_ref, g_ref, o_ref):
    x = x_ref[...]
    y = x * lax.rsqrt(jnp.mean(x * x, axis=-1, keepdims=True) + NORM_EPS)
    o_ref[...] = (y * g_ref[...]).astype(o_ref.dtype)


def _rmsnorm(x, g, tm, out_dtype):
    n, d = x.shape
    return pl.pallas_call(
        _rmsnorm_kernel,
        out_shape=jax.ShapeDtypeStruct((n, d), out_dtype),
        grid=(n // tm,),
        in_specs=[pl.BlockSpec((tm, d), lambda i: (i, 0)), pl.BlockSpec((1, d), lambda i: (0, 0))],
        out_specs=pl.BlockSpec((tm, d), lambda i: (i, 0)),
        compiler_params=_cparams(1),
        name="rmsnorm",
    )(x, g.reshape(1, d))


def _matmul_kernel(x_ref, w_ref, o_ref):
    o_ref[...] = _dg(x_ref[...], w_ref[...], 1, 1)


def _matmul_sigmoid_kernel(x_ref, w_ref, o_ref):
    o_ref[...] = _sigmoid(_dg(x_ref[...], w_ref[...], 1, 1)).astype(o_ref.dtype)


def _matmul(x, w, layer, tm, tn, name, gate=False):
    n, k = x.shape
    m = w.shape[1]
    return pl.pallas_call(
        _matmul_sigmoid_kernel if gate else _matmul_kernel,
        out_shape=jax.ShapeDtypeStruct((n, m), BF16 if gate else F32),
        grid=(n // tm, m // tn),
        in_specs=[pl.BlockSpec((tm, k), lambda i, j: (i, 0)), pl.BlockSpec((None, tn, k), lambda i, j: (layer, j, 0))],
        out_specs=pl.BlockSpec((tm, tn), lambda i, j: (i, j)),
        compiler_params=_cparams(2),
        name=name,
    )(x, w)


IN_A, IN_B, IN_C = RW_COLS, RW_COLS + RET_COLS, RW_COLS + RET_COLS + HG_COLS
IN_BA = IN_C + GD_QKV
IN_G = IN_BA + 2 * GD_HEADS + BR_W
GATE_COLS = 4 * D_MODEL


def _w_in_prep_kernel(w_ref, a_ref, b_ref, c_ref, d_ref, g_ref):
    cast = lambda lo, hi: w_ref[lo:hi, :].astype(BF16)
    a_ref[...] = cast(0, IN_A)
    b_ref[...] = cast(IN_A, IN_B)
    c_ref[...] = cast(IN_B, IN_C)
    d_ref[0:GD_QKV, :] = cast(IN_C, IN_BA)
    d_ref[GD_QKV:GD_QKV + BR_W, :] = cast(IN_BA + 2 * GD_HEADS, IN_G)
    ba = jnp.concatenate([w_ref[IN_BA:IN_BA + 2 * GD_HEADS, :],
                          jnp.zeros((LANES - 2 * GD_HEADS, w_ref.shape[1]), F32)], axis=0)
    d_ref[GD_QKV + BR_W:GD_COLS_PAD, :] = ba.astype(BF16)
    step = 1024
    for lo in range(0, GATE_COLS, step):
        g_ref[lo:lo + step, :] = cast(IN_G + lo, IN_G + lo + step)


def _w_in_prep(w_in_t, tk):
    depth, n, k = w_in_t.shape
    out = lambda cols: (jax.ShapeDtypeStruct((depth, cols, k), BF16), pl.BlockSpec((None, cols, tk), lambda l, i: (l, 0, i)))
    shapes, specs = zip(*(out(cols) for cols in (RW_COLS, RET_COLS, HG_COLS, GD_COLS_PAD, GATE_COLS)))
    return pl.pallas_call(
        _w_in_prep_kernel,
        out_shape=shapes,
        grid=(depth, k // tk),
        in_specs=[pl.BlockSpec((None, n, tk), lambda l, i: (l, 0, i))],
        out_specs=specs,
        compiler_params=_cparams(2),
        name="w_in_prep",
    )(w_in_t)


def _merge_kernel(x_ref, oa_ref, ob_ref, oc_ref, od_ref, g0_ref, g1_ref, g2_ref, g3_ref, wb_ref, wo_ref, y_ref):
    j = pl.program_id(1)

    @pl.when(j == 0)
    def _init():
        y_ref[...] = x_ref[...]

    merged = None
    for n, (o_ref, g_ref) in enumerate(((oa_ref, g0_ref), (ob_ref, g1_ref), (oc_ref, g2_ref), (od_ref, g3_ref))):
        term = g_ref[...].astype(F32) * _dg(o_ref[...], wb_ref[n], 1, 0)
        merged = term if merged is None else merged + term
    y_ref[...] += _dg(merged.astype(BF16), wo_ref[...], 1, 0)


def _merge(x, outs, pg, wb, wo, layer, tm, tj):
    n, d = x.shape
    nj = d // tj
    o_spec = pl.BlockSpec((tm, BR_W), lambda i, j: (i, 0))
    g_specs = [pl.BlockSpec((tm, tj), functools.partial(lambda i, j, nb: (i, nb * nj + j), nb=nb)) for nb in range(4)]
    return pl.pallas_call(
        _merge_kernel,
        out_shape=jax.ShapeDtypeStruct((n, d), F32),
        grid=(n // tm, nj),
        in_specs=[pl.BlockSpec((tm, d), lambda i, j: (i, 0)), o_spec, o_spec, o_spec, o_spec, *g_specs,
                  pl.BlockSpec((None, 4, BR_W, tj), lambda i, j: (layer, 0, 0, j)),
                  pl.BlockSpec((None, tj, d), lambda i, j: (layer, j, 0))],
        out_specs=pl.BlockSpec((tm, d), lambda i, j: (i, 0)),
        compiler_params=_cparams(2),
        name="merge",
    )(x, *outs, pg, pg, pg, pg, wb, wo)


def _rms(x, g):
    return x * lax.rsqrt(jnp.mean(x * x, axis=-1, keepdims=True) + NORM_EPS) * g


def _ffn_kernel(x_ref, g_ref, gn_ref, wu_ref, wg_ref, wd_ref, *out_and_scratch, last):
    *outs, h_ref = out_and_scratch
    y_ref = outs[0]
    j = pl.program_id(1)

    @pl.when(j == 0)
    def _init():
        x = x_ref[...]
        h_ref[...] = _rms(x, g_ref[...]).astype(BF16)
        y_ref[...] = x

    h = h_ref[...]
    up = _dg(h, wu_ref[...], 1, 0)
    gate = _dg(h, wg_ref[...], 1, 0)
    y_ref[...] += _dg((_silu(gate) * up).astype(BF16), wd_ref[...], 1, 0)

    @pl.when(j == pl.num_programs(1) - 1)
    def _fin():
        if last:
            y_ref[...] = _rms(y_ref[...], gn_ref[...])
        else:
            outs[1][...] = _rms(y_ref[...], gn_ref[...]).astype(BF16)


def _ffn(x, g, g_next, w_up, w_down, layer, tm, tf, last):
    n, d = x.shape
    ff = w_down.shape[1]
    nf = ff // tf
    row_spec = pl.BlockSpec((tm, d), lambda i, j: (i, 0))
    vec_spec = pl.BlockSpec((1, d), lambda i, j: (0, 0))
    if last:
        out_shape, out_specs = jax.ShapeDtypeStruct((n, d), F32), row_spec
    else:
        out_shape = (jax.ShapeDtypeStruct((n, d), F32), jax.ShapeDtypeStruct((n, d), BF16))
        out_specs = (row_spec, row_spec)
    return pl.pallas_call(
        functools.partial(_ffn_kernel, last=last),
        out_shape=out_shape,
        grid=(n // tm, nf),
        in_specs=[row_spec, vec_spec, vec_spec,
                  pl.BlockSpec((None, d, tf), lambda i, j: (layer, 0, j)),
                  pl.BlockSpec((None, d, tf), lambda i, j: (layer, 0, nf + j)),
                  pl.BlockSpec((None, tf, d), lambda i, j: (layer, j, 0))],
        out_specs=out_specs,
        scratch_shapes=[pltpu.VMEM((tm, d), BF16)],
        compiler_params=_cparams(2),
        name="ffn",
    )(x, g.reshape(1, d), g_next.reshape(1, d), w_up, w_up, w_down)


def _rwkv_kernel(*refs, c, nb, ns, n_chunks, rw_pad, n_alias):
    refs = refs[n_alias:]
    tok_refs = refs[:1]
    (sh0_ref, s0_ref, mu_ref, w0_ref, w2_ref, a0_ref, a2_ref, g2_ref, kkw_ref, ka_ref, rk_ref, lng_ref, lnb_ref,
     bd_ref, hm_ref, ms_ref, mi_ref, tri_ref, o_ref, s_ref, sh_ref, xs_scr, sbd_scr) = refs[1:]
    ci = pl.program_id(1)
    seqs = range(nb)
    head_block = lambda h: (h // RW_GROUP, slice((h % RW_GROUP) * RW_HEAD, (h % RW_GROUP + 1) * RW_HEAD))

    @pl.when(ci == 0)
    def _init():
        sbd_scr[...] = jnp.zeros_like(sbd_scr)
        for j in seqs:
            xs_scr[j, SUBLANES - 1:SUBLANES, :] = sh0_ref[j]
            for h in range(RW_HEADS):
                g, sl = head_block(h)
                sbd_scr[j, g, sl, sl] = s0_ref[j, h]

    pa = jnp.concatenate([t[...] for t in tok_refs], axis=0)
    prevs = []
    for j in seqs:
        pa_j = pa[j * c:(j + 1) * c, :]
        xs_scr[j, SUBLANES:SUBLANES + c, :] = pa_j
        prevs.append(xs_scr[j, SUBLANES - 1:SUBLANES - 1 + c, :])
        last = pa_j[c - 1:c, :]
        xs_scr[j, SUBLANES - 1:SUBLANES, :] = last
        sh_ref[j] = last
    prev = jnp.concatenate(prevs, axis=0)

    xm = pa + (prev - pa) * mu_ref[...]
    r = xm[:, 0:512]
    k0 = xm[:, 512:1024]
    v = xm[:, 1024:1536]
    wa = xm[:, 1536:1664]
    gl = xm[:, 1664:1792]
    w_log = -_softplus(-(w0_ref[...] + _mm(jnp.tanh(wa), w2_ref[...]))) - 0.5
    ld = -jnp.exp(w_log)
    asig = _sigmoid(a0_ref[...] + _mm(wa, a2_ref[...]))
    g_a = _mm(_sigmoid(gl), g2_ref[...])
    bd = bd_ref[...]
    kkr = k0 * kkw_ref[...]
    kkn = kkr * lax.rsqrt(_mm(kkr * kkr, bd) + 1e-6)
    k1 = k0 * (1.0 + (asig - 1.0) * ka_ref[...])
    a_vec = -kkn
    b_vec = kkn * asig

    cum = _mm_mask(tri_ref[...], ld)
    cum_ends = [cum[(j + 1) * c - 1:(j + 1) * c, :] for j in seqs]
    cum_last = jnp.concatenate([jnp.broadcast_to(e, (c, BR_W)) for e in cum_ends], axis=0)
    a_t = a_vec * jnp.exp(cum - ld)
    r_t = r * jnp.exp(cum)
    inv_p = jnp.exp(-cum)
    b_t = b_vec * inv_p
    k_t = k1 * inv_p
    to_end = jnp.exp(cum_last - cum)
    b_e = b_vec * to_end
    k_e = k1 * to_end

    seq_rows = RW_GROUP * c
    rows = ns * seq_rows
    pad = [jnp.zeros((rw_pad - rows, RW_GW), F32)] if rw_pad > rows else []
    m_strict = ms_ref[...] > 0.5
    m_incl = mi_ref[...] > 0.5
    groups = range(RW_HEADS // RW_GROUP)
    units = [(s0, g) for s0 in range(0, nb, ns) for g in groups]
    lanes = lambda g: slice(g * RW_GW, (g + 1) * RW_GW)
    of_seq = lambda x, i: x[i * seq_rows:(i + 1) * seq_rows]

    def stack(x, unit):
        s0, g = unit
        return jnp.concatenate([x[j * c:(j + 1) * c, lanes(g)] * hm_ref[hh:hh + 1, :]
                                for j in range(s0, s0 + ns) for hh in range(RW_GROUP)], axis=0)

    a4 = [stack(a_t, un) for un in units]
    r4 = [stack(r_t, un) for un in units]
    v4 = [stack(v, un) for un in units]
    gram = [_mm_nt(jnp.concatenate([a4[n], r4[n]], axis=0),
                   jnp.concatenate([stack(b_t, un), *pad, stack(k_t, un), *pad], axis=0))
            for n, un in enumerate(units)]
    from_state = [[_mm_nt(jnp.concatenate([of_seq(a4[n], i), of_seq(r4[n], i)], axis=0), sbd_scr[s0 + i, g])
                   for i in range(ns)] for n, (s0, g) in enumerate(units)]
    fs_a = [jnp.concatenate([f[0:seq_rows] for f in fs], axis=0) for fs in from_state]
    fs_r = [jnp.concatenate([f[seq_rows:2 * seq_rows] for f in fs], axis=0) for fs in from_state]
    low2 = [jnp.where(m_strict, gm[0:rows], 0.0) for gm in gram]
    rhs_u = [fs_a[n] + _mm(low2[n][:, rw_pad:2 * rw_pad], jnp.concatenate([v4[n], *pad], axis=0))
             for n in range(len(units))]
    inv = _unit_lower_inverse([lw[:, 0:rows] for lw in low2], c)
    u = [_mm(inv[n], rhs_u[n]) for n in range(len(units))]
    o_unit = {}
    for n, (s0, g) in enumerate(units):
        uv = jnp.concatenate([u[n], *pad, v4[n], *pad], axis=0)
        o4 = fs_r[n] + _mm(jnp.where(m_incl, gram[n][rows:2 * rows], 0.0), uv)
        for i in range(ns):
            o_j = o4[i * seq_rows:i * seq_rows + c]
            for hh in range(1, RW_GROUP):
                o_j = o_j + o4[i * seq_rows + hh * c:i * seq_rows + (hh + 1) * c]
            o_unit[(s0 + i, g)] = o_j
        b4 = stack(b_e, (s0, g))
        k4 = stack(k_e, (s0, g))
        for i in range(ns):
            ends = jnp.concatenate([of_seq(b4, i), of_seq(k4, i)], axis=0)
            uv_j = jnp.concatenate([of_seq(u[n], i), of_seq(v4[n], i)], axis=0)
            sbd_scr[s0 + i, g] = sbd_scr[s0 + i, g] * jnp.exp(cum_ends[s0 + i][:, lanes(g)]) + _mm_tn(uv_j, ends)

    @pl.when(ci == n_chunks - 1)
    def _fin():
        for j in seqs:
            for h in range(RW_HEADS):
                g, sl = head_block(h)
                s_ref[j, h] = sbd_scr[j, g, sl, sl]

    o = jnp.concatenate([jnp.concatenate([o_unit[(j, g)] for g in groups], axis=1) for j in seqs], axis=0)
    inv_n = 1.0 / RW_HEAD
    mean = _mm(o, bd) * inv_n
    dev = o - mean
    var = _mm(dev * dev, bd) * inv_n
    normed = dev * lax.rsqrt(var + RW_LN_EPS) * lng_ref[...] + lnb_ref[...]
    bonus = _mm(r * k1 * rk_ref[...], bd) * v
    out = (normed + bonus) * g_a
    o_ref[...] = out.astype(o_ref.dtype)


def _block_masks(n_blocks, c):
    r = np.arange(n_blocks * c)
    same = (r[:, None] // c) == (r[None, :] // c)
    causal = (same & (r[None, :] <= r[:, None])).astype(np.float32)
    strict = (same & (r[None, :] < r[:, None])).astype(np.float32)
    return causal, strict


def _rwkv_masks(c, nb):
    rows = nb * RW_GROUP * c
    rw_pad = -(-rows // LANES) * LANES
    lane_head = np.arange(RW_GW) // RW_HEAD
    hm = (lane_head[None, :] == np.arange(RW_GROUP)[:, None]).astype(np.float32)
    incl, strict = _block_masks(nb * RW_GROUP, c)
    widen = lambda m: np.tile(np.pad(m, ((0, 0), (0, rw_pad - rows))), (1, 2))
    return rw_pad, jnp.asarray(hm), jnp.asarray(widen(strict)), jnp.asarray(widen(incl))


ANY_SPEC = pl.BlockSpec(memory_space=pl.ANY)


def _mixer_io(cols, tails, *, o_prev, batch, c, nb, nc, row_block0, stacked):
    zeros = lambda t: (0,) * len(t)
    depth, layer, prev = stacked
    prev = list(prev)
    st_shapes = [jax.ShapeDtypeStruct((depth, batch) + t, F32) for t in tails]
    st_specs = [pl.BlockSpec((None, nb) + t, functools.partial(lambda b, ci, z: (layer, b) + z, z=zeros(t)))
                for t in tails]
    tok_idx = lambda b, ci: (row_block0 + b * nc + ci, 0)
    aliased = [o_prev, *prev]
    in_specs = [ANY_SPEC] * len(aliased) + [pl.BlockSpec((nb * c, cols), tok_idx)]
    out_shape = (jax.ShapeDtypeStruct(o_prev.shape, o_prev.dtype), *st_shapes)
    out_specs = (pl.BlockSpec((nb * c, BR_W), tok_idx), *st_specs)
    return in_specs, out_shape, out_specs, {i: i for i in range(len(aliased))}, aliased


def _rwkv(pa, shift0, wkv0, layer, prm, *, o_prev, batch, seq, c, nb, ns, row_block0, stacked):
    nc = seq // c
    assert batch % nb == 0 and nb % ns == 0
    rw_pad, hm, m_strict, m_incl = _rwkv_masks(c, ns)
    rows = ns * RW_GROUP * c
    tri = jnp.asarray(_block_masks(nb, c)[0]).astype(BF16)
    io_specs, out_shape, out_specs, aliases, aliased = _mixer_io(
        RW_COLS, [(RW_HEADS, RW_HEAD, RW_HEAD), (1, RW_COLS)], o_prev=o_prev, batch=batch, c=c, nb=nb, nc=nc,
        row_block0=row_block0, stacked=stacked)
    kern = functools.partial(_rwkv_kernel, c=c, nb=nb, ns=ns, n_chunks=nc, rw_pad=rw_pad, n_alias=len(aliased))
    vec = lambda n: pl.BlockSpec((1, n), lambda b, ci: (0, 0))
    mat = lambda r_, n: pl.BlockSpec((r_, n), lambda b, ci: (0, 0))
    prm = tuple(prm) + (hm, m_strict, m_incl, tri)
    return pl.pallas_call(
        kern,
        out_shape=out_shape,
        grid=(batch // nb, nc),
        input_output_aliases=aliases,
        in_specs=[*io_specs,
                  pl.BlockSpec((None, nb, 1, RW_COLS), lambda b, ci: (layer, b, 0, 0)),
                  pl.BlockSpec((None, nb, RW_HEADS, RW_HEAD, RW_HEAD), lambda b, ci: (layer, b, 0, 0, 0)),
                  vec(RW_COLS), vec(512), mat(128, 512), vec(512), mat(128, 512), mat(128, 512),
                  vec(512), vec(512), vec(512), vec(512), vec(512), mat(512, 512),
                  mat(RW_GROUP, RW_GW), mat(rows, 2 * rw_pad), mat(rows, 2 * rw_pad), mat(nb * c, nb * c)],
        out_specs=out_specs,
        scratch_shapes=[pltpu.VMEM((nb, SUBLANES + c, RW_COLS), F32),
                        pltpu.VMEM((nb, RW_HEADS // RW_GROUP, RW_GW, RW_GW), F32)],
        compiler_params=_cparams(2),
        name="rwkv7",
    )(*aliased, pa, shift0, wkv0, *prm)


def _ret_kernel(*refs, c, nb, n_alias):
    refs = refs[n_alias:]
    tok_refs = refs[:1]
    cos_ref, sin_ref, mask_ref, s0_ref, o_ref, s_ref = refs[1:]
    ci = pl.program_id(1)

    @pl.when(ci == 0)
    def _init():
        s_ref[...] = s0_ref[...]

    rows = nb * c
    width = RET_HEADS * RET_DK
    half = RET_DK // 2
    cos = jnp.concatenate([cos_ref[...]] * nb, axis=0)
    sin = jnp.concatenate([sin_ref[...]] * nb, axis=0)
    first_half = (_iota((rows, width), 1) & half) == 0

    def rot(x):
        partner = jnp.where(first_half, pltpu.roll(x, width - half, 1), pltpu.roll(x, half, 1))
        return x * cos + partner * sin

    pb = jnp.concatenate([t[...] for t in tok_refs], axis=0)
    q = rot(pb[:, 0:width])
    k = rot(pb[:, width:2 * width]) * (RET_DK ** -0.5)
    v = pb[:, 2 * width:2 * width + BR_W]
    g_b = pb[:, 2 * width + BR_W:2 * width + 2 * BR_W]

    mask = mask_ref[...]
    causal = mask > 0.5
    dist = jnp.where(causal, (_iota((rows, rows), 0) - _iota((rows, rows), 1)).astype(F32), 0.0)
    t1 = jnp.sum(mask, axis=-1, keepdims=True)
    for h in range(RET_HEADS):
        log_gamma = math.log1p(-(2.0 ** (-5.0 - h)))
        dmat = jnp.where(causal, jnp.exp(dist * log_gamma), 0.0)
        q_h = q[:, h * RET_DK:(h + 1) * RET_DK]
        k_h = k[:, h * RET_DK:(h + 1) * RET_DK]
        v_h = v[:, h * RET_DV:(h + 1) * RET_DV]
        k_dec = k_h * jnp.exp((c - t1) * log_gamma)
        from_state = []
        for j in range(nb):
            js = slice(j * c, (j + 1) * c)
            s_h = s_ref[j, h]
            from_state.append(_mm(q_h[js], s_h))
            s_ref[j, h] = math.exp(c * log_gamma) * s_h + _mm_tn(k_dec[js], v_h[js])
        o_h = _mm(_mm_nt(q_h, k_h) * dmat, v_h) + jnp.exp(t1 * log_gamma) * jnp.concatenate(from_state, axis=0)
        o_n = o_h * lax.rsqrt(jnp.mean(o_h * o_h, axis=-1, keepdims=True) + NORM_EPS)
        o_ref[:, h * RET_DV:(h + 1) * RET_DV] = (o_n * _silu(g_b[:, h * RET_DV:(h + 1) * RET_DV])).astype(o_ref.dtype)


def _ret(pb, cos, sin, ret0, layer, *, o_prev, batch, seq, c, nb, row_block0, stacked):
    nc = seq // c
    assert batch % nb == 0
    width = RET_HEADS * RET_DK
    mask = jnp.asarray(_block_masks(nb, c)[0])
    io_specs, out_shape, out_specs, aliases, aliased = _mixer_io(
        RET_COLS, [(RET_HEADS, RET_DK, RET_DV)], o_prev=o_prev, batch=batch, c=c, nb=nb, nc=nc,
        row_block0=row_block0, stacked=stacked)
    return pl.pallas_call(
        functools.partial(_ret_kernel, c=c, nb=nb, n_alias=len(aliased)),
        out_shape=out_shape,
        grid=(batch // nb, nc),
        input_output_aliases=aliases,
        in_specs=[*io_specs,
                  pl.BlockSpec((c, width), lambda b, ci: (ci, 0)),
                  pl.BlockSpec((c, width), lambda b, ci: (ci, 0)),
                  pl.BlockSpec((nb * c, nb * c), lambda b, ci: (0, 0)),
                  pl.BlockSpec((None, nb, RET_HEADS, RET_DK, RET_DV), lambda b, ci: (layer, b, 0, 0, 0))],
        out_specs=out_specs,
        compiler_params=_cparams(2),
        name="retention",
    )(*aliased, pb, cos, sin, mask, ret0)


def _hgrn_kernel(*refs, c, m, nb, layer, n_chunks, n_alias):
    refs = refs[n_alias:]
    tok_refs = refs[:1]
    lbp_ref, ng_ref, tri_ref, s0_ref, o_ref, s_ref, st_scr = refs[1:]
    ci = pl.program_id(1)

    @pl.when(ci == 0)
    def _init():
        for j in range(nb):
            for h in range(HG_HEADS):
                st_scr[j, h] = s0_ref[j, h].T

    lbp = lbp_ref[...]
    e = jnp.exp(lbp - jnp.max(lbp, axis=0, keepdims=True))
    soft = e / jnp.sum(e, axis=0, keepdims=True)
    lb = jnp.zeros((1, BR_W), F32)
    for j in range(1, layer + 1):
        lb = lb + soft[j:j + 1, :]

    pc = jnp.concatenate([t[...] for t in tok_refs], axis=0)
    q_all = _silu(pc[:, 0:BR_W])
    f_in = pc[:, BR_W:2 * BR_W]
    v_all = pc[:, 2 * BR_W:3 * BR_W]
    g_c = pc[:, 3 * BR_W:4 * BR_W]
    log_f = jnp.log(lb + (1.0 - lb) * _sigmoid(f_in))
    k_all = (1.0 - lb) * _sigmoid(-f_in)
    cum_all = _mm_mask(tri_ref[...], log_f)

    col_m = _iota((m, m), 1)
    row_l = _iota((m, LANES), 0)
    outs = {}
    for j in range(nb):
        js = slice(j * c, (j + 1) * c)
        q, k, v, cum = q_all[js], k_all[js], v_all[js], cum_all[js]
        cum_last = cum[c - 1:c, :]
        q_e = q * jnp.exp(cum)
        k_e = k * jnp.exp(cum_last - cum)
        p_end = jnp.exp(cum_last)
        for h in range(HG_HEADS):
            sl = slice(h * HG_D, (h + 1) * HG_D)
            st = st_scr[j, h]
            o_rows = []
            for i in range(c // m):
                r0 = i * m
                q_i = q[r0:r0 + m, sl]
                c_i = cum[r0:r0 + m, sl]
                k_i = k[r0:r0 + m, sl]
                diag = jnp.zeros((m, m), F32)
                for s in range(m):
                    keep = row_l >= s
                    decay = jnp.exp(jnp.where(keep, c_i - c_i[s:s + 1, :], 0.0))
                    col = jnp.sum(jnp.where(keep, q_i * k_i[s:s + 1, :] * decay, 0.0), axis=-1, keepdims=True)
                    diag = jnp.where(col_m == s, col, diag)
                o_i = _mm(diag, v[r0:r0 + m, sl])
                if i > 0:
                    c_ref = cum[r0 - 1:r0, sl]
                    q_s = q_i * jnp.exp(c_i - c_ref)
                    k_s = k[0:r0, sl] * jnp.exp(c_ref - cum[0:r0, sl])
                    o_i = o_i + _mm(_mm_nt(q_s, k_s), v[0:r0, sl])
                o_rows.append(o_i)
            outs[(j, h)] = _mm_nt(q_e[:, sl], st) + jnp.concatenate(o_rows, axis=0)
            st_scr[j, h] = st * p_end[:, sl] + _mm_tn(v[:, sl], k_e[:, sl])
    for h in range(HG_HEADS):
        sl = slice(h * HG_D, (h + 1) * HG_D)
        o_h = jnp.concatenate([outs[(j, h)] for j in range(nb)], axis=0)
        o_n = o_h * lax.rsqrt(jnp.mean(o_h * o_h, axis=-1, keepdims=True) + NORM_EPS) * ng_ref[:, sl]
        o_ref[:, sl] = (o_n * _silu(g_c[:, sl])).astype(o_ref.dtype)

    @pl.when(ci == n_chunks - 1)
    def _fin():
        for j in range(nb):
            for h in range(HG_HEADS):
                s_ref[j, h] = st_scr[j, h].T


def _hgrn(pc, lbp, ng, hg0, state_layer, layer, *, o_prev, batch, seq, c, nb, row_block0, stacked):
    nc = seq // c
    assert batch % nb == 0
    m = 16 if c % 16 == 0 else SUBLANES
    depth = lbp.shape[0]
    tri = jnp.asarray(_block_masks(nb, c)[0]).astype(BF16)
    io_specs, out_shape, out_specs, aliases, aliased = _mixer_io(
        HG_COLS, [(HG_HEADS, HG_D, HG_D)], o_prev=o_prev, batch=batch, c=c, nb=nb, nc=nc,
        row_block0=row_block0, stacked=stacked)
    return pl.pallas_call(
        functools.partial(_hgrn_kernel, c=c, m=m, nb=nb, layer=layer, n_chunks=nc, n_alias=len(aliased)),
        out_shape=out_shape,
        grid=(batch // nb, nc),
        input_output_aliases=aliases,
        in_specs=[*io_specs,
                  pl.BlockSpec((depth, BR_W), lambda b, ci: (0, 0)),
                  pl.BlockSpec((1, BR_W), lambda b, ci: (0, 0)),
                  pl.BlockSpec((nb * c, nb * c), lambda b, ci: (0, 0)),
                  pl.BlockSpec((None, nb, HG_HEADS, HG_D, HG_D), lambda b, ci: (state_layer, b, 0, 0, 0))],
        out_specs=out_specs,
        scratch_shapes=[pltpu.VMEM((nb, HG_HEADS, HG_D, HG_D), F32)],
        compiler_params=_cparams(2),
        name="hgrn2",
    )(*aliased, pc, lbp, ng, tri, hg0)


def _gdn_kernel(*refs, c, nb, ns, n_alias):
    refs = refs[n_alias:]
    tok_refs = refs[:1]
    (cs0_ref, s0_ref, cw_ref, alog_ref, dtb_ref, ng_ref, mc_ref, mst_ref, tri_ref,
     o_ref, s_ref, cs_ref, ext_scr) = refs[1:]
    ci = pl.program_id(1)
    keep = GD_CONV - 1

    @pl.when(ci == 0)
    def _init():
        for j in range(nb):
            ext_scr[j, SUBLANES - keep:SUBLANES, :] = cs0_ref[j]
        s_ref[...] = s0_ref[...]

    convs = []
    for j in range(nb):
        x = tok_refs[0][j * c:(j + 1) * c, 0:GD_QKV]
        ext_scr[j, SUBLANES:SUBLANES + c, :] = x
        conv = ext_scr[j, SUBLANES - keep:SUBLANES - keep + c, :] * cw_ref[0:1, :]
        for i in range(1, keep):
            conv = conv + ext_scr[j, SUBLANES - keep + i:SUBLANES - keep + i + c, :] * cw_ref[i:i + 1, :]
        convs.append(conv + x * cw_ref[keep:keep + 1, :])
        tail = ext_scr[j, SUBLANES + c - keep:SUBLANES + c, :]
        ext_scr[j, SUBLANES - keep:SUBLANES, :] = tail
        cs_ref[j] = tail

    act = _silu(jnp.concatenate(convs, axis=0))
    q = act[:, 0:BR_W]
    k = act[:, BR_W:2 * BR_W]
    v = act[:, 2 * BR_W:3 * BR_W]
    gate_ba = jnp.concatenate([t[:, GD_QKV:GD_COLS_PAD] for t in tok_refs], axis=0)
    g_d = gate_ba[:, 0:BR_W]
    ba = gate_ba[:, BR_W:BR_W + LANES]
    beta_all = _sigmoid(ba)
    g_all = -jnp.exp(alog_ref[...]) * _softplus(ba + dtb_ref[...])

    cum_all = _mm_mask(tri_ref[...], g_all)

    ur = ns * c
    rows = GD_HEADS * ur
    stack = lambda parts: jnp.concatenate(parts, axis=0)
    l2n = lambda z: z * lax.rsqrt(jnp.sum(z * z, axis=-1, keepdims=True) + 1e-6)
    head = lambda z, h: z[:, h * GD_D:(h + 1) * GD_D]
    qn = jnp.concatenate([l2n(head(q, h)) for h in range(GD_HEADS)], axis=1) * (GD_D ** -0.5)
    kn = jnp.concatenate([l2n(head(k, h)) for h in range(GD_HEADS)], axis=1)
    units = range(nb // ns)
    heads = range(GD_HEADS)
    stack_u = lambda z, un: stack([head(z[un * ur:(un + 1) * ur], h) for h in heads])
    col_u = lambda z, lane0, un: stack([z[un * ur:(un + 1) * ur, lane0 + h:lane0 + h + 1] for h in heads])
    q4 = [stack_u(qn, un) for un in units]
    k4 = [stack_u(kn, un) for un in units]
    v4 = [stack_u(v, un) for un in units]
    beta = [col_u(beta_all, 0, un) for un in units]
    cum = [col_u(cum_all, GD_HEADS, un) for un in units]
    causal = mc_ref[...] > 0.5
    m_strict = mst_ref[...]
    eye = _iota((rows, rows), 0) == _iota((rows, rows), 1)
    ones = jnp.ones((rows, rows), BF16)
    cum_t = [jnp.broadcast_to(cm, (rows, rows)) for cm in cum]
    cum_s = [_mm_mask(ones, jnp.where(eye, ct, 0.0)) for ct in cum_t]
    dmat = [jnp.where(causal, jnp.exp(jnp.where(causal, ct - cs_, 0.0)), 0.0) for ct, cs_ in zip(cum_t, cum_s)]
    a_mat = [beta[un] * _mm_nt(k4[un], k4[un]) * (dmat[un] * m_strict) for un in units]
    inv = _unit_lower_inverse([-a for a in a_mat], c)
    e_cum = [jnp.exp(cm) for cm in cum]
    sol = [_mm(inv[un], jnp.concatenate([(beta[un] * e_cum[un]) * k4[un], beta[un] * v4[un]], axis=1)) for un in units]
    blocks = [(h, i) for h in heads for i in range(ns)]
    rows_of = lambda h, i: slice((h * ns + i) * c, (h * ns + i + 1) * c)
    qw = [[_mm(jnp.concatenate([q4[un][rows_of(h, i)], sol[un][rows_of(h, i), 0:GD_D]], axis=0), s_ref[un * ns + i, h])
           for h, i in blocks] for un in units]
    delta = [sol[un][:, GD_D:2 * GD_D] - stack([z[c:2 * c] for z in qw[un]]) for un in units]
    o4 = [e_cum[un] * stack([z[0:c] for z in qw[un]]) + _mm(_mm_nt(q4[un], k4[un]) * dmat[un], delta[un])
          for un in units]
    for un in units:
        for h, i in blocks:
            hs = rows_of(h, i)
            cum_h = cum[un][hs]
            cum_last = cum_h[c - 1:c, :]
            s_ref[un * ns + i, h] = (jnp.exp(cum_last) * s_ref[un * ns + i, h]
                                     + _mm_tn(k4[un][hs] * jnp.exp(cum_last - cum_h), delta[un][hs]))
    for h in heads:
        sl = slice(h * GD_D, (h + 1) * GD_D)
        o_h = stack([o4[un][h * ur:(h + 1) * ur] for un in units])
        o_n = o_h * lax.rsqrt(jnp.mean(o_h * o_h, axis=-1, keepdims=True) + NORM_EPS) * ng_ref[:, sl]
        o_ref[:, sl] = (o_n * _silu(g_d[:, sl])).astype(o_ref.dtype)


def _gdn(pd, conv0, gd0, layer, cw, alog, dtb, ng, *, o_prev, batch, seq, c, nb, ns, row_block0, stacked):
    nc = seq // c
    assert batch % nb == 0 and nb % ns == 0
    keep = GD_CONV - 1
    rows = GD_HEADS * ns * c
    m_causal, m_strict = (jnp.asarray(m) for m in _block_masks(GD_HEADS * ns, c))
    tri = jnp.asarray(_block_masks(nb, c)[0]).astype(BF16)
    full = lambda *shape: pl.BlockSpec(shape, lambda b, ci: (0,) * len(shape))
    io_specs, out_shape, out_specs, aliases, aliased = _mixer_io(
        GD_COLS_PAD, [(GD_HEADS, GD_D, GD_D), (keep, GD_QKV)], o_prev=o_prev, batch=batch, c=c, nb=nb, nc=nc,
        row_block0=row_block0, stacked=stacked)
    return pl.pallas_call(
        functools.partial(_gdn_kernel, c=c, nb=nb, ns=ns, n_alias=len(aliased)),
        out_shape=out_shape,
        grid=(batch // nb, nc),
        input_output_aliases=aliases,
        in_specs=[*io_specs,
                  pl.BlockSpec((None, nb, keep, GD_QKV), lambda b, ci: (layer, b, 0, 0)),
                  pl.BlockSpec((None, nb, GD_HEADS, GD_D, GD_D), lambda b, ci: (layer, b, 0, 0, 0)),
                  full(GD_CONV, GD_QKV), full(1, LANES), full(1, LANES), full(1, BR_W),
                  full(rows, rows), full(rows, rows), full(nb * c, nb * c)],
        out_specs=out_specs,
        scratch_shapes=[pltpu.VMEM((nb, SUBLANES + c, GD_QKV), F32)],
        compiler_params=_cparams(2),
        name="gdn",
    )(*aliased, pd, conv0, gd0, cw, alog, dtb, ng, m_causal, m_strict, tri)


def _pick_chunk(seq):
    for c in (64, 56, 48, 40, 32, 24, 16, 8):
        if seq % c == 0:
            return c
    raise ValueError(f"sequence length {seq} is not a multiple of {SUBLANES}")


def _pick_tile(n, candidates):
    for t in candidates:
        if n % t == 0:
            return t
    raise ValueError(f"no tile in {candidates} divides {n}")


def _rope_tables(pos):
    half = RET_DK // 2
    inv = ROPE_BASE ** (-jnp.arange(half, dtype=F32) / half)
    ang = pos.astype(F32)[:, None] * inv[None, :]
    cos, sin = jnp.cos(ang), jnp.sin(ang)
    cos_t = jnp.tile(jnp.concatenate([cos, cos], axis=1), (1, RET_HEADS))
    sin_t = jnp.tile(jnp.concatenate([-sin, sin], axis=1), (1, RET_HEADS))
    return cos_t, sin_t


def kernel(x_prompt, x_sample, state_rwkv_wkv, state_rwkv_shift, state_ret, state_hgrn, state_gdn, state_gdn_conv, meta_tokens, norm_mix, w_in, rw_mu, rw_w0, rw_w2, rw_a0, rw_a2, rw_g2, rw_kk, rw_ka, rw_rk, rw_ln_g, rw_ln_b, hg_lb, hg_norm_g, gd_conv, gd_a_log, gd_dt_bias, gd_norm_g, w_branch, w_out, norm_ffn, w_up, w_down, norm_final):
    depth = norm_mix.shape[0]
    bp, tq, d = x_prompt.shape
    tp = tq + N_META
    bs, ts, _ = x_sample.shape
    n_p, n_s = bp * tp, bs * ts
    n_tok = n_p + n_s
    cp, cs = _pick_chunk(tp), _pick_chunk(ts)
    tm = _pick_tile(n_tok, (464, 512, 256, 232, 128, 64, 32, 16, 8))
    tm_in = _pick_tile(n_tok, (928, 464, 512, 256, 232, 128, 64, 32, 16, 8))
    nb_s = _pick_tile(bs, (8, 4, 2, 1))
    nb_p = _pick_tile(bp, (4, 2, 1))
    ncp, gp = tp // cp, bp // nb_p
    assert cs == ts and n_p % (nb_s * ts) == 0

    meta = jnp.broadcast_to(meta_tokens.astype(F32)[None], (bp, N_META, d))
    xp = jnp.concatenate([meta, x_prompt], axis=1).reshape(gp, nb_p, ncp, cp, d)
    x = jnp.concatenate([xp.transpose(0, 2, 1, 3, 4).reshape(n_p, d), x_sample.reshape(n_s, d)], axis=0)

    cos_p, sin_p = _rope_tables(jnp.arange(tp, dtype=jnp.int32))
    cos_s, sin_s = _rope_tables(PAST_LEN + jnp.arange(ts, dtype=jnp.int32))

    head_of = jnp.arange(BR_W) // RW_HEAD
    bd = (head_of[:, None] == head_of[None, :]).astype(BF16)

    zeros = lambda *s: jnp.zeros((1, bp) + s, F32)
    z_wkv, z_shift = zeros(RW_HEADS, RW_HEAD, RW_HEAD), zeros(1, RW_COLS)
    z_ret, z_hg, z_gd = zeros(RET_HEADS, RET_DK, RET_DV), zeros(HG_HEADS, HG_D, HG_D), zeros(GD_HEADS, GD_D, GD_D)
    z_conv = zeros(GD_CONV - 1, GD_QKV)
    shift_s = state_rwkv_shift.reshape(depth, bs, 1, RW_COLS)

    assert w_in.shape[2] == IN_G + GATE_COLS
    w_a, w_b, w_c, w_d, w_g = _w_in_prep(jnp.swapaxes(w_in, 1, 2), LANES)
    wb_all, wo_all = w_branch.astype(BF16), w_out.astype(BF16)
    wu_all, wd_all = w_up.astype(BF16), w_down.astype(BF16)

    tails = ((RW_HEADS, RW_HEAD, RW_HEAD), (1, RW_COLS), (RET_HEADS, RET_DK, RET_DV), (HG_HEADS, HG_D, HG_D),
             (GD_HEADS, GD_D, GD_D), (GD_CONV - 1, GD_QKV))
    stacks = [[jnp.zeros((depth, batch) + tail, F32) for tail in tails] for batch in (bp, bs)]
    h = _rmsnorm(x, norm_mix[0], tm, BF16)
    for l in range(depth):
        pa = _matmul(h, w_a, l, tm_in, 896, "in_proj_a")
        pb = _matmul(h, w_b, l, tm_in, 768, "in_proj_b")
        pc = _matmul(h, w_c, l, tm_in, 1024, "in_proj_c")
        pd = _matmul(h, w_d, l, tm_in, GD_COLS_PAD, "in_proj_d")
        gates = _matmul(h, w_g, l, tm_in, 2048, "in_proj_g", gate=True)

        row1 = lambda a: a.reshape(1, -1)
        pad_rows = lambda a, top: jnp.concatenate(
            [jnp.zeros((top, BR_W), F32), a, jnp.zeros((LANES - top - a.shape[0], BR_W), F32)], axis=0).astype(BF16)
        rw_prm = (row1(rw_mu[l]), row1(rw_w0[l]), pad_rows(rw_w2[l], 0), row1(rw_a0[l]), pad_rows(rw_a2[l], 64),
                  rw_g2[l].astype(BF16), row1(rw_kk[l]), row1(rw_ka[l]), row1(rw_rk[l]), row1(rw_ln_g[l]),
                  row1(rw_ln_b[l]), bd)
        lane_pad = lambda a: jnp.concatenate(
            [jnp.zeros((GD_HEADS,), F32), a, jnp.zeros((LANES - 2 * GD_HEADS,), F32)]).reshape(1, LANES)
        gd_prm = (gd_conv[l], lane_pad(gd_a_log[l]), lane_pad(gd_dt_bias[l]), row1(gd_norm_g[l]))
        hg_g = row1(hg_norm_g[l])

        if l == 0:
            oa = ob = oc = od = jnp.zeros((n_tok, BR_W), BF16)
        for grp, (batch, seq, c, nb, ns, rb0, sts, lay, cos, sin) in enumerate((
                (bp, tp, cp, nb_p, 1, 0, (z_wkv, z_shift, z_ret, z_hg, z_gd, z_conv), 0, cos_p, sin_p),
                (bs, ts, cs, nb_s, nb_s, n_p // (nb_s * ts),
                 (state_rwkv_wkv, shift_s, state_ret, state_hgrn, state_gdn, state_gdn_conv), l, cos_s, sin_s))):
            kw = dict(batch=batch, seq=seq, c=c, nb=nb, row_block0=rb0)
            prev = stacks[grp]
            stk = lambda *arrays: (depth, l, arrays)
            oa, wkv, shift = _rwkv(pa, sts[1], sts[0], lay, rw_prm, o_prev=oa, ns=ns, stacked=stk(*prev[0:2]), **kw)
            ob, ret = _ret(pb, cos, sin, sts[2], lay, o_prev=ob, stacked=stk(*prev[2:3]), **kw)
            oc, hg = _hgrn(pc, hg_lb, hg_g, sts[3], lay, l, o_prev=oc, stacked=stk(*prev[3:4]), **kw)
            od, gd, conv = _gdn(pd, sts[5], sts[4], lay, *gd_prm, o_prev=od, ns=ns, stacked=stk(*prev[4:6]), **kw)
            stacks[grp] = [wkv, shift, ret, hg, gd, conv]
        x = _merge(x, (oa, ob, oc, od), gates, wb_all, wo_all, l, tm_in, 256)
        last = l == depth - 1
        res = _ffn(x, norm_ffn[l], norm_final if last else norm_mix[l + 1], wu_all, wd_all, l, tm, 512, last)
        if last:
            y = res
        else:
            x, h = res

    y_prompt = y[:n_p].reshape(gp, ncp, nb_p, cp, d).transpose(0, 2, 1, 3, 4).reshape(bp, tp, d)[:, N_META:]
    y_sample = y[n_p:].reshape(bs, ts, d)
    states = []
    for batch, (wkv, shift, ret, hg, gd, conv) in zip((bp, bs), stacks):
        states += [wkv, shift.reshape(depth, batch, RW_COLS), ret, hg, gd, conv]
    return (y_prompt, y_sample, *states)
```

```python
import functools
import math

import jax
import jax.numpy as jnp
import numpy as np
from jax import lax
from jax.experimental import pallas as pl
from jax.experimental.pallas import tpu as pltpu

F32 = jnp.float32
BF16 = jnp.bfloat16

N_META = 16
PAST_LEN = 16384
NORM_EPS = 1e-6
RW_LN_EPS = 64e-5
ROPE_BASE = 10000.0

D_MODEL = 2048
BR_W = 512
RW_HEADS, RW_HEAD = 8, 64
RW_COLS = 1792
RW_GROUP = 4
RW_GW = RW_GROUP * RW_HEAD
RET_HEADS, RET_DK, RET_DV = 4, 64, 128
RET_COLS = 1536
HG_HEADS, HG_D = 4, 128
HG_COLS = 2048
GD_HEADS, GD_D = 4, 128
GD_CONV = 4
GD_QKV = 1536
GD_COLS_PAD = 2176

LANES = 128
SUBLANES = 8
VMEM_LIMIT = 56 * 1024 * 1024


def _cparams(n_axes):
    return pltpu.CompilerParams(dimension_semantics=("arbitrary",) * n_axes, vmem_limit_bytes=VMEM_LIMIT)


def _dg(a, b, ca, cb):
    return lax.dot_general(a, b, (((ca,), (cb,)), ((), ())), preferred_element_type=F32)


def _mm(a, b):
    return _dg(a.astype(BF16), b.astype(BF16), 1, 0)


def _mm_nt(a, b):
    return _dg(a.astype(BF16), b.astype(BF16), 1, 1)


def _mm_tn(a, b):
    return _dg(a.astype(BF16), b.astype(BF16), 0, 0)


def _split(x, n):
    parts, r = [], x
    for _ in range(n):
        p = r.astype(BF16)
        parts.append(p)
        r = r - p.astype(F32)
    return parts


def _mm_mask(m01, x):
    x0, x1, x2 = _split(x, 3)
    return _dg(m01, x0, 1, 0) + (_dg(m01, x1, 1, 0) + _dg(m01, x2, 1, 0))


def _iota(shape, dim):
    return lax.broadcasted_iota(jnp.int32, shape, dim)


def _softplus(x):
    return jnp.maximum(x, 0.0) + jnp.log(1.0 + jnp.exp(-jnp.abs(x)))


def _sigmoid(x):
    return jax.nn.sigmoid(x)


def _silu(x):
    return x * jax.nn.sigmoid(x)


def _unit_lower_inverse(lows, c):
    shape = lows[0].shape
    eye = (_iota(shape, 0) == _iota(shape, 1)).astype(F32)
    invs = [eye + low for low in lows]
    powers = list(lows)
    covered = 1
    while covered < c - 1:
        powers = [_mm(p, p) for p in powers]
        invs = [inv + _mm(inv, p) for inv, p in zip(invs, powers)]
        covered = 2 * covered + 1
    return invs


def _rmsnorm_kernel(x_ref, g_ref, o_ref):
    x = x_ref[...]
    y = x * lax.rsqrt(jnp.mean(x * x, axis=-1, keepdims=True) + NORM_EPS)
    o_ref[...] = (y * g_ref[...]).astype(o_ref.dtype)


def _rmsnorm(x, g, tm, out_dtype):
    n, d = x.shape
    return pl.pallas_call(
        _rmsnorm_kernel,
        out_shape=jax.ShapeDtypeStruct((n, d), out_dtype),
        grid=(n // tm,),
        in_specs=[pl.BlockSpec((tm, d), lambda i: (i, 0)), pl.BlockSpec((1, d), lambda i: (0, 0))],
        out_specs=pl.BlockSpec((tm, d), lambda i: (i, 0)),
        compiler_params=_cparams(1),
        name="rmsnorm",
    )(x, g.reshape(1, d))


def _matmul_kernel(x_ref, w_ref, o_ref):
    o_ref[...] = _dg(x_ref[...], w_ref[...], 1, 1)


def _matmul_sigmoid_kernel(x_ref, w_ref, o_ref):
    o_ref[...] = _sigmoid(_dg(x_ref[...], w_ref[...], 1, 1)).astype(o_ref.dtype)


def _matmul(x, w, layer, tm, tn, name, gate=False):
    n, k = x.shape
    m = w.shape[1]
    return pl.pallas_call(
        _matmul_sigmoid_kernel if gate else _matmul_kernel,
        out_shape=jax.ShapeDtypeStruct((n, m), BF16 if gate else F32),
        grid=(n // tm, m // tn),
        in_specs=[pl.BlockSpec((tm, k), lambda i, j: (i, 0)), pl.BlockSpec((None, tn, k), lambda i, j: (layer, j, 0))],
        out_specs=pl.BlockSpec((tm, tn), lambda i, j: (i, j)),
        compiler_params=_cparams(2),
        name=name,
    )(x, w)


IN_A, IN_B, IN_C = RW_COLS, RW_COLS + RET_COLS, RW_COLS + RET_COLS + HG_COLS
IN_BA = IN_C + GD_QKV
IN_G = IN_BA + 2 * GD_HEADS + BR_W
GATE_COLS = 4 * D_MODEL


def _w_in_prep_kernel(w_ref, a_ref, b_ref, c_ref, d_ref, g_ref):
    cast = lambda lo, hi: w_ref[lo:hi, :].astype(BF16)
    a_ref[...] = cast(0, IN_A)
    b_ref[...] = cast(IN_A, IN_B)
    c_ref[...] = cast(IN_B, IN_C)
    d_ref[0:GD_QKV, :] = cast(IN_C, IN_BA)
    d_ref[GD_QKV:GD_QKV + BR_W, :] = cast(IN_BA + 2 * GD_HEADS, IN_G)
    ba = jnp.concatenate([w_ref[IN_BA:IN_BA + 2 * GD_HEADS, :],
                          jnp.zeros((LANES - 2 * GD_HEADS, w_ref.shape[1]), F32)], axis=0)
    d_ref[GD_QKV + BR_W:GD_COLS_PAD, :] = ba.astype(BF16)
    step = 1024
    for lo in range(0, GATE_COLS, step):
        g_ref[lo:lo + step, :] = cast(IN_G + lo, IN_G + lo + step)


def _w_in_prep(w_in_t, tk):
    depth, n, k = w_in_t.shape
    out = lambda cols: (jax.ShapeDtypeStruct((depth, cols, k), BF16), pl.BlockSpec((None, cols, tk), lambda l, i: (l, 0, i)))
    shapes, specs = zip(*(out(cols) for cols in (RW_COLS, RET_COLS, HG_COLS, GD_COLS_PAD, GATE_COLS)))
    return pl.pallas_call(
        _w_in_prep_kernel,
        out_shape=shapes,
        grid=(depth, k // tk),
        in_specs=[pl.BlockSpec((None, n, tk), lambda l, i: (l, 0, i))],
        out_specs=specs,
        compiler_params=_cparams(2),
        name="w_in_prep",
    )(w_in_t)


def _merge_kernel(x_ref, oa_ref, ob_ref, oc_ref, od_ref, g0_ref, g1_ref, g2_ref, g3_ref, wb_ref, wo_ref, gf_ref,
                  y_ref, h_ref):
    j = pl.program_id(1)

    @pl.when(j == 0)
    def _init():
        y_ref[...] = x_ref[...]

    merged = None
    for n, (o_ref, g_ref) in enumerate(((oa_ref, g0_ref), (ob_ref, g1_ref), (oc_ref, g2_ref), (od_ref, g3_ref))):
        term = g_ref[...].astype(F32) * _dg(o_ref[...], wb_ref[n], 1, 0)
        merged = term if merged is None else merged + term
    y_ref[...] += _dg(merged.astype(BF16), wo_ref[...], 1, 0)

    @pl.when(j == pl.num_programs(1) - 1)
    def _fin():
        h_ref[...] = _rms(y_ref[...], gf_ref[...]).astype(BF16)


def _merge(x, outs, pg, wb, wo, g_ffn, layer, tm, tj):
    n, d = x.shape
    nj = d // tj
    o_spec = pl.BlockSpec((tm, BR_W), lambda i, j: (i, 0))
    g_specs = [pl.BlockSpec((tm, tj), functools.partial(lambda i, j, nb: (i, nb * nj + j), nb=nb)) for nb in range(4)]
    return pl.pallas_call(
        _merge_kernel,
        out_shape=(jax.ShapeDtypeStruct((n, d), F32), jax.ShapeDtypeStruct((n, d), BF16)),
        grid=(n // tm, nj),
        in_specs=[pl.BlockSpec((tm, d), lambda i, j: (i, 0)), o_spec, o_spec, o_spec, o_spec, *g_specs,
                  pl.BlockSpec((None, 4, BR_W, tj), lambda i, j: (layer, 0, 0, j)),
                  pl.BlockSpec((None, tj, d), lambda i, j: (layer, j, 0)),
                  pl.BlockSpec((1, d), lambda i, j: (0, 0))],
        out_specs=(pl.BlockSpec((tm, d), lambda i, j: (i, 0)), pl.BlockSpec((tm, d), lambda i, j: (i, 0))),
        compiler_params=_cparams(2),
        name="merge",
    )(x, *outs, pg, pg, pg, pg, wb, wo, g_ffn.reshape(1, d))


def _rms(x, g):
    return x * lax.rsqrt(jnp.mean(x * x, axis=-1, keepdims=True) + NORM_EPS) * g


def _ffn_kernel(x_ref, h_ref, gn_ref, wu_ref, wg_ref, wd_ref, *outs, last):
    y_ref = outs[0]
    j = pl.program_id(1)

    @pl.when(j == 0)
    def _init():
        y_ref[...] = x_ref[...]

    h = h_ref[...]
    up = _dg(h, wu_ref[...], 1, 0)
    gate = _dg(h, wg_ref[...], 1, 0)
    y_ref[...] += _dg((_silu(gate) * up).astype(BF16), wd_ref[...], 1, 0)

    @pl.when(j == pl.num_programs(1) - 1)
    def _fin():
        if last:
            y_ref[...] = _rms(y_ref[...], gn_ref[...])
        else:
            outs[1][...] = _rms(y_ref[...], gn_ref[...]).astype(BF16)


def _ffn(x, h, g_next, w_up, w_down, layer, tm, tf, last):
    n, d = x.shape
    ff = w_down.shape[1]
    nf = ff // tf
    row_spec = pl.BlockSpec((tm, d), lambda i, j: (i, 0))
    vec_spec = pl.BlockSpec((1, d), lambda i, j: (0, 0))
    if last:
        out_shape, out_specs = jax.ShapeDtypeStruct((n, d), F32), row_spec
    else:
        out_shape = (jax.ShapeDtypeStruct((n, d), F32), jax.ShapeDtypeStruct((n, d), BF16))
        out_specs = (row_spec, row_spec)
    return pl.pallas_call(
        functools.partial(_ffn_kernel, last=last),
        out_shape=out_shape,
        grid=(n // tm, nf),
        in_specs=[row_spec, row_spec, vec_spec,
                  pl.BlockSpec((None, d, tf), lambda i, j: (layer, 0, j)),
                  pl.BlockSpec((None, d, tf), lambda i, j: (layer, 0, nf + j)),
                  pl.BlockSpec((None, tf, d), lambda i, j: (layer, j, 0))],
        out_specs=out_specs,
        compiler_params=_cparams(2),
        name="ffn",
    )(x, h, g_next.reshape(1, d), w_up, w_up, w_down)


def _rwkv_kernel(*refs, c, nb, ns, n_chunks, rw_pad, n_alias):
    refs = refs[n_alias:]
    tok_refs = refs[:1]
    (sh0_ref, s0_ref, mu_ref, w0_ref, w2_ref, a0_ref, a2_ref, g2_ref, kkw_ref, ka_ref, rk_ref, lng_ref, lnb_ref,
     bd_ref, hm_ref, ms_ref, mi_ref, tri_ref, o_ref, s_ref, sh_ref, xs_scr, sbd_scr) = refs[1:]
    ci = pl.program_id(1)
    seqs = range(nb)
    head_block = lambda h: (h // RW_GROUP, slice((h % RW_GROUP) * RW_HEAD, (h % RW_GROUP + 1) * RW_HEAD))

    @pl.when(ci == 0)
    def _init():
        sbd_scr[...] = jnp.zeros_like(sbd_scr)
        for j in seqs:
            xs_scr[j, SUBLANES - 1:SUBLANES, :] = sh0_ref[j]
            for h in range(RW_HEADS):
                g, sl = head_block(h)
                sbd_scr[j, g, sl, sl] = s0_ref[j, h]

    pa = jnp.concatenate([t[...] for t in tok_refs], axis=0)
    prevs = []
    for j in seqs:
        pa_j = pa[j * c:(j + 1) * c, :]
        xs_scr[j, SUBLANES:SUBLANES + c, :] = pa_j
        prevs.append(xs_scr[j, SUBLANES - 1:SUBLANES - 1 + c, :])
        last = pa_j[c - 1:c, :]
        xs_scr[j, SUBLANES - 1:SUBLANES, :] = last
        sh_ref[j] = last
    prev = jnp.concatenate(prevs, axis=0)

    xm = pa + (prev - pa) * mu_ref[...]
    r = xm[:, 0:512]
    k0 = xm[:, 512:1024]
    v = xm[:, 1024:1536]
    wa = xm[:, 1536:1664]
    gl = xm[:, 1664:1792]
    w_log = -_softplus(-(w0_ref[...] + _mm(jnp.tanh(wa), w2_ref[...]))) - 0.5
    ld = -jnp.exp(w_log)
    asig = _sigmoid(a0_ref[...] + _mm(wa, a2_ref[...]))
    g_a = _mm(_sigmoid(gl), g2_ref[...])
    bd = bd_ref[...]
    kkr = k0 * kkw_ref[...]
    kkn = kkr * lax.rsqrt(_mm(kkr * kkr, bd) + 1e-6)
    k1 = k0 * (1.0 + (asig - 1.0) * ka_ref[...])
    a_vec = -kkn
    b_vec = kkn * asig

    cum = _mm_mask(tri_ref[...], ld)
    cum_ends = [cum[(j + 1) * c - 1:(j + 1) * c, :] for j in seqs]
    cum_last = jnp.concatenate([jnp.broadcast_to(e, (c, BR_W)) for e in cum_ends], axis=0)
    a_t = a_vec * jnp.exp(cum - ld)
    r_t = r * jnp.exp(cum)
    inv_p = jnp.exp(-cum)
    b_t = b_vec * inv_p
    k_t = k1 * inv_p
    to_end = jnp.exp(cum_last - cum)
    b_e = b_vec * to_end
    k_e = k1 * to_end

    seq_rows = RW_GROUP * c
    rows = ns * seq_rows
    pad = [jnp.zeros((rw_pad - rows, RW_GW), F32)] if rw_pad > rows else []
    m_strict = ms_ref[...] > 0.5
    m_incl = mi_ref[...] > 0.5
    groups = range(RW_HEADS // RW_GROUP)
    units = [(s0, g) for s0 in range(0, nb, ns) for g in groups]
    lanes = lambda g: slice(g * RW_GW, (g + 1) * RW_GW)
    of_seq = lambda x, i: x[i * seq_rows:(i + 1) * seq_rows]

    def stack(x, unit):
        s0, g = unit
        return jnp.concatenate([x[j * c:(j + 1) * c, lanes(g)] * hm_ref[hh:hh + 1, :]
                                for j in range(s0, s0 + ns) for hh in range(RW_GROUP)], axis=0)

    a4 = [stack(a_t, un) for un in units]
    r4 = [stack(r_t, un) for un in units]
    v4 = [stack(v, un) for un in units]
    gram = [_mm_nt(jnp.concatenate([a4[n], r4[n]], axis=0),
                   jnp.concatenate([stack(b_t, un), *pad, stack(k_t, un), *pad], axis=0))
            for n, un in enumerate(units)]
    from_state = [[_mm_nt(jnp.concatenate([of_seq(a4[n], i), of_seq(r4[n], i)], axis=0), sbd_scr[s0 + i, g])
                   for i in range(ns)] for n, (s0, g) in enumerate(units)]
    fs_a = [jnp.concatenate([f[0:seq_rows] for f in fs], axis=0) for fs in from_state]
    fs_r = [jnp.concatenate([f[seq_rows:2 * seq_rows] for f in fs], axis=0) for fs in from_state]
    low2 = [jnp.where(m_strict, gm[0:rows], 0.0) for gm in gram]
    rhs_u = [fs_a[n] + _mm(low2[n][:, rw_pad:2 * rw_pad], jnp.concatenate([v4[n], *pad], axis=0))
             for n in range(len(units))]
    inv = _unit_lower_inverse([lw[:, 0:rows] for lw in low2], c)
    u = [_mm(inv[n], rhs_u[n]) for n in range(len(units))]
    o_unit = {}
    for n, (s0, g) in enumerate(units):
        uv = jnp.concatenate([u[n], *pad, v4[n], *pad], axis=0)
        o4 = fs_r[n] + _mm(jnp.where(m_incl, gram[n][rows:2 * rows], 0.0), uv)
        for i in range(ns):
            o_j = o4[i * seq_rows:i * seq_rows + c]
            for hh in range(1, RW_GROUP):
                o_j = o_j + o4[i * seq_rows + hh * c:i * seq_rows + (hh + 1) * c]
            o_unit[(s0 + i, g)] = o_j
        b4 = stack(b_e, (s0, g))
        k4 = stack(k_e, (s0, g))
        for i in range(ns):
            ends = jnp.concatenate([of_seq(b4, i), of_seq(k4, i)], axis=0)
            uv_j = jnp.concatenate([of_seq(u[n], i), of_seq(v4[n], i)], axis=0)
            sbd_scr[s0 + i, g] = sbd_scr[s0 + i, g] * jnp.exp(cum_ends[s0 + i][:, lanes(g)]) + _mm_tn(uv_j, ends)

    @pl.when(ci == n_chunks - 1)
    def _fin():
        for j in seqs:
            for h in range(RW_HEADS):
                g, sl = head_block(h)
                s_ref[j, h] = sbd_scr[j, g, sl, sl]

    o = jnp.concatenate([jnp.concatenate([o_unit[(j, g)] for g in groups], axis=1) for j in seqs], axis=0)
    inv_n = 1.0 / RW_HEAD
    mean = _mm(o, bd) * inv_n
    dev = o - mean
    var = _mm(dev * dev, bd) * inv_n
    normed = dev * lax.rsqrt(var + RW_LN_EPS) * lng_ref[...] + lnb_ref[...]
    bonus = _mm(r * k1 * rk_ref[...], bd) * v
    out = (normed + bonus) * g_a
    o_ref[...] = out.astype(o_ref.dtype)


def _block_masks(n_blocks, c):
    r = np.arange(n_blocks * c)
    same = (r[:, None] // c) == (r[None, :] // c)
    causal = (same & (r[None, :] <= r[:, None])).astype(np.float32)
    strict = (same & (r[None, :] < r[:, None])).astype(np.float32)
    return causal, strict


def _rwkv_masks(c, nb):
    rows = nb * RW_GROUP * c
    rw_pad = -(-rows // LANES) * LANES
    lane_head = np.arange(RW_GW) // RW_HEAD
    hm = (lane_head[None, :] == np.arange(RW_GROUP)[:, None]).astype(np.float32)
    incl, strict = _block_masks(nb * RW_GROUP, c)
    widen = lambda m: np.tile(np.pad(m, ((0, 0), (0, rw_pad - rows))), (1, 2))
    return rw_pad, jnp.asarray(hm), jnp.asarray(widen(strict)), jnp.asarray(widen(incl))


ANY_SPEC = pl.BlockSpec(memory_space=pl.ANY)


def _mixer_io(cols, tails, *, o_prev, batch, c, nb, nc, row_block0, stacked):
    zeros = lambda t: (0,) * len(t)
    depth, layer, prev = stacked
    prev = list(prev)
    st_shapes = [jax.ShapeDtypeStruct((depth, batch) + t, F32) for t in tails]
    st_specs = [pl.BlockSpec((None, nb) + t, functools.partial(lambda b, ci, z: (layer, b) + z, z=zeros(t)))
                for t in tails]
    tok_idx = lambda b, ci: (row_block0 + b * nc + ci, 0)
    aliased = [o_prev, *prev]
    in_specs = [ANY_SPEC] * len(aliased) + [pl.BlockSpec((nb * c, cols), tok_idx)]
    out_shape = (jax.ShapeDtypeStruct(o_prev.shape, o_prev.dtype), *st_shapes)
    out_specs = (pl.BlockSpec((nb * c, BR_W), tok_idx), *st_specs)
    return in_specs, out_shape, out_specs, {i: i for i in range(len(aliased))}, aliased


def _rwkv(pa, shift0, wkv0, layer, prm, *, o_prev, batch, seq, c, nb, ns, row_block0, stacked):
    nc = seq // c
    assert batch % nb == 0 and nb % ns == 0
    rw_pad, hm, m_strict, m_incl = _rwkv_masks(c, ns)
    rows = ns * RW_GROUP * c
    tri = jnp.asarray(_block_masks(nb, c)[0]).astype(BF16)
    io_specs, out_shape, out_specs, aliases, aliased = _mixer_io(
        RW_COLS, [(RW_HEADS, RW_HEAD, RW_HEAD), (1, RW_COLS)], o_prev=o_prev, batch=batch, c=c, nb=nb, nc=nc,
        row_block0=row_block0, stacked=stacked)
    kern = functools.partial(_rwkv_kernel, c=c, nb=nb, ns=ns, n_chunks=nc, rw_pad=rw_pad, n_alias=len(aliased))
    vec = lambda n: pl.BlockSpec((1, n), lambda b, ci: (0, 0))
    mat = lambda r_, n: pl.BlockSpec((r_, n), lambda b, ci: (0, 0))
    prm = tuple(prm) + (hm, m_strict, m_incl, tri)
    return pl.pallas_call(
        kern,
        out_shape=out_shape,
        grid=(batch // nb, nc),
        input_output_aliases=aliases,
        in_specs=[*io_specs,
                  pl.BlockSpec((None, nb, 1, RW_COLS), lambda b, ci: (layer, b, 0, 0)),
                  pl.BlockSpec((None, nb, RW_HEADS, RW_HEAD, RW_HEAD), lambda b, ci: (layer, b, 0, 0, 0)),
                  vec(RW_COLS), vec(512), mat(128, 512), vec(512), mat(128, 512), mat(128, 512),
                  vec(512), vec(512), vec(512), vec(512), vec(512), mat(512, 512),
                  mat(RW_GROUP, RW_GW), mat(rows, 2 * rw_pad), mat(rows, 2 * rw_pad), mat(nb * c, nb * c)],
        out_specs=out_specs,
        scratch_shapes=[pltpu.VMEM((nb, SUBLANES + c, RW_COLS), F32),
                        pltpu.VMEM((nb, RW_HEADS // RW_GROUP, RW_GW, RW_GW), F32)],
        compiler_params=_cparams(2),
        name="rwkv7",
    )(*aliased, pa, shift0, wkv0, *prm)


def _ret_kernel(*refs, c, nb, n_alias):
    refs = refs[n_alias:]
    tok_refs = refs[:1]
    cos_ref, sin_ref, mask_ref, s0_ref, o_ref, s_ref = refs[1:]
    ci = pl.program_id(1)

    @pl.when(ci == 0)
    def _init():
        s_ref[...] = s0_ref[...]

    rows = nb * c
    width = RET_HEADS * RET_DK
    half = RET_DK // 2
    cos = jnp.concatenate([cos_ref[...]] * nb, axis=0)
    sin = jnp.concatenate([sin_ref[...]] * nb, axis=0)
    first_half = (_iota((rows, width), 1) & half) == 0

    def rot(x):
        partner = jnp.where(first_half, pltpu.roll(x, width - half, 1), pltpu.roll(x, half, 1))
        return x * cos + partner * sin

    pb = jnp.concatenate([t[...] for t in tok_refs], axis=0)
    q = rot(pb[:, 0:width])
    k = rot(pb[:, width:2 * width]) * (RET_DK ** -0.5)
    v = pb[:, 2 * width:2 * width + BR_W]
    g_b = pb[:, 2 * width + BR_W:2 * width + 2 * BR_W]

    mask = mask_ref[...]
    causal = mask > 0.5
    dist = jnp.where(causal, (_iota((rows, rows), 0) - _iota((rows, rows), 1)).astype(F32), 0.0)
    t1 = jnp.sum(mask, axis=-1, keepdims=True)
    for h in range(RET_HEADS):
        log_gamma = math.log1p(-(2.0 ** (-5.0 - h)))
        dmat = jnp.where(causal, jnp.exp(dist * log_gamma), 0.0)
        q_h = q[:, h * RET_DK:(h + 1) * RET_DK]
        k_h = k[:, h * RET_DK:(h + 1) * RET_DK]
        v_h = v[:, h * RET_DV:(h + 1) * RET_DV]
        k_dec = k_h * jnp.exp((c - t1) * log_gamma)
        from_state = []
        for j in range(nb):
            js = slice(j * c, (j + 1) * c)
            s_h = s_ref[j, h]
            from_state.append(_mm(q_h[js], s_h))
            s_ref[j, h] = math.exp(c * log_gamma) * s_h + _mm_tn(k_dec[js], v_h[js])
        o_h = _mm(_mm_nt(q_h, k_h) * dmat, v_h) + jnp.exp(t1 * log_gamma) * jnp.concatenate(from_state, axis=0)
        o_n = o_h * lax.rsqrt(jnp.mean(o_h * o_h, axis=-1, keepdims=True) + NORM_EPS)
        o_ref[:, h * RET_DV:(h + 1) * RET_DV] = (o_n * _silu(g_b[:, h * RET_DV:(h + 1) * RET_DV])).astype(o_ref.dtype)


def _ret(pb, cos, sin, ret0, layer, *, o_prev, batch, seq, c, nb, row_block0, stacked):
    nc = seq // c
    assert batch % nb == 0
    width = RET_HEADS * RET_DK
    mask = jnp.asarray(_block_masks(nb, c)[0])
    io_specs, out_shape, out_specs, aliases, aliased = _mixer_io(
        RET_COLS, [(RET_HEADS, RET_DK, RET_DV)], o_prev=o_prev, batch=batch, c=c, nb=nb, nc=nc,
        row_block0=row_block0, stacked=stacked)
    return pl.pallas_call(
        functools.partial(_ret_kernel, c=c, nb=nb, n_alias=len(aliased)),
        out_shape=out_shape,
        grid=(batch // nb, nc),
        input_output_aliases=aliases,
        in_specs=[*io_specs,
                  pl.BlockSpec((c, width), lambda b, ci: (ci, 0)),
                  pl.BlockSpec((c, width), lambda b, ci: (ci, 0)),
                  pl.BlockSpec((nb * c, nb * c), lambda b, ci: (0, 0)),
                  pl.BlockSpec((None, nb, RET_HEADS, RET_DK, RET_DV), lambda b, ci: (layer, b, 0, 0, 0))],
        out_specs=out_specs,
        compiler_params=_cparams(2),
        name="retention",
    )(*aliased, pb, cos, sin, mask, ret0)


def _hgrn_kernel(*refs, c, m, nb, layer, n_chunks, n_alias):
    refs = refs[n_alias:]
    tok_refs = refs[:1]
    lbp_ref, ng_ref, tri_ref, s0_ref, o_ref, s_ref, st_scr = refs[1:]
    ci = pl.program_id(1)

    @pl.when(ci == 0)
    def _init():
        for j in range(nb):
            for h in range(HG_HEADS):
                st_scr[j, h] = s0_ref[j, h].T

    lbp = lbp_ref[...]
    e = jnp.exp(lbp - jnp.max(lbp, axis=0, keepdims=True))
    soft = e / jnp.sum(e, axis=0, keepdims=True)
    lb = jnp.zeros((1, BR_W), F32)
    for j in range(1, layer + 1):
        lb = lb + soft[j:j + 1, :]

    pc = jnp.concatenate([t[...] for t in tok_refs], axis=0)
    q_all = _silu(pc[:, 0:BR_W])
    f_in = pc[:, BR_W:2 * BR_W]
    v_all = pc[:, 2 * BR_W:3 * BR_W]
    g_c = pc[:, 3 * BR_W:4 * BR_W]
    log_f = jnp.log(lb + (1.0 - lb) * _sigmoid(f_in))
    k_all = (1.0 - lb) * _sigmoid(-f_in)
    cum_all = _mm_mask(tri_ref[...], log_f)

    col_m = _iota((m, m), 1)
    row_l = _iota((m, LANES), 0)
    outs = {}
    for j in range(nb):
        js = slice(j * c, (j + 1) * c)
        q, k, v, cum = q_all[js], k_all[js], v_all[js], cum_all[js]
        cum_last = cum[c - 1:c, :]
        q_e = q * jnp.exp(cum)
        k_e = k * jnp.exp(cum_last - cum)
        p_end = jnp.exp(cum_last)
        for h in range(HG_HEADS):
            sl = slice(h * HG_D, (h + 1) * HG_D)
            st = st_scr[j, h]
            o_rows = []
            for i in range(c // m):
                r0 = i * m
                q_i = q[r0:r0 + m, sl]
                c_i = cum[r0:r0 + m, sl]
                k_i = k[r0:r0 + m, sl]
                diag = jnp.zeros((m, m), F32)
                for s in range(m):
                    keep = row_l >= s
                    decay = jnp.exp(jnp.where(keep, c_i - c_i[s:s + 1, :], 0.0))
                    col = jnp.sum(jnp.where(keep, q_i * k_i[s:s + 1, :] * decay, 0.0), axis=-1, keepdims=True)
                    diag = jnp.where(col_m == s, col, diag)
                o_i = _mm(diag, v[r0:r0 + m, sl])
                if i > 0:
                    c_ref = cum[r0 - 1:r0, sl]
                    q_s = q_i * jnp.exp(c_i - c_ref)
                    k_s = k[0:r0, sl] * jnp.exp(c_ref - cum[0:r0, sl])
                    o_i = o_i + _mm(_mm_nt(q_s, k_s), v[0:r0, sl])
                o_rows.append(o_i)
            outs[(j, h)] = _mm_nt(q_e[:, sl], st) + jnp.concatenate(o_rows, axis=0)
            st_scr[j, h] = st * p_end[:, sl] + _mm_tn(v[:, sl], k_e[:, sl])
    for h in range(HG_HEADS):
        sl = slice(h * HG_D, (h + 1) * HG_D)
        o_h = jnp.concatenate([outs[(j, h)] for j in range(nb)], axis=0)
        o_n = o_h * lax.rsqrt(jnp.mean(o_h * o_h, axis=-1, keepdims=True) + NORM_EPS) * ng_ref[:, sl]
        o_ref[:, sl] = (o_n * _silu(g_c[:, sl])).astype(o_ref.dtype)

    @pl.when(ci == n_chunks - 1)
    def _fin():
        for j in range(nb):
            for h in range(HG_HEADS):
                s_ref[j, h] = st_scr[j, h].T


def _hgrn(pc, lbp, ng, hg0, state_layer, layer, *, o_prev, batch, seq, c, nb, row_block0, stacked):
    nc = seq // c
    assert batch % nb == 0
    m = 16 if c % 16 == 0 else SUBLANES
    depth = lbp.shape[0]
    tri = jnp.asarray(_block_masks(nb, c)[0]).astype(BF16)
    io_specs, out_shape, out_specs, aliases, aliased = _mixer_io(
        HG_COLS, [(HG_HEADS, HG_D, HG_D)], o_prev=o_prev, batch=batch, c=c, nb=nb, nc=nc,
        row_block0=row_block0, stacked=stacked)
    return pl.pallas_call(
        functools.partial(_hgrn_kernel, c=c, m=m, nb=nb, layer=layer, n_chunks=nc, n_alias=len(aliased)),
        out_shape=out_shape,
        grid=(batch // nb, nc),
        input_output_aliases=aliases,
        in_specs=[*io_specs,
                  pl.BlockSpec((depth, BR_W), lambda b, ci: (0, 0)),
                  pl.BlockSpec((1, BR_W), lambda b, ci: (0, 0)),
                  pl.BlockSpec((nb * c, nb * c), lambda b, ci: (0, 0)),
                  pl.BlockSpec((None, nb, HG_HEADS, HG_D, HG_D), lambda b, ci: (state_layer, b, 0, 0, 0))],
        out_specs=out_specs,
        scratch_shapes=[pltpu.VMEM((nb, HG_HEADS, HG_D, HG_D), F32)],
        compiler_params=_cparams(2),
        name="hgrn2",
    )(*aliased, pc, lbp, ng, tri, hg0)


def _gdn_kernel(*refs, c, nb, ns, n_alias):
    refs = refs[n_alias:]
    tok_refs = refs[:1]
    (cs0_ref, s0_ref, cw_ref, alog_ref, dtb_ref, ng_ref, mc_ref, mst_ref, tri_ref,
     o_ref, s_ref, cs_ref, ext_scr) = refs[1:]
    ci = pl.program_id(1)
    keep = GD_CONV - 1

    @pl.when(ci == 0)
    def _init():
        for j in range(nb):
            ext_scr[j, SUBLANES - keep:SUBLANES, :] = cs0_ref[j]
        s_ref[...] = s0_ref[...]

    convs = []
    for j in range(nb):
        x = tok_refs[0][j * c:(j + 1) * c, 0:GD_QKV]
        ext_scr[j, SUBLANES:SUBLANES + c, :] = x
        conv = ext_scr[j, SUBLANES - keep:SUBLANES - keep + c, :] * cw_ref[0:1, :]
        for i in range(1, keep):
            conv = conv + ext_scr[j, SUBLANES - keep + i:SUBLANES - keep + i + c, :] * cw_ref[i:i + 1, :]
        convs.append(conv + x * cw_ref[keep:keep + 1, :])
        tail = ext_scr[j, SUBLANES + c - keep:SUBLANES + c, :]
        ext_scr[j, SUBLANES - keep:SUBLANES, :] = tail
        cs_ref[j] = tail

    act = _silu(jnp.concatenate(convs, axis=0))
    q = act[:, 0:BR_W]
    k = act[:, BR_W:2 * BR_W]
    v = act[:, 2 * BR_W:3 * BR_W]
    gate_ba = jnp.concatenate([t[:, GD_QKV:GD_COLS_PAD] for t in tok_refs], axis=0)
    g_d = gate_ba[:, 0:BR_W]
    ba = gate_ba[:, BR_W:BR_W + LANES]
    beta_all = _sigmoid(ba)
    g_all = -jnp.exp(alog_ref[...]) * _softplus(ba + dtb_ref[...])

    cum_all = _mm_mask(tri_ref[...], g_all)

    ur = ns * c
    rows = GD_HEADS * ur
    stack = lambda parts: jnp.concatenate(parts, axis=0)
    l2n = lambda z: z * lax.rsqrt(jnp.sum(z * z, axis=-1, keepdims=True) + 1e-6)
    head = lambda z, h: z[:, h * GD_D:(h + 1) * GD_D]
    qn = jnp.concatenate([l2n(head(q, h)) for h in range(GD_HEADS)], axis=1) * (GD_D ** -0.5)
    kn = jnp.concatenate([l2n(head(k, h)) for h in range(GD_HEADS)], axis=1)
    units = range(nb // ns)
    heads = range(GD_HEADS)
    stack_u = lambda z, un: stack([head(z[un * ur:(un + 1) * ur], h) for h in heads])
    col_u = lambda z, lane0, un: stack([z[un * ur:(un + 1) * ur, lane0 + h:lane0 + h + 1] for h in heads])
    q4 = [stack_u(qn, un) for un in units]
    k4 = [stack_u(kn, un) for un in units]
    v4 = [stack_u(v, un) for un in units]
    beta = [col_u(beta_all, 0, un) for un in units]
    cum = [col_u(cum_all, GD_HEADS, un) for un in units]
    causal = mc_ref[...] > 0.5
    m_strict = mst_ref[...]
    eye = _iota((rows, rows), 0) == _iota((rows, rows), 1)
    ones = jnp.ones((rows, rows), BF16)
    cum_t = [jnp.broadcast_to(cm, (rows, rows)) for cm in cum]
    cum_s = [_mm_mask(ones, jnp.where(eye, ct, 0.0)) for ct in cum_t]
    dmat = [jnp.where(causal, jnp.exp(jnp.where(causal, ct - cs_, 0.0)), 0.0) for ct, cs_ in zip(cum_t, cum_s)]
    a_mat = [beta[un] * _mm_nt(k4[un], k4[un]) * (dmat[un] * m_strict) for un in units]
    inv = _unit_lower_inverse([-a for a in a_mat], c)
    e_cum = [jnp.exp(cm) for cm in cum]
    sol = [_mm(inv[un], jnp.concatenate([(beta[un] * e_cum[un]) * k4[un], beta[un] * v4[un]], axis=1)) for un in units]
    blocks = [(h, i) for h in heads for i in range(ns)]
    rows_of = lambda h, i: slice((h * ns + i) * c, (h * ns + i + 1) * c)
    qw = [[_mm(jnp.concatenate([q4[un][rows_of(h, i)], sol[un][rows_of(h, i), 0:GD_D]], axis=0), s_ref[un * ns + i, h])
           for h, i in blocks] for un in units]
    delta = [sol[un][:, GD_D:2 * GD_D] - stack([z[c:2 * c] for z in qw[un]]) for un in units]
    o4 = [e_cum[un] * stack([z[0:c] for z in qw[un]]) + _mm(_mm_nt(q4[un], k4[un]) * dmat[un], delta[un])
          for un in units]
    for un in units:
        for h, i in blocks:
            hs = rows_of(h, i)
            cum_h = cum[un][hs]
            cum_last = cum_h[c - 1:c, :]
            s_ref[un * ns + i, h] = (jnp.exp(cum_last) * s_ref[un * ns + i, h]
                                     + _mm_tn(k4[un][hs] * jnp.exp(cum_last - cum_h), delta[un][hs]))
    for h in heads:
        sl = slice(h * GD_D, (h + 1) * GD_D)
        o_h = stack([o4[un][h * ur:(h + 1) * ur] for un in units])
        o_n = o_h * lax.rsqrt(jnp.mean(o_h * o_h, axis=-1, keepdims=True) + NORM_EPS) * ng_ref[:, sl]
        o_ref[:, sl] = (o_n * _silu(g_d[:, sl])).astype(o_ref.dtype)


def _gdn(pd, conv0, gd0, layer, cw, alog, dtb, ng, *, o_prev, batch, seq, c, nb, ns, row_block0, stacked):
    nc = seq // c
    assert batch % nb == 0 and nb % ns == 0
    keep = GD_CONV - 1
    rows = GD_HEADS * ns * c
    m_causal, m_strict = (jnp.asarray(m) for m in _block_masks(GD_HEADS * ns, c))
    tri = jnp.asarray(_block_masks(nb, c)[0]).astype(BF16)
    full = lambda *shape: pl.BlockSpec(shape, lambda b, ci: (0,) * len(shape))
    io_specs, out_shape, out_specs, aliases, aliased = _mixer_io(
        GD_COLS_PAD, [(GD_HEADS, GD_D, GD_D), (keep, GD_QKV)], o_prev=o_prev, batch=batch, c=c, nb=nb, nc=nc,
        row_block0=row_block0, stacked=stacked)
    return pl.pallas_call(
        functools.partial(_gdn_kernel, c=c, nb=nb, ns=ns, n_alias=len(aliased)),
        out_shape=out_shape,
        grid=(batch // nb, nc),
        input_output_aliases=aliases,
        in_specs=[*io_specs,
                  pl.BlockSpec((None, nb, keep, GD_QKV), lambda b, ci: (layer, b, 0, 0)),
                  pl.BlockSpec((None, nb, GD_HEADS, GD_D, GD_D), lambda b, ci: (layer, b, 0, 0, 0)),
                  full(GD_CONV, GD_QKV), full(1, LANES), full(1, LANES), full(1, BR_W),
                  full(rows, rows), full(rows, rows), full(nb * c, nb * c)],
        out_specs=out_specs,
        scratch_shapes=[pltpu.VMEM((nb, SUBLANES + c, GD_QKV), F32)],
        compiler_params=_cparams(2),
        name="gdn",
    )(*aliased, pd, conv0, gd0, cw, alog, dtb, ng, m_causal, m_strict, tri)


def _pick_chunk(seq):
    for c in (64, 56, 48, 40, 32, 24, 16, 8):
        if seq % c == 0:
            return c
    raise ValueError(f"sequence length {seq} is not a multiple of {SUBLANES}")


def _pick_tile(n, candidates):
    for t in candidates:
        if n % t == 0:
            return t
    raise ValueError(f"no tile in {candidates} divides {n}")


def _rope_tables(pos):
    half = RET_DK // 2
    inv = ROPE_BASE ** (-jnp.arange(half, dtype=F32) / half)
    ang = pos.astype(F32)[:, None] * inv[None, :]
    cos, sin = jnp.cos(ang), jnp.sin(ang)
    cos_t = jnp.tile(jnp.concatenate([cos, cos], axis=1), (1, RET_HEADS))
    sin_t = jnp.tile(jnp.concatenate([-sin, sin], axis=1), (1, RET_HEADS))
    return cos_t, sin_t


def kernel(x_prompt, x_sample, state_rwkv_wkv, state_rwkv_shift, state_ret, state_hgrn, state_gdn, state_gdn_conv, meta_tokens, norm_mix, w_in, rw_mu, rw_w0, rw_w2, rw_a0, rw_a2, rw_g2, rw_kk, rw_ka, rw_rk, rw_ln_g, rw_ln_b, hg_lb, hg_norm_g, gd_conv, gd_a_log, gd_dt_bias, gd_norm_g, w_branch, w_out, norm_ffn, w_up, w_down, norm_final):
    depth = norm_mix.shape[0]
    bp, tq, d = x_prompt.shape
    tp = tq + N_META
    bs, ts, _ = x_sample.shape
    n_p, n_s = bp * tp, bs * ts
    n_tok = n_p + n_s
    cp, cs = _pick_chunk(tp), _pick_chunk(ts)
    tm = _pick_tile(n_tok, (464, 512, 256, 232, 128, 64, 32, 16, 8))
    tm_in = _pick_tile(n_tok, (928, 464, 512, 256, 232, 128, 64, 32, 16, 8))
    nb_s = _pick_tile(bs, (8, 4, 2, 1))
    nb_p = _pick_tile(bp, (4, 2, 1))
    ncp, gp = tp // cp, bp // nb_p
    assert cs == ts and n_p % (nb_s * ts) == 0

    meta = jnp.broadcast_to(meta_tokens.astype(F32)[None], (bp, N_META, d))
    xp = jnp.concatenate([meta, x_prompt], axis=1).reshape(gp, nb_p, ncp, cp, d)
    x = jnp.concatenate([xp.transpose(0, 2, 1, 3, 4).reshape(n_p, d), x_sample.reshape(n_s, d)], axis=0)

    cos_p, sin_p = _rope_tables(jnp.arange(tp, dtype=jnp.int32))
    cos_s, sin_s = _rope_tables(PAST_LEN + jnp.arange(ts, dtype=jnp.int32))

    head_of = jnp.arange(BR_W) // RW_HEAD
    bd = (head_of[:, None] == head_of[None, :]).astype(BF16)

    zeros = lambda *s: jnp.zeros((1, bp) + s, F32)
    z_wkv, z_shift = zeros(RW_HEADS, RW_HEAD, RW_HEAD), zeros(1, RW_COLS)
    z_ret, z_hg, z_gd = zeros(RET_HEADS, RET_DK, RET_DV), zeros(HG_HEADS, HG_D, HG_D), zeros(GD_HEADS, GD_D, GD_D)
    z_conv = zeros(GD_CONV - 1, GD_QKV)
    shift_s = state_rwkv_shift.reshape(depth, bs, 1, RW_COLS)

    assert w_in.shape[2] == IN_G + GATE_COLS
    w_a, w_b, w_c, w_d, w_g = _w_in_prep(jnp.swapaxes(w_in, 1, 2), LANES)
    wb_all, wo_all = w_branch.astype(BF16), w_out.astype(BF16)
    wu_all, wd_all = w_up.astype(BF16), w_down.astype(BF16)

    tails = ((RW_HEADS, RW_HEAD, RW_HEAD), (1, RW_COLS), (RET_HEADS, RET_DK, RET_DV), (HG_HEADS, HG_D, HG_D),
             (GD_HEADS, GD_D, GD_D), (GD_CONV - 1, GD_QKV))
    stacks = [[jnp.zeros((depth, batch) + tail, F32) for tail in tails] for batch in (bp, bs)]
    h = _rmsnorm(x, norm_mix[0], tm, BF16)
    for l in range(depth):
        pa = _matmul(h, w_a, l, tm_in, 896, "in_proj_a")
        pb = _matmul(h, w_b, l, tm_in, 768, "in_proj_b")
        pc = _matmul(h, w_c, l, tm_in, 1024, "in_proj_c")
        pd = _matmul(h, w_d, l, tm_in, GD_COLS_PAD, "in_proj_d")
        gates = _matmul(h, w_g, l, tm_in, 2048, "in_proj_g", gate=True)

        row1 = lambda a: a.reshape(1, -1)
        pad_rows = lambda a, top: jnp.concatenate(
            [jnp.zeros((top, BR_W), F32), a, jnp.zeros((LANES - top - a.shape[0], BR_W), F32)], axis=0).astype(BF16)
        rw_prm = (row1(rw_mu[l]), row1(rw_w0[l]), pad_rows(rw_w2[l], 0), row1(rw_a0[l]), pad_rows(rw_a2[l], 64),
                  rw_g2[l].astype(BF16), row1(rw_kk[l]), row1(rw_ka[l]), row1(rw_rk[l]), row1(rw_ln_g[l]),
                  row1(rw_ln_b[l]), bd)
        lane_pad = lambda a: jnp.concatenate(
            [jnp.zeros((GD_HEADS,), F32), a, jnp.zeros((LANES - 2 * GD_HEADS,), F32)]).reshape(1, LANES)
        gd_prm = (gd_conv[l], lane_pad(gd_a_log[l]), lane_pad(gd_dt_bias[l]), row1(gd_norm_g[l]))
        hg_g = row1(hg_norm_g[l])

        if l == 0:
            oa = ob = oc = od = jnp.zeros((n_tok, BR_W), BF16)
        for grp, (batch, seq, c, nb, ns, rb0, sts, lay, cos, sin) in enumerate((
                (bp, tp, cp, nb_p, 1, 0, (z_wkv, z_shift, z_ret, z_hg, z_gd, z_conv), 0, cos_p, sin_p),
                (bs, ts, cs, nb_s, nb_s, n_p // (nb_s * ts),
                 (state_rwkv_wkv, shift_s, state_ret, state_hgrn, state_gdn, state_gdn_conv), l, cos_s, sin_s))):
            kw = dict(batch=batch, seq=seq, c=c, nb=nb, row_block0=rb0)
            prev = stacks[grp]
            stk = lambda *arrays: (depth, l, arrays)
            oa, wkv, shift = _rwkv(pa, sts[1], sts[0], lay, rw_prm, o_prev=oa, ns=ns, stacked=stk(*prev[0:2]), **kw)
            ob, ret = _ret(pb, cos, sin, sts[2], lay, o_prev=ob, stacked=stk(*prev[2:3]), **kw)
            oc, hg = _hgrn(pc, hg_lb, hg_g, sts[3], lay, l, o_prev=oc, stacked=stk(*prev[3:4]), **kw)
            od, gd, conv = _gdn(pd, sts[5], sts[4], lay, *gd_prm, o_prev=od, ns=ns, stacked=stk(*prev[4:6]), **kw)
            stacks[grp] = [wkv, shift, ret, hg, gd, conv]
        x, h2 = _merge(x, (oa, ob, oc, od), gates, wb_all, wo_all, norm_ffn[l], l, tm_in, 256)
        last = l == depth - 1
        res = _ffn(x, h2, norm_final if last else norm_mix[l + 1], wu_all, wd_all, l, tm, 512, last)
        if last:
            y = res
        else:
            x, h = res

    y_prompt = y[:n_p].reshape(gp, ncp, nb_p, cp, d).transpose(0, 2, 1, 3, 4).reshape(bp, tp, d)[:, N_META:]
    y_sample = y[n_p:].reshape(bs, ts, d)
    states = []
    for batch, (wkv, shift, ret, hg, gd, conv) in zip((bp, bs), stacks):
        states += [wkv, shift.reshape(depth, batch, RW_COLS), ret, hg, gd, conv]
    return (y_prompt, y_sample, *states)
```
